```python
import math
import jax
import jax.numpy as jnp
from jax import lax
import numpy as np

D_MODEL = 2048
BATCH = 2
SEQ = 8192
DEPTH = 2
DEC_BATCH = 32
DEC_SEQ = 64
PAST_LEN = 4096

CHUNK = 64
Q_BLOCK = 128
MLA_HEADS = 6
MLA_Q_RANK = 512
MLA_KV_RANK = 256
MLA_NOPE = 128
MLA_ROPE = 64
MLA_V = 128
ROPE_THETA = 10000.0
GDN_HEADS = 6
GDN_DK = 128
GDN_DV = 128
GDN_CONV = 4
GDN_KEY_DIM = GDN_HEADS * GDN_DK
GDN_VAL_DIM = GDN_HEADS * GDN_DV
GDN_CONV_DIM = 2 * GDN_KEY_DIM + GDN_VAL_DIM
CB_HEADS = 4
CB_DH = 128
CB_DIM = CB_HEADS * CB_DH
CB_PAST_CHUNKS = 8
CB_PAST_ROWS = CB_PAST_CHUNKS * CHUNK
REL_CLIP = 256
IN_SPLITS = (MLA_Q_RANK, MLA_KV_RANK, MLA_ROPE, GDN_KEY_DIM, GDN_KEY_DIM, GDN_VAL_DIM, GDN_VAL_DIM, GDN_HEADS, GDN_HEADS, 3 * CB_DIM)
IN_WIDTH = MLA_Q_RANK + MLA_KV_RANK + MLA_ROPE + 2 * GDN_KEY_DIM + 2 * GDN_VAL_DIM + 2 * GDN_HEADS + 3 * CB_DIM
MIX_WIDTH = MLA_HEADS * MLA_V + GDN_VAL_DIM + CB_DIM
N_EXPERTS = 16
N_GROUPS = 4
EXPERTS_PER_GROUP = N_EXPERTS // N_GROUPS
TOP_K = 2
EXPERT_FF = 1024
DEEPNORM_ALPHA = (2 * DEPTH) ** 0.25
DEEPNORM_BETA = (8 * DEPTH) ** -0.25

kernel_name = 'hybrid_stream_encoder_step'

F32 = jnp.float32


def rms_norm(x, g, eps=1e-6):
    xf = x.astype(F32)
    y = xf * lax.rsqrt(jnp.mean(xf * xf, -1, keepdims=True) + eps) * g.astype(F32)
    return y.astype(x.dtype)


def layer_norm(x, g, b, eps=1e-5):
    xf = x.astype(F32)
    mu = jnp.mean(xf, -1, keepdims=True)
    var = jnp.mean(jnp.square(xf - mu), -1, keepdims=True)
    return ((xf - mu) * lax.rsqrt(var + eps) * g.astype(F32) + b.astype(F32)).astype(x.dtype)


def l2_norm(x, eps=1e-6):
    xf = x.astype(F32)
    return xf * lax.rsqrt(jnp.sum(xf * xf, -1, keepdims=True) + eps)


def rope_angles(pos):
    inv = ROPE_THETA ** (-jnp.arange(0, MLA_ROPE, 2, dtype=F32) / MLA_ROPE)
    ang = pos.astype(F32)[:, None] * inv[None, :]
    return jnp.cos(ang), jnp.sin(ang)


def apply_rope(x, cos, sin):
    x1, x2 = jnp.split(x.astype(F32), 2, axis=-1)
    return jnp.concatenate([x1 * cos - x2 * sin, x1 * sin + x2 * cos], -1).astype(x.dtype)


def mla_attend(q_lat, q_rope, ckv, krope, q_pos):
    k_chunk = jnp.arange(ckv.shape[1]) // CHUNK
    scale = (MLA_NOPE + MLA_ROPE) ** -0.5

    def block(args):
        ql, qr, qp = args
        s = (jnp.einsum('bthc,bsc->bhts', ql, ckv) + jnp.einsum('bthr,bsr->bhts', qr, krope)).astype(F32) * scale
        mask = k_chunk[None, :] <= (qp // CHUNK)[:, None]
        p = jax.nn.softmax(jnp.where(mask, s, -jnp.inf), axis=-1).astype(ckv.dtype)
        return jnp.einsum('bhts,bsc->bthc', p, ckv)

    b, t = q_lat.shape[0], q_lat.shape[1]
    if t <= Q_BLOCK:
        return block((q_lat, q_rope, q_pos))
    nb = t // Q_BLOCK

    def split(a):
        return a.reshape((b, nb, Q_BLOCK) + a.shape[2:]).swapaxes(0, 1)

    out = lax.map(block, (split(q_lat), split(q_rope), q_pos.reshape(nb, Q_BLOCK)))
    return out.swapaxes(0, 1).reshape((b, t) + out.shape[3:])


def gated_delta(q, k, v, g, beta, s0):
    b, t, h, dk = q.shape
    dv = v.shape[-1]
    L = min(CHUNK, t)
    n = t // L

    def blk(a):
        a = a.astype(F32).reshape((b, n, L) + a.shape[2:])
        return jnp.moveaxis(a, (1, 2), (0, 3))

    q, k, v = blk(q) * (dk ** -0.5), blk(k), blk(v)
    g, beta = blk(g), blk(beta)
    G = jnp.cumsum(g, axis=-1)
    incl = jnp.tril(jnp.ones((L, L), bool))
    strict = jnp.tril(jnp.ones((L, L), bool), -1)
    dmat = jnp.exp(jnp.where(incl, G[..., :, None] - G[..., None, :], -jnp.inf))
    a_mat = jnp.where(strict, beta[..., None] * jnp.einsum('nbhid,nbhjd->nbhij', k, k) * dmat, 0.0)
    rhs = jnp.concatenate([beta[..., None] * v, (beta * jnp.exp(G))[..., None] * k], -1)
    sol = lax.linalg.triangular_solve(a_mat + jnp.eye(L, dtype=F32), rhs, left_side=True, lower=True, unit_diagonal=True)
    u, w = sol[..., :dv], sol[..., dv:]
    qk = jnp.einsum('nbhid,nbhjd->nbhij', q, k) * dmat

    def step(S, xs):
        u_c, w_c, q_c, k_c, G_c, qk_c = xs
        delta = u_c - jnp.einsum('bhlk,bhkv->bhlv', w_c, S)
        o = jnp.exp(G_c)[..., None] * jnp.einsum('bhlk,bhkv->bhlv', q_c, S) + jnp.einsum('bhij,bhjv->bhiv', qk_c, delta)
        G_last = G_c[..., -1:]
        S = jnp.exp(G_last)[..., None] * S + jnp.einsum('bhlk,bhlv->bhkv', k_c * jnp.exp(G_last - G_c)[..., None], delta)
        return S, o

    S, o = lax.scan(step, s0.astype(F32), (u, w, q, k, G, qk))
    o = jnp.moveaxis(o, (0, 3), (1, 2)).reshape(b, t, h, dv)
    return o, S


def chunk_band_attend(q, k, v, k_past, v_past, rel_bias):
    b, t, h, dh = q.shape
    p_rows = k_past.shape[1]
    L = min(CHUNK, t)
    n = t // L
    width = CB_PAST_ROWS + L
    pad = CB_PAST_ROWS - p_rows
    zeros = jnp.zeros((b, pad, h, dh), k.dtype)
    kp = jnp.concatenate([zeros, k_past, k], 1)
    vp = jnp.concatenate([zeros, v_past, v], 1)
    band = (jnp.arange(n) * L)[:, None] + jnp.arange(width)[None, :]
    kb, vb = kp[:, band], vp[:, band]
    qc = q.reshape(b, n, L, h, dh)
    s = jnp.einsum('bnlhd,bnwhd->bnhlw', qc, kb).astype(F32) * (dh ** -0.5)
    rel = CB_PAST_ROWS + jnp.arange(L)[:, None] - jnp.arange(width)[None, :]
    bias = rel_bias[:, jnp.clip(rel, -REL_CLIP, REL_CLIP) + REL_CLIP].astype(F32)
    valid = band >= pad
    s = jnp.where(valid[None, :, None, None, :], s + bias[None, None], -jnp.inf)
    p = jax.nn.softmax(s, axis=-1).astype(v.dtype)
    return jnp.einsum('bnhlw,bnwhd->bnlhd', p, vb).reshape(b, t, h * dh)


def token_mixers(x, ckv_past, krope_past, s_past, conv_past, cbk_past, cbv_past,
                 w_in, q_norm_g, w_uq, kv_norm_g, w_uk, w_uv, conv_w, a_log, dt_bias,
                 gdn_norm_g, rel_bias, w_out):
    b, t, _ = x.shape
    start = ckv_past.shape[1]
    proj = x @ w_in
    cq, ckv, kr, gq, gk, gv, gz, gb, ga, cqkv = jnp.split(proj, np.cumsum(IN_SPLITS)[:-1], axis=-1)

    pos = start + jnp.arange(t, dtype=jnp.int32)
    cos, sin = rope_angles(pos)
    q = jnp.einsum('btr,rhe->bthe', rms_norm(cq, q_norm_g), w_uq)
    q_nope = q[..., :MLA_NOPE]
    q_rope = apply_rope(q[..., MLA_NOPE:], cos[:, None], sin[:, None])
    ckv_new = rms_norm(ckv, kv_norm_g)
    krope_new = apply_rope(kr, cos, sin)
    q_lat = jnp.einsum('bthn,chn->bthc', q_nope, w_uk)
    o_lat = mla_attend(q_lat, q_rope, jnp.concatenate([ckv_past, ckv_new], 1),
                       jnp.concatenate([krope_past, krope_new], 1), pos)
    o_a = jnp.einsum('bthc,chv->bthv', o_lat, w_uv).reshape(b, t, MLA_HEADS * MLA_V)

    xpad = jnp.concatenate([conv_past, jnp.concatenate([gq, gk, gv], -1)], 1)
    conv = jax.nn.silu(sum(xpad[:, i:i + t] * conv_w[i] for i in range(GDN_CONV)))
    c_q, c_k, c_v = jnp.split(conv, [GDN_KEY_DIM, 2 * GDN_KEY_DIM], axis=-1)
    beta = jax.nn.sigmoid(gb.astype(F32))
    g = -jnp.exp(a_log.astype(F32)) * jax.nn.softplus((ga + dt_bias).astype(F32))
    o, s_new = gated_delta(l2_norm(c_q.reshape(b, t, GDN_HEADS, GDN_DK)), l2_norm(c_k.reshape(b, t, GDN_HEADS, GDN_DK)),
                           c_v.reshape(b, t, GDN_HEADS, GDN_DV), g, beta, s_past)
    o_b = (rms_norm(o, gdn_norm_g) * jax.nn.silu(gz.reshape(b, t, GDN_HEADS, GDN_DV).astype(F32)))
    o_b = o_b.astype(x.dtype).reshape(b, t, GDN_VAL_DIM)

    cb_q, cb_k, cb_v = [a.reshape(b, t, CB_HEADS, CB_DH) for a in jnp.split(cqkv, 3, axis=-1)]
    o_c = chunk_band_attend(cb_q, cb_k, cb_v, cbk_past, cbv_past, rel_bias)

    y = jnp.concatenate([o_a, o_b, o_c], -1) @ w_out
    keep = min(CB_PAST_ROWS, t)
    return y, (ckv_new, krope_new, s_new.astype(x.dtype), xpad[:, -(GDN_CONV - 1):], cb_k[:, -keep:], cb_v[:, -keep:])


def grouped_moe(x, router_w, router_b, w_gate, w_up, w_down):
    b, t, d = x.shape
    xt = x.reshape(b * t, d)
    s = jax.nn.sigmoid((xt @ router_w).astype(F32))
    sb = s + router_b.astype(F32)
    grp_score = lax.top_k(sb.reshape(-1, N_GROUPS, EXPERTS_PER_GROUP), TOP_K)[0].sum(-1)
    gsel = jnp.argmax(grp_score, axis=-1)
    in_grp = (jnp.arange(N_EXPERTS) // EXPERTS_PER_GROUP)[None, :] == gsel[:, None]
    _, idx = lax.top_k(jnp.where(in_grp, sb, -jnp.inf), TOP_K)
    wsel = jnp.take_along_axis(s, idx, axis=-1)
    wsel = wsel / jnp.sum(wsel, -1, keepdims=True)
    gate = jnp.sum(jax.nn.one_hot(idx, N_EXPERTS, dtype=F32) * wsel[..., None], axis=1).astype(x.dtype)
    y = jnp.zeros_like(xt)
    for e in range(N_EXPERTS):
        h = jax.nn.silu(xt @ w_gate[e]) * (xt @ w_up[e])
        y = y + gate[:, e:e + 1] * (h @ w_down[e])
    return y.reshape(b, t, d)


def setup_inputs(seed: int = 0) -> dict:
    key = jax.random.key(seed)
    ks = iter(jax.random.split(key, 40))
    nrm = lambda shape, scale: jax.random.normal(next(ks), shape, F32) * scale
    gain = lambda shape: 1.0 + 0.02 * jax.random.normal(next(ks), shape, F32)
    cb_rows = min(CB_PAST_ROWS, PAST_LEN)
    dt = jnp.exp(jax.random.uniform(next(ks), (DEPTH, GDN_HEADS), F32, math.log(1e-3), math.log(1e-1)))
    return {
        'x_prompt': nrm((BATCH, SEQ, D_MODEL), 1.0),
        'x_sample': nrm((DEC_BATCH, DEC_SEQ, D_MODEL), 1.0),
        'cache_mla_ckv': nrm((DEPTH, DEC_BATCH, PAST_LEN, MLA_KV_RANK), 1.0),
        'cache_mla_krope': nrm((DEPTH, DEC_BATCH, PAST_LEN, MLA_ROPE), 1.0),
        'state_gdn': nrm((DEPTH, DEC_BATCH, GDN_HEADS, GDN_DK, GDN_DV), GDN_DK ** -0.5),
        'state_gdn_conv': nrm((DEPTH, DEC_BATCH, GDN_CONV - 1, GDN_CONV_DIM), 1.0),
        'cache_cb_k': nrm((DEPTH, DEC_BATCH, cb_rows, CB_HEADS, CB_DH), 1.0),
        'cache_cb_v': nrm((DEPTH, DEC_BATCH, cb_rows, CB_HEADS, CB_DH), 1.0),
        'w_in': nrm((DEPTH, D_MODEL, IN_WIDTH), D_MODEL ** -0.5),
        'q_norm_g': gain((DEPTH, MLA_Q_RANK)),
        'w_uq': nrm((DEPTH, MLA_Q_RANK, MLA_HEADS, MLA_NOPE + MLA_ROPE), MLA_Q_RANK ** -0.5),
        'kv_norm_g': gain((DEPTH, MLA_KV_RANK)),
        'w_uk': nrm((DEPTH, MLA_KV_RANK, MLA_HEADS, MLA_NOPE), MLA_KV_RANK ** -0.5),
        'w_uv': nrm((DEPTH, MLA_KV_RANK, MLA_HEADS, MLA_V), MLA_KV_RANK ** -0.5 * DEEPNORM_BETA),
        'conv_w': nrm((DEPTH, GDN_CONV, GDN_CONV_DIM), GDN_CONV ** -0.5),
        'a_log': jnp.log(jax.random.uniform(next(ks), (DEPTH, GDN_HEADS), F32, 1.0, 16.0)),
        'dt_bias': dt + jnp.log(-jnp.expm1(-dt)),
        'gdn_norm_g': gain((DEPTH, GDN_DV)),
        'rel_bias': nrm((DEPTH, CB_HEADS, 2 * REL_CLIP + 1), 0.2),
        'w_out': nrm((DEPTH, MIX_WIDTH, D_MODEL), MIX_WIDTH ** -0.5 * DEEPNORM_BETA),
        'ln1_g': gain((DEPTH, D_MODEL)),
        'ln1_b': nrm((DEPTH, D_MODEL), 0.02),
        'router_w': nrm((D_MODEL, N_EXPERTS), D_MODEL ** -0.5),
        'router_b': nrm((N_EXPERTS,), 0.01),
        'w_gate': nrm((DEPTH, N_EXPERTS, D_MODEL, EXPERT_FF), D_MODEL ** -0.5),
        'w_up': nrm((DEPTH, N_EXPERTS, D_MODEL, EXPERT_FF), D_MODEL ** -0.5),
        'w_down': nrm((DEPTH, N_EXPERTS, EXPERT_FF, D_MODEL), EXPERT_FF ** -0.5 * DEEPNORM_BETA),
        'ln2_g': gain((DEPTH, D_MODEL)),
        'ln2_b': nrm((DEPTH, D_MODEL), 0.02),
    }


def reference(x_prompt, x_sample, cache_mla_ckv, cache_mla_krope, state_gdn, state_gdn_conv, cache_cb_k, cache_cb_v,
              w_in, q_norm_g, w_uq, kv_norm_g, w_uk, w_uv, conv_w, a_log, dt_bias, gdn_norm_g, rel_bias, w_out,
              ln1_g, ln1_b, router_w, router_b, w_gate, w_up, w_down, ln2_g, ln2_b):
    def run_trunk(x, ckv_c, krope_c, s_c, conv_c, cbk_c, cbv_c):
        new = ([], [], [], [], [], [])
        for l in range(DEPTH):
            mix, st = token_mixers(x, ckv_c[l], krope_c[l], s_c[l], conv_c[l], cbk_c[l], cbv_c[l],
                                   w_in[l], q_norm_g[l], w_uq[l], kv_norm_g[l], w_uk[l], w_uv[l], conv_w[l],
                                   a_log[l], dt_bias[l], gdn_norm_g[l], rel_bias[l], w_out[l])
            x = layer_norm(DEEPNORM_ALPHA * x + mix, ln1_g[l], ln1_b[l])
            x = layer_norm(DEEPNORM_ALPHA * x + grouped_moe(x, router_w, router_b, w_gate[l], w_up[l], w_down[l]),
                           ln2_g[l], ln2_b[l])
            for lst, a in zip(new, st):
                lst.append(a)
        return (x, *[jnp.stack(a) for a in new])

    bp, dtp = x_prompt.shape[0], x_prompt.dtype
    empty = lambda *s: jnp.zeros((DEPTH, bp) + s, dtp)
    y_prompt, p_ckv, p_krope, p_gdn, p_conv, p_cb_k, p_cb_v = run_trunk(
        x_prompt, empty(0, MLA_KV_RANK), empty(0, MLA_ROPE), empty(GDN_HEADS, GDN_DK, GDN_DV),
        empty(GDN_CONV - 1, GDN_CONV_DIM), empty(0, CB_HEADS, CB_DH), empty(0, CB_HEADS, CB_DH))
    y_sample, s_ckv, s_krope, s_gdn, s_conv, s_cb_k, s_cb_v = run_trunk(
        x_sample, cache_mla_ckv, cache_mla_krope, state_gdn, state_gdn_conv, cache_cb_k, cache_cb_v)
    return (y_prompt, y_sample, p_ckv, p_krope, p_gdn, p_conv, p_cb_k, p_cb_v,
            s_ckv, s_krope, s_gdn, s_conv, s_cb_k, s_cb_v)
```

```python
import functools
import math

import jax
import jax.numpy as jnp
import numpy as np
from jax import lax
from jax.experimental import pallas as pl
from jax.experimental.pallas import tpu as pltpu

F32 = jnp.float32
BF16 = jnp.bfloat16

CHUNK = 64
MLA_HEADS = 6
MLA_Q_RANK = 512
MLA_KV_RANK = 256
MLA_NOPE = 128
MLA_ROPE = 64
MLA_V = 128
ROPE_THETA = 10000.0
GDN_HEADS = 6
GDN_DK = 128
GDN_DV = 128
GDN_CONV = 4
GDN_KEY_DIM = GDN_HEADS * GDN_DK
GDN_VAL_DIM = GDN_HEADS * GDN_DV
GDN_CONV_DIM = 2 * GDN_KEY_DIM + GDN_VAL_DIM
CB_HEADS = 4
CB_DH = 128
CB_DIM = CB_HEADS * CB_DH
CB_PAST_ROWS = 8 * CHUNK
REL_CLIP = 256
N_EXPERTS = 16
N_GROUPS = 4
EXPERTS_PER_GROUP = N_EXPERTS // N_GROUPS
DEPTH = 2
DEEPNORM_ALPHA = (2 * DEPTH) ** 0.25

LANES = 128
C_GQKV = 0
C_GZ = C_GQKV + GDN_CONV_DIM
C_CQ = C_GZ + GDN_VAL_DIM
C_CKV = C_CQ + MLA_Q_RANK
C_KR = C_CKV + MLA_KV_RANK
C_GBA = C_KR + 2 * MLA_ROPE
C_CB = C_GBA + LANES
IN_PAD = C_CB + 3 * CB_DIM

VMEM_LIMIT = 56 * 1024 * 1024


def _cparams(sem):
    return pltpu.CompilerParams(dimension_semantics=sem, vmem_limit_bytes=VMEM_LIMIT)


def _dot(a, b):
    return jnp.dot(a, b, preferred_element_type=F32)


def _dot_nt(a, b):
    return lax.dot_general(a, b, (((1,), (1,)), ((), ())), preferred_element_type=F32)


def _dot_hi(a, b):
    return jnp.dot(a, b, preferred_element_type=F32, precision=lax.Precision.HIGHEST)


def _dot_nt_hi(a, b):
    return lax.dot_general(a, b, (((1,), (1,)), ((), ())), preferred_element_type=F32,
                           precision=lax.Precision.HIGHEST)


def _sigmoid(x):
    return 1.0 / (1.0 + jnp.exp(-x))


def _silu(x):
    return x * _sigmoid(x)


def _layer_norm(h, g, b, eps=1e-5):
    mu = jnp.mean(h, -1, keepdims=True)
    d = h - mu
    var = jnp.mean(d * d, -1, keepdims=True)
    return d * lax.rsqrt(var + eps) * g + b


def _rms_norm(x, g, eps=1e-6):
    return x * lax.rsqrt(jnp.mean(x * x, -1, keepdims=True) + eps) * g


def _inproj_kernel(x_ref, w_ref, o_ref, xb_ref):
    @pl.when(pl.program_id(1) == 0)
    def _():
        xb_ref[...] = x_ref[...].astype(BF16)

    o_ref[...] = _dot(xb_ref[...], w_ref[...])


def _in_proj(x2d, w):
    n, d = x2d.shape
    width = w.shape[1]
    tm = min(1024, n)
    tn = 512
    return pl.pallas_call(
        _inproj_kernel,
        grid=(n // tm, width // tn),
        in_specs=[pl.BlockSpec((tm, d), lambda i, j: (i, 0)),
                  pl.BlockSpec((d, tn), lambda i, j: (0, j))],
        out_specs=pl.BlockSpec((tm, tn), lambda i, j: (i, j)),
        out_shape=jax.ShapeDtypeStruct((n, width), F32),
        scratch_shapes=[pltpu.VMEM((tm, d), BF16)],
        compiler_params=_cparams(("parallel", "arbitrary")),
        name="in_proj",
    )(x2d, w)


def _mla_prep_kernel(cq_ref, ckv_ref, kr_ref, cosq_ref, sinq_ref, cosk_ref, sink_ref,
                     qg_ref, kvg_ref, wqn_ref, wqr_ref, wqs_ref, wuk_ref,
                     q_ref, ckvn_ref, krn_ref):
    cqn = _rms_norm(cq_ref[0], qg_ref[...]).astype(BF16)
    q_nope = _dot(cqn, wqn_ref[...]).astype(BF16)
    q_rope = _dot(cqn, wqr_ref[...])
    q_rope_sw = _dot(cqn, wqs_ref[...])
    q_rot = (q_rope * cosq_ref[...] + q_rope_sw * sinq_ref[...]).astype(BF16)
    for h in range(MLA_HEADS):
        q_lat = _dot(q_nope[:, h * MLA_NOPE:(h + 1) * MLA_NOPE], wuk_ref[h])
        q_ref[0, h, :, 0:MLA_KV_RANK] = q_lat.astype(BF16)
        q_ref[0, h, :, MLA_KV_RANK:] = q_rot[:, h * MLA_ROPE:(h + 1) * MLA_ROPE]
    ckvn_ref[0] = _rms_norm(ckv_ref[0], kvg_ref[...])
    kr = kr_ref[0]
    krn_ref[0] = kr[:, :MLA_ROPE] * cosk_ref[...] + kr[:, MLA_ROPE:] * sink_ref[...]


def _mla_prep(proj, cos2, sin2, q_norm_g, kv_norm_g, wq_nope, wq_rope, wq_rope_sw, wuk_t):
    b, t, _ = proj.shape
    tm = min(512, t)
    cosq = jnp.tile(cos2, (1, MLA_HEADS))
    sinq = jnp.tile(sin2, (1, MLA_HEADS))
    full = lambda a: pl.BlockSpec(a.shape, lambda bi, i: (0,) * a.ndim)
    row = lambda w: pl.BlockSpec((tm, w), lambda bi, i: (i, 0))
    qd = MLA_KV_RANK + MLA_ROPE
    return pl.pallas_call(
        _mla_prep_kernel,
        grid=(b, t // tm),
        in_specs=[pl.BlockSpec((1, tm, MLA_Q_RANK), lambda bi, i: (bi, i, C_CQ // MLA_Q_RANK)),
                  pl.BlockSpec((1, tm, MLA_KV_RANK), lambda bi, i: (bi, i, C_CKV // MLA_KV_RANK)),
                  pl.BlockSpec((1, tm, 2 * MLA_ROPE), lambda bi, i: (bi, i, C_KR // (2 * MLA_ROPE))),
                  row(MLA_HEADS * MLA_ROPE), row(MLA_HEADS * MLA_ROPE), row(MLA_ROPE), row(MLA_ROPE),
                  full(q_norm_g), full(kv_norm_g), full(wq_nope), full(wq_rope), full(wq_rope_sw),
                  full(wuk_t)],
        out_specs=[pl.BlockSpec((1, MLA_HEADS, tm, qd), lambda bi, i: (bi, 0, i, 0)),
                   pl.BlockSpec((1, tm, MLA_KV_RANK), lambda bi, i: (bi, i, 0)),
                   pl.BlockSpec((1, tm, MLA_ROPE), lambda bi, i: (bi, i, 0))],
        out_shape=[jax.ShapeDtypeStruct((b, MLA_HEADS, t, qd), BF16),
                   jax.ShapeDtypeStruct((b, t, MLA_KV_RANK), F32),
                   jax.ShapeDtypeStruct((b, t, MLA_ROPE), F32)],
        compiler_params=_cparams(("parallel", "parallel")),
        name="mla_prep",
    )(proj, proj, proj, cosq, sinq, cos2, sin2, q_norm_g, kv_norm_g, wq_nope, wq_rope, wq_rope_sw, wuk_t)


def _mla_attn_kernel(q_ref, ckv_ref, kr_ref, wuv_ref, o_ref, m_ref, l_ref, acc_ref, *, start, tq, tk):
    i = pl.program_id(1)
    j = pl.program_id(2)
    rows = MLA_HEADS * tq

    @pl.when(j == 0)
    def _():
        m_ref[...] = jnp.full(m_ref.shape, -jnp.inf, F32)
        l_ref[...] = jnp.zeros(l_ref.shape, F32)
        acc_ref[...] = jnp.zeros(acc_ref.shape, F32)

    last_q_chunk = (start + i * tq + tq - 1) // CHUNK

    @pl.when((j * tk) // CHUNK <= last_q_chunk)
    def _():
        q = q_ref[0].reshape(rows, MLA_KV_RANK + MLA_ROPE)
        ckv = ckv_ref[0].astype(BF16)
        kr = kr_ref[0].astype(BF16)
        scale = (MLA_NOPE + MLA_ROPE) ** -0.5
        s = (_dot_nt(q[:, :MLA_KV_RANK], ckv) + _dot_nt(q[:, MLA_KV_RANK:], kr)) * scale
        q_pos = start + i * tq + lax.broadcasted_iota(jnp.int32, (rows, 1), 0) % tq
        k_pos = j * tk + lax.broadcasted_iota(jnp.int32, (1, tk), 1)
        s = jnp.where(k_pos // CHUNK <= q_pos // CHUNK, s, -jnp.inf)
        m_old = m_ref[...]
        m_new = jnp.maximum(m_old, jnp.max(s, -1, keepdims=True))
        alpha = jnp.exp(m_old - m_new)
        p = jnp.exp(s - m_new)
        l_ref[...] = alpha * l_ref[...] + jnp.sum(p, -1, keepdims=True)
        acc_ref[...] = alpha * acc_ref[...] + _dot(p.astype(BF16), ckv)
        m_ref[...] = m_new

    @pl.when(j == pl.num_programs(2) - 1)
    def _():
        o_lat = (acc_ref[...] / l_ref[...]).astype(BF16)
        for h in range(MLA_HEADS):
            o_ref[0, :, h * MLA_V:(h + 1) * MLA_V] = _dot(o_lat[h * tq:(h + 1) * tq], wuv_ref[h]).astype(BF16)


def _pick_tk(s):
    for cand in (512, 1024, 832, 768, 640, 576, 448, 384, 320, 256, 192, 128, 64):
        if s % cand == 0:
            return cand
    raise ValueError(f"unsupported key length {s}")


def _mla_attn(q, ckv_all, kr_all, wuv, start):
    b, _, t, qd = q.shape
    s = ckv_all.shape[1]
    tq = min(128, t)
    tk = _pick_tk(s)
    nkv = s // tk

    def kv_map(bi, i, j):
        last = ((start + i * tq + tq - 1) // CHUNK * CHUNK) // tk
        return (bi, jnp.minimum(j, last), 0)

    return pl.pallas_call(
        functools.partial(_mla_attn_kernel, start=start, tq=tq, tk=tk),
        grid=(b, t // tq, nkv),
        in_specs=[pl.BlockSpec((1, MLA_HEADS, tq, qd), lambda bi, i, j: (bi, 0, i, 0)),
                  pl.BlockSpec((1, tk, MLA_KV_RANK), kv_map),
                  pl.BlockSpec((1, tk, MLA_ROPE), kv_map),
                  pl.BlockSpec(wuv.shape, lambda bi, i, j: (0, 0, 0))],
        out_specs=pl.BlockSpec((1, tq, MLA_HEADS * MLA_V), lambda bi, i, j: (bi, i, 0)),
        out_shape=jax.ShapeDtypeStruct((b, t, MLA_HEADS * MLA_V), BF16),
        scratch_shapes=[pltpu.VMEM((MLA_HEADS * tq, 1), F32),
                        pltpu.VMEM((MLA_HEADS * tq, 1), F32),
                        pltpu.VMEM((MLA_HEADS * tq, MLA_KV_RANK), F32)],
        compiler_params=_cparams(("parallel", "parallel", "arbitrary")),
        name="mla_attn",
    )(q, ckv_all, kr_all, wuv)


def _unit_lower_solve(a, rhs):
    x = rhs - _dot_hi(a, rhs)
    p = a
    for _ in range(int(math.log2(CHUNK)) - 1):
        p = _dot_hi(p, p)
        x = x + _dot_hi(p, x)
    return x


def _gdn_kernel(qkv_ref, gz_ref, gba_ref, convp_ref, convw_ref, alog_ref, dtb_ref, gnorm_ref, s0_ref,
                ob_ref, snew_ref, convn_ref, s_scr, ext_scr):
    n = pl.program_id(1)
    L = CHUNK
    tail = 8

    @pl.when(n == 0)
    def _():
        s_scr[...] = s0_ref[0]
        ext_scr[0:tail, :] = jnp.zeros((tail, GDN_CONV_DIM), F32)
        ext_scr[tail - (GDN_CONV - 1):tail, :] = convp_ref[0]

    cur = qkv_ref[0]
    ext_scr[tail:tail + L, :] = cur
    w = convw_ref[...]
    conv = ext_scr[tail - 3:tail - 3 + L, :] * w[0:1]
    conv = conv + ext_scr[tail - 2:tail - 2 + L, :] * w[1:2]
    conv = conv + ext_scr[tail - 1:tail - 1 + L, :] * w[2:3]
    conv = conv + cur * w[3:4]
    conv = _silu(conv)

    @pl.when(n == pl.num_programs(1) - 1)
    def _():
        convn_ref[0] = ext_scr[tail + L - (GDN_CONV - 1):tail + L, :]

    ext_scr[0:tail, :] = cur[L - tail:, :]

    gba = gba_ref[0]
    beta_all = _sigmoid(gba)
    z = gba + dtb_ref[...]
    softplus = jnp.maximum(z, 0.0) + jnp.log1p(jnp.exp(-jnp.abs(z)))
    g_all = -jnp.exp(alog_ref[...]) * softplus
    ri = lax.broadcasted_iota(jnp.int32, (L, L), 0)
    ci = lax.broadcasted_iota(jnp.int32, (L, L), 1)
    incl = ci <= ri
    strict = ci < ri
    g_cum = _dot_hi(incl.astype(F32), g_all)
    g_cum_t = g_cum.T

    def l2n(x):
        return x * lax.rsqrt(jnp.sum(x * x, -1, keepdims=True) + 1e-6)

    for h in range(GDN_HEADS):
        q = l2n(conv[:, h * GDN_DK:(h + 1) * GDN_DK]) * (GDN_DK ** -0.5)
        k = l2n(conv[:, GDN_KEY_DIM + h * GDN_DK:GDN_KEY_DIM + (h + 1) * GDN_DK])
        v = conv[:, 2 * GDN_KEY_DIM + h * GDN_DV:2 * GDN_KEY_DIM + (h + 1) * GDN_DV]
        beta = beta_all[:, h:h + 1]
        gc = g_cum[:, GDN_HEADS + h:GDN_HEADS + h + 1]
        gr = g_cum_t[GDN_HEADS + h:GDN_HEADS + h + 1, :]
        dmat = jnp.exp(jnp.where(incl, gc - gr, -jnp.inf))
        a_mat = jnp.where(strict, beta * _dot_nt_hi(k, k) * dmat, 0.0)
        eg = jnp.exp(gc)
        rhs = jnp.concatenate([beta * v, (beta * eg) * k], -1)
        sol = _unit_lower_solve(a_mat, rhs)
        u, wmat = sol[:, :GDN_DV], sol[:, GDN_DV:]
        qk = _dot_nt_hi(q, k) * dmat
        s_old = s_scr[h]
        delta = u - _dot_hi(wmat, s_old)
        o = eg * _dot_hi(q, s_old) + _dot_hi(qk, delta)
        g_last = gc[L - 1:L, :]
        kd = k * jnp.exp(g_last - gc)
        s_scr[h] = jnp.exp(g_last) * s_old + _dot_hi(kd.T, delta)
        gz = gz_ref[0, :, h * GDN_DV:(h + 1) * GDN_DV]
        ob_ref[0, :, h * GDN_DV:(h + 1) * GDN_DV] = (_rms_norm(o, gnorm_ref[...]) * _silu(gz)).astype(BF16)

    @pl.when(n == pl.num_programs(1) - 1)
    def _():
        snew_ref[0] = s_scr[...]


def _gdn(proj, conv_past, conv_w, alog128, dtb128, gnorm, s0):
    b, t, _ = proj.shape
    L = CHUNK
    full = lambda a: pl.BlockSpec(a.shape, lambda bi, n: (0,) * a.ndim)
    return pl.pallas_call(
        _gdn_kernel,
        grid=(b, t // L),
        in_specs=[pl.BlockSpec((1, L, GDN_CONV_DIM), lambda bi, n: (bi, n, C_GQKV // GDN_CONV_DIM)),
                  pl.BlockSpec((1, L, GDN_VAL_DIM), lambda bi, n: (bi, n, C_GZ // GDN_VAL_DIM)),
                  pl.BlockSpec((1, L, LANES), lambda bi, n: (bi, n, C_GBA // LANES)),
                  pl.BlockSpec((1, GDN_CONV - 1, GDN_CONV_DIM), lambda bi, n: (bi, 0, 0)),
                  full(conv_w), full(alog128), full(dtb128), full(gnorm),
                  pl.BlockSpec((1, GDN_HEADS, GDN_DK, GDN_DV), lambda bi, n: (bi, 0, 0, 0))],
        out_specs=[pl.BlockSpec((1, L, GDN_VAL_DIM), lambda bi, n: (bi, n, 0)),
                   pl.BlockSpec((1, GDN_HEADS, GDN_DK, GDN_DV), lambda bi, n: (bi, 0, 0, 0)),
                   pl.BlockSpec((1, GDN_CONV - 1, GDN_CONV_DIM), lambda bi, n: (bi, 0, 0))],
        out_shape=[jax.ShapeDtypeStruct((b, t, GDN_VAL_DIM), BF16),
                   jax.ShapeDtypeStruct((b, GDN_HEADS, GDN_DK, GDN_DV), F32),
                   jax.ShapeDtypeStruct((b, GDN_CONV - 1, GDN_CONV_DIM), F32)],
        scratch_shapes=[pltpu.VMEM((GDN_HEADS, GDN_DK, GDN_DV), F32),
                        pltpu.VMEM((8 + L, GDN_CONV_DIM), F32)],
        compiler_params=_cparams(("parallel", "arbitrary")),
        name="gdn",
    )(proj, proj, proj, conv_past, conv_w, alog128, dtb128, gnorm, s0)


def _cb_attn_kernel(q_ref, kprev_ref, kcur_ref, vprev_ref, vcur_ref, bias_ref, o_ref, *, tq, pad):
    i = pl.program_id(1)
    L = CHUNK
    width = CB_PAST_ROWS + L
    scale = CB_DH ** -0.5
    for c in range(tq // L):
        lo = c * L
        kwin = jnp.concatenate([kprev_ref[0, lo:, :], kcur_ref[0, :lo + L, :]], 0).astype(BF16)
        vwin = jnp.concatenate([vprev_ref[0, lo:, :], vcur_ref[0, :lo + L, :]], 0).astype(BF16)
        q = q_ref[0, lo:lo + L, :].astype(BF16)
        row = i * tq + lo + lax.broadcasted_iota(jnp.int32, (1, width), 1)
        valid = row >= pad
        for h in range(CB_HEADS):
            hs = slice(h * CB_DH, (h + 1) * CB_DH)
            s = _dot_nt(q[:, hs], kwin[:, hs]) * scale + bias_ref[h]
            s = jnp.where(valid, s, -jnp.inf)
            m = jnp.max(s, -1, keepdims=True)
            p = jnp.exp(s - m)
            p = (p / jnp.sum(p, -1, keepdims=True)).astype(BF16)
            o_ref[0, lo:lo + L, hs] = _dot(p, vwin[:, hs]).astype(BF16)


def _cb_attn(proj, kp, vp, bias, pad):
    b, t, _ = proj.shape
    tq = min(CB_PAST_ROWS, t)
    assert tq == CB_PAST_ROWS or tq == t
    assert CB_PAST_ROWS % tq == 0
    prev_spec = pl.BlockSpec((1, CB_PAST_ROWS, CB_DIM), lambda bi, i: (bi, i, 0))
    cur_spec = pl.BlockSpec((1, tq, CB_DIM), lambda bi, i: (bi, CB_PAST_ROWS // tq + i, 0))
    return pl.pallas_call(
        functools.partial(_cb_attn_kernel, tq=tq, pad=pad),
        grid=(b, t // tq),
        in_specs=[pl.BlockSpec((1, tq, CB_DIM), lambda bi, i: (bi, i, C_CB // CB_DIM)),
                  prev_spec, cur_spec, prev_spec, cur_spec,
                  pl.BlockSpec(bias.shape, lambda bi, i: (0, 0, 0))],
        out_specs=pl.BlockSpec((1, tq, CB_DIM), lambda bi, i: (bi, i, 0)),
        out_shape=jax.ShapeDtypeStruct((b, t, CB_DIM), BF16),
        compiler_params=_cparams(("parallel", "parallel")),
        name="cb_attn",
    )(proj, kp, kp, vp, vp, bias)


def _route(logits_t, rb):
    s = _sigmoid(logits_t)
    sb = s + rb
    rows = [sb[e:e + 1, :] for e in range(N_EXPERTS)]
    grp = []
    for g in range(N_GROUPS):
        r = rows[g * EXPERTS_PER_GROUP:(g + 1) * EXPERTS_PER_GROUP]
        best = None
        for a in range(EXPERTS_PER_GROUP):
            for c in range(a + 1, EXPERTS_PER_GROUP):
                pair = r[a] + r[c]
                best = pair if best is None else jnp.maximum(best, pair)
        grp.append(best)
    gmax = functools.reduce(jnp.maximum, grp)
    gsel = jnp.full(gmax.shape, N_GROUPS, jnp.int32)
    for g in reversed(range(N_GROUPS)):
        gsel = jnp.where(grp[g] == gmax, g, gsel)
    sel = []
    for e in range(N_EXPERTS):
        g = e // EXPERTS_PER_GROUP
        rank = jnp.zeros(gmax.shape, jnp.int32)
        for e2 in range(g * EXPERTS_PER_GROUP, (g + 1) * EXPERTS_PER_GROUP):
            if e2 == e:
                continue
            ahead = (rows[e2] >= rows[e]) if e2 < e else (rows[e2] > rows[e])
            rank = rank + ahead.astype(jnp.int32)
        sel.append(jnp.where((gsel == g) & (rank < 2), s[e:e + 1, :], 0.0))
    wsel = jnp.concatenate(sel, 0)
    return wsel / jnp.sum(wsel, 0, keepdims=True)


def _out_ln1_kernel(oa_ref, ob_ref, oc_ref, x_ref, wa_ref, wb_ref, wc_ref, g_ref, b_ref, rwt_ref, rb_ref,
                    x1_ref, gate_ref):
    y = _dot(oa_ref[...], wa_ref[...]) + _dot(ob_ref[...], wb_ref[...]) + _dot(oc_ref[...], wc_ref[...])
    x1 = _layer_norm(DEEPNORM_ALPHA * x_ref[...] + y, g_ref[...], b_ref[...])
    x1_ref[...] = x1
    logits_t = _dot_nt_hi(rwt_ref[...], x1)
    gate = _route(logits_t, rb_ref[...])
    tm = gate.shape[1]
    gate = jnp.concatenate([gate, jnp.zeros((LANES - N_EXPERTS, tm), F32)], 0)
    gate_ref[...] = gate.T


def _out_ln1(oa, ob, oc, x2d, wa, wb, wc, g, bb, rwt, rb):
    n, d = x2d.shape
    tm = min(256, n)
    full = lambda a: pl.BlockSpec(a.shape, lambda i: (0,) * a.ndim)
    row = lambda w: pl.BlockSpec((tm, w), lambda i: (i, 0))
    return pl.pallas_call(
        _out_ln1_kernel,
        grid=(n // tm,),
        in_specs=[row(oa.shape[1]), row(ob.shape[1]), row(oc.shape[1]), row(d),
                  full(wa), full(wb), full(wc), full(g), full(bb), full(rwt), full(rb)],
        out_specs=[row(d), row(LANES)],
        out_shape=[jax.ShapeDtypeStruct((n, d), F32), jax.ShapeDtypeStruct((n, LANES), F32)],
        compiler_params=_cparams(("parallel",)),
        name="out_ln1",
    )(oa, ob, oc, x2d, wa, wb, wc, g, bb, rwt, rb)


def _moe_kernel(x_ref, gate_ref, wg_ref, wu_ref, wd_ref, g_ref, b_ref, o_ref, xb_ref, acc_ref):
    e = pl.program_id(1)
    f = pl.program_id(2)

    @pl.when((e == 0) & (f == 0))
    def _():
        xb_ref[...] = x_ref[...].astype(BF16)
        acc_ref[...] = jnp.zeros(acc_ref.shape, F32)

    xb = xb_ref[...]
    h = (_silu(_dot(xb, wg_ref[0])) * _dot(xb, wu_ref[0])).astype(BF16)
    gate = gate_ref[...]
    lane = lax.broadcasted_iota(jnp.int32, gate.shape, 1)
    gcol = jnp.sum(jnp.where(lane == e, gate, 0.0), -1, keepdims=True)
    acc_ref[...] += gcol * _dot(h, wd_ref[0])

    @pl.when((e == pl.num_programs(1) - 1) & (f == pl.num_programs(2) - 1))
    def _():
        o_ref[...] = _layer_norm(DEEPNORM_ALPHA * x_ref[...] + acc_ref[...], g_ref[...], b_ref[...])


def _moe_ln2(x1, gate, wg, wu, wd, g, bb):
    n, d = x1.shape
    ne, _, ff = wg.shape
    tm = min(512, n)
    tf = min(512, ff)
    full = lambda a: pl.BlockSpec(a.shape, lambda i, e, f: (0,) * a.ndim)
    return pl.pallas_call(
        _moe_kernel,
        grid=(n // tm, ne, ff // tf),
        in_specs=[pl.BlockSpec((tm, d), lambda i, e, f: (i, 0)),
                  pl.BlockSpec((tm, LANES), lambda i, e, f: (i, 0)),
                  pl.BlockSpec((1, d, tf), lambda i, e, f: (e, 0, f)),
                  pl.BlockSpec((1, d, tf), lambda i, e, f: (e, 0, f)),
                  pl.BlockSpec((1, tf, d), lambda i, e, f: (e, f, 0)),
                  full(g), full(bb)],
        out_specs=pl.BlockSpec((tm, d), lambda i, e, f: (i, 0)),
        out_shape=jax.ShapeDtypeStruct((n, d), F32),
        scratch_shapes=[pltpu.VMEM((tm, d), BF16), pltpu.VMEM((tm, d), F32)],
        compiler_params=_cparams(("parallel", "arbitrary", "arbitrary")),
        name="moe_ln2",
    )(x1, gate, wg, wu, wd, g, bb)


def _prep_layer(w_in, q_norm_g, w_uq, kv_norm_g, w_uk, w_uv, conv_w, a_log, dt_bias, gdn_norm_g, rel_bias, w_out,
                ln1_g, ln1_b, w_gate, w_up, w_down, ln2_g, ln2_b):
    splits = np.cumsum([MLA_Q_RANK, MLA_KV_RANK, MLA_ROPE, GDN_KEY_DIM, GDN_KEY_DIM, GDN_VAL_DIM, GDN_VAL_DIM,
                        GDN_HEADS, GDN_HEADS])
    cq, ckv, kr, gq, gk, gv, gz, gb, ga, cqkv = jnp.split(w_in, splits, axis=1)
    half = MLA_ROPE // 2
    kr_sw = jnp.concatenate([kr[:, half:], kr[:, :half]], 1)
    gba = jnp.pad(jnp.concatenate([gb, ga], 1), ((0, 0), (0, LANES - 2 * GDN_HEADS)))
    w_in_p = jnp.concatenate([gq, gk, gv, gz, cq, ckv, kr, kr_sw, gba, cqkv], 1).astype(BF16)
    r = w_uq.shape[0]
    wq_nope = w_uq[:, :, :MLA_NOPE].reshape(r, MLA_HEADS * MLA_NOPE).astype(BF16)
    wq_r = w_uq[:, :, MLA_NOPE:]
    wq_rope = wq_r.reshape(r, MLA_HEADS * MLA_ROPE).astype(BF16)
    wq_rope_sw = jnp.concatenate([wq_r[..., half:], wq_r[..., :half]], -1).reshape(r, MLA_HEADS * MLA_ROPE).astype(BF16)
    wuk_t = jnp.transpose(w_uk, (1, 2, 0)).astype(BF16)
    wuv = jnp.transpose(w_uv, (1, 0, 2)).astype(BF16)
    lane_pad = lambda a: jnp.pad(a, (GDN_HEADS, LANES - 2 * GDN_HEADS))[None, :]
    rel = CB_PAST_ROWS + np.arange(CHUNK)[:, None] - np.arange(CB_PAST_ROWS + CHUNK)[None, :]
    bias = rel_bias[:, np.clip(rel, -REL_CLIP, REL_CLIP) + REL_CLIP].astype(F32)
    w_out_b = w_out.astype(BF16)
    na = MLA_HEADS * MLA_V
    return dict(
        w_in=w_in_p, q_norm_g=q_norm_g[None, :], kv_norm_g=kv_norm_g[None, :],
        wq_nope=wq_nope, wq_rope=wq_rope, wq_rope_sw=wq_rope_sw, wuk_t=wuk_t, wuv=wuv,
        conv_w=conv_w, alog=lane_pad(a_log), dtb=lane_pad(dt_bias), gnorm=gdn_norm_g[None, :], bias=bias,
        wo_a=w_out_b[:na], wo_b=w_out_b[na:na + GDN_VAL_DIM], wo_c=w_out_b[na + GDN_VAL_DIM:],
        ln1_g=ln1_g[None, :], ln1_b=ln1_b[None, :], ln2_g=ln2_g[None, :], ln2_b=ln2_b[None, :],
        w_gate=w_gate.astype(BF16), w_up=w_up.astype(BF16), w_down=w_down.astype(BF16))


def _rope_tables(start, t):
    pos = start + jnp.arange(t, dtype=jnp.int32)
    inv = ROPE_THETA ** (-jnp.arange(0, MLA_ROPE, 2, dtype=F32) / MLA_ROPE)
    ang = pos.astype(F32)[:, None] * inv[None, :]
    cos, sin = jnp.cos(ang), jnp.sin(ang)
    return jnp.concatenate([cos, cos], -1), jnp.concatenate([-sin, sin], -1)


def _layer(x, p, rwt, rb, ckv_past, krope_past, s_past, conv_past, cbk_past, cbv_past):
    b, t, d = x.shape
    start = ckv_past.shape[1]
    x2d = x.reshape(b * t, d)
    proj = _in_proj(x2d, p["w_in"]).reshape(b, t, IN_PAD)

    cos2, sin2 = _rope_tables(start, t)
    q, ckv_new, krope_new = _mla_prep(proj, cos2, sin2, p["q_norm_g"], p["kv_norm_g"], p["wq_nope"], p["wq_rope"],
                                      p["wq_rope_sw"], p["wuk_t"])
    if start:
        ckv_all = jnp.concatenate([ckv_past, ckv_new], 1)
        kr_all = jnp.concatenate([krope_past, krope_new], 1)
    else:
        ckv_all, kr_all = ckv_new, krope_new
    o_a = _mla_attn(q, ckv_all, kr_all, p["wuv"], start)

    o_b, s_new, conv_new = _gdn(proj, conv_past, p["conv_w"], p["alog"], p["dtb"], p["gnorm"], s_past)

    cb_k = proj[:, :, C_CB + CB_DIM:C_CB + 2 * CB_DIM]
    cb_v = proj[:, :, C_CB + 2 * CB_DIM:]
    p_rows = cbk_past.shape[1]
    pad = CB_PAST_ROWS - p_rows
    zeros = jnp.zeros((b, pad, CB_DIM), F32)
    kp = jnp.concatenate([zeros, cbk_past.reshape(b, p_rows, CB_DIM), cb_k], 1)
    vp = jnp.concatenate([zeros, cbv_past.reshape(b, p_rows, CB_DIM), cb_v], 1)
    o_c = _cb_attn(proj, kp, vp, p["bias"], pad)

    x1, gate = _out_ln1(o_a.reshape(b * t, -1), o_b.reshape(b * t, -1), o_c.reshape(b * t, -1), x2d,
                        p["wo_a"], p["wo_b"], p["wo_c"], p["ln1_g"], p["ln1_b"], rwt, rb)
    x2 = _moe_ln2(x1, gate, p["w_gate"], p["w_up"], p["w_down"], p["ln2_g"], p["ln2_b"])

    keep = min(CB_PAST_ROWS, t)
    state = (ckv_new, krope_new, s_new, conv_new,
             cb_k[:, -keep:].reshape(b, keep, CB_HEADS, CB_DH), cb_v[:, -keep:].reshape(b, keep, CB_HEADS, CB_DH))
    return x2.reshape(b, t, d), state


def kernel(x_prompt, x_sample, cache_mla_ckv, cache_mla_krope, state_gdn, state_gdn_conv, cache_cb_k, cache_cb_v,
           w_in, q_norm_g, w_uq, kv_norm_g, w_uk, w_uv, conv_w, a_log, dt_bias, gdn_norm_g, rel_bias, w_out,
           ln1_g, ln1_b, router_w, router_b, w_gate, w_up, w_down, ln2_g, ln2_b):
    depth = w_in.shape[0]
    layers = [_prep_layer(w_in[l], q_norm_g[l], w_uq[l], kv_norm_g[l], w_uk[l], w_uv[l], conv_w[l], a_log[l],
                          dt_bias[l], gdn_norm_g[l], rel_bias[l], w_out[l], ln1_g[l], ln1_b[l],
                          w_gate[l], w_up[l], w_down[l], ln2_g[l], ln2_b[l]) for l in range(depth)]
    rwt = router_w.T
    rb = router_b[:, None]

    def run_trunk(x, ckv_c, krope_c, s_c, conv_c, cbk_c, cbv_c):
        new = ([], [], [], [], [], [])
        for l in range(depth):
            x, st = _layer(x, layers[l], rwt, rb, ckv_c[l], krope_c[l], s_c[l], conv_c[l], cbk_c[l], cbv_c[l])
            for lst, a in zip(new, st):
                lst.append(a)
        return (x, *[jnp.stack(a) for a in new])

    bp = x_prompt.shape[0]
    empty = lambda *s: jnp.zeros((depth, bp) + s, F32)
    outs_p = run_trunk(x_prompt, empty(0, MLA_KV_RANK), empty(0, MLA_ROPE), empty(GDN_HEADS, GDN_DK, GDN_DV),
                       empty(GDN_CONV - 1, GDN_CONV_DIM), empty(0, CB_HEADS, CB_DH), empty(0, CB_HEADS, CB_DH))
    outs_s = run_trunk(x_sample, cache_mla_ckv, cache_mla_krope, state_gdn, state_gdn_conv, cache_cb_k, cache_cb_v)
    return (outs_p[0], outs_s[0], *outs_p[1:], *outs_s[1:])
```

```python
import functools
import math

import jax
import jax.numpy as jnp
import numpy as np
from jax import lax
from jax.experimental import pallas as pl
from jax.experimental.pallas import tpu as pltpu

F32 = jnp.float32
BF16 = jnp.bfloat16

CHUNK = 64
MLA_HEADS = 6
MLA_Q_RANK = 512
MLA_KV_RANK = 256
MLA_NOPE = 128
MLA_ROPE = 64
MLA_V = 128
ROPE_THETA = 10000.0
GDN_HEADS = 6
GDN_DK = 128
GDN_DV = 128
GDN_CONV = 4
GDN_KEY_DIM = GDN_HEADS * GDN_DK
GDN_VAL_DIM = GDN_HEADS * GDN_DV
GDN_CONV_DIM = 2 * GDN_KEY_DIM + GDN_VAL_DIM
CB_HEADS = 4
CB_DH = 128
CB_DIM = CB_HEADS * CB_DH
CB_PAST_ROWS = 8 * CHUNK
REL_CLIP = 256
N_EXPERTS = 16
N_GROUPS = 4
EXPERTS_PER_GROUP = N_EXPERTS // N_GROUPS
DEPTH = 2
DEEPNORM_ALPHA = (2 * DEPTH) ** 0.25

LANES = 128
C_GQKV = 0
C_GZ = C_GQKV + GDN_CONV_DIM
C_CQ = C_GZ + GDN_VAL_DIM
C_CKV = C_CQ + MLA_Q_RANK
C_KR = C_CKV + MLA_KV_RANK
C_GBA = C_KR + 2 * MLA_ROPE
C_CB = C_GBA + LANES
IN_PAD = C_CB + 3 * CB_DIM

SLAB_ROWS = 2048 // LANES
EXT_SEL = 0
EXT_W_LO = N_EXPERTS

VMEM_LIMIT = 56 * 1024 * 1024


def _cparams(sem):
    return pltpu.CompilerParams(dimension_semantics=sem, vmem_limit_bytes=VMEM_LIMIT)


def _dot(a, b):
    return jnp.dot(a, b, preferred_element_type=F32)


def _dot_nt(a, b):
    return lax.dot_general(a, b, (((1,), (1,)), ((), ())), preferred_element_type=F32)


def _dot_hi(a, b):
    return jnp.dot(a, b, preferred_element_type=F32, precision=lax.Precision.HIGHEST)


def _dot_nt_hi(a, b):
    return lax.dot_general(a, b, (((1,), (1,)), ((), ())), preferred_element_type=F32,
                           precision=lax.Precision.HIGHEST)


def _sigmoid(x):
    return 1.0 / (1.0 + jnp.exp(-x))


def _silu(x):
    return x * _sigmoid(x)


def _layer_norm(h, g, b, eps=1e-5):
    mu = jnp.mean(h, -1, keepdims=True)
    d = h - mu
    var = jnp.mean(d * d, -1, keepdims=True)
    return d * lax.rsqrt(var + eps) * g + b


def _rms_norm(x, g, eps=1e-6):
    return x * lax.rsqrt(jnp.mean(x * x, -1, keepdims=True) + eps) * g


def _inproj_kernel(x_ref, w_ref, o_ref, xb_ref):
    @pl.when(pl.program_id(1) == 0)
    def _():
        xb_ref[...] = x_ref[...].astype(BF16)

    o_ref[...] = _dot(xb_ref[...], w_ref[...])


def _in_proj(x2d, w):
    n, d = x2d.shape
    width = w.shape[1]
    tm = min(1024, n)
    tn = 512
    return pl.pallas_call(
        _inproj_kernel,
        grid=(n // tm, width // tn),
        in_specs=[pl.BlockSpec((tm, d), lambda i, j: (i, 0)),
                  pl.BlockSpec((d, tn), lambda i, j: (0, j))],
        out_specs=pl.BlockSpec((tm, tn), lambda i, j: (i, j)),
        out_shape=jax.ShapeDtypeStruct((n, width), F32),
        scratch_shapes=[pltpu.VMEM((tm, d), BF16)],
        compiler_params=_cparams(("parallel", "arbitrary")),
        name="in_proj",
    )(x2d, w)


def _mla_prep_kernel(cq_ref, ckv_ref, kr_ref, cosq_ref, sinq_ref, cosk_ref, sink_ref,
                     qg_ref, kvg_ref, wqn_ref, wqr_ref, wqs_ref, wuk_ref,
                     q_ref, ckvn_ref, krn_ref):
    cqn = _rms_norm(cq_ref[0], qg_ref[...]).astype(BF16)
    q_nope = _dot(cqn, wqn_ref[...]).astype(BF16)
    q_rope = _dot(cqn, wqr_ref[...])
    q_rope_sw = _dot(cqn, wqs_ref[...])
    q_rot = (q_rope * cosq_ref[...] + q_rope_sw * sinq_ref[...]).astype(BF16)
    for h in range(MLA_HEADS):
        q_lat = _dot(q_nope[:, h * MLA_NOPE:(h + 1) * MLA_NOPE], wuk_ref[h])
        q_ref[0, h, :, 0:MLA_KV_RANK] = q_lat.astype(BF16)
        q_ref[0, h, :, MLA_KV_RANK:] = q_rot[:, h * MLA_ROPE:(h + 1) * MLA_ROPE]
    ckvn_ref[0] = _rms_norm(ckv_ref[0], kvg_ref[...])
    kr = kr_ref[0]
    krn_ref[0] = kr[:, :MLA_ROPE] * cosk_ref[...] + kr[:, MLA_ROPE:] * sink_ref[...]


def _mla_prep(proj, cos2, sin2, q_norm_g, kv_norm_g, wq_nope, wq_rope, wq_rope_sw, wuk_t):
    b, t, _ = proj.shape
    tm = min(512, t)
    cosq = jnp.tile(cos2, (1, MLA_HEADS))
    sinq = jnp.tile(sin2, (1, MLA_HEADS))
    full = lambda a: pl.BlockSpec(a.shape, lambda bi, i: (0,) * a.ndim)
    row = lambda w: pl.BlockSpec((tm, w), lambda bi, i: (i, 0))
    qd = MLA_KV_RANK + MLA_ROPE
    return pl.pallas_call(
        _mla_prep_kernel,
        grid=(b, t // tm),
        in_specs=[pl.BlockSpec((1, tm, MLA_Q_RANK), lambda bi, i: (bi, i, C_CQ // MLA_Q_RANK)),
                  pl.BlockSpec((1, tm, MLA_KV_RANK), lambda bi, i: (bi, i, C_CKV // MLA_KV_RANK)),
                  pl.BlockSpec((1, tm, 2 * MLA_ROPE), lambda bi, i: (bi, i, C_KR // (2 * MLA_ROPE))),
                  row(MLA_HEADS * MLA_ROPE), row(MLA_HEADS * MLA_ROPE), row(MLA_ROPE), row(MLA_ROPE),
                  full(q_norm_g), full(kv_norm_g), full(wq_nope), full(wq_rope), full(wq_rope_sw),
                  full(wuk_t)],
        out_specs=[pl.BlockSpec((1, MLA_HEADS, tm, qd), lambda bi, i: (bi, 0, i, 0)),
                   pl.BlockSpec((1, tm, MLA_KV_RANK), lambda bi, i: (bi, i, 0)),
                   pl.BlockSpec((1, tm, MLA_ROPE), lambda bi, i: (bi, i, 0))],
        out_shape=[jax.ShapeDtypeStruct((b, MLA_HEADS, t, qd), BF16),
                   jax.ShapeDtypeStruct((b, t, MLA_KV_RANK), F32),
                   jax.ShapeDtypeStruct((b, t, MLA_ROPE), F32)],
        compiler_params=_cparams(("parallel", "parallel")),
        name="mla_prep",
    )(proj, proj, proj, cosq, sinq, cos2, sin2, q_norm_g, kv_norm_g, wq_nope, wq_rope, wq_rope_sw, wuk_t)


def _mla_attn_kernel(qi_ref, kj_ref, q_ref, ckv_ref, kr_ref, wuv_ref, o_ref, m_ref, l_ref, acc_ref, *, start, tq, tk):
    step = pl.program_id(1)
    i = qi_ref[step]
    j = kj_ref[step]

    @pl.when(j == 0)
    def _():
        m_ref[...] = jnp.full(m_ref.shape, -jnp.inf, F32)
        l_ref[...] = jnp.zeros(l_ref.shape, F32)
        acc_ref[...] = jnp.zeros(acc_ref.shape, F32)

    ckv = ckv_ref[0].astype(BF16)
    kr = kr_ref[0].astype(BF16)
    scale = (MLA_NOPE + MLA_ROPE) ** -0.5

    def update(masked):
        if masked:
            q_pos = start + i * tq + lax.broadcasted_iota(jnp.int32, (tq, 1), 0)
            k_pos = j * tk + lax.broadcasted_iota(jnp.int32, (1, tk), 1)
            visible = k_pos // CHUNK <= q_pos // CHUNK
        for h in range(MLA_HEADS):
            q = q_ref[0, h]
            s = (_dot_nt(q[:, :MLA_KV_RANK], ckv) + _dot_nt(q[:, MLA_KV_RANK:], kr)) * scale
            if masked:
                s = jnp.where(visible, s, -jnp.inf)
            rows = slice(h * tq, (h + 1) * tq)
            m_old = m_ref[rows]
            m_new = jnp.maximum(m_old, jnp.max(s, -1, keepdims=True))
            alpha = jnp.exp(m_old - m_new)
            p = jnp.exp(s - m_new)
            l_ref[rows] = alpha * l_ref[rows] + jnp.sum(p, -1, keepdims=True)
            acc_ref[rows] = alpha * acc_ref[rows] + _dot(p.astype(BF16), ckv)
            m_ref[rows] = m_new

    needs_mask = (j * tk + tk - 1) // CHUNK > (start + i * tq) // CHUNK
    pl.when(needs_mask)(lambda: update(True))
    pl.when(jnp.logical_not(needs_mask))(lambda: update(False))

    @pl.when(j == _last_kv_block(start, i, tq, tk))
    def _():
        o_lat = (acc_ref[...] / l_ref[...]).astype(BF16)
        for h in range(MLA_HEADS):
            o_ref[0, :, h * MLA_V:(h + 1) * MLA_V] = _dot(o_lat[h * tq:(h + 1) * tq], wuv_ref[h]).astype(BF16)


def _last_kv_block(start, i, tq, tk):
    return ((start + i * tq + tq - 1) // CHUNK * CHUNK) // tk


def _pick_tk(s):
    for cand in (512, 1024, 832, 768, 640, 576, 448, 384, 320, 256, 192, 128, 64):
        if s % cand == 0:
            return cand
    raise ValueError(f"unsupported key length {s}")


def _mla_attn(q, ckv_all, kr_all, wuv, start):
    b, _, t, qd = q.shape
    s = ckv_all.shape[1]
    tq = min(256, t)
    tk = _pick_tk(s)
    assert tk % CHUNK == 0 and s % tk == 0 and t % tq == 0
    pairs = [(i, j) for i in range(t // tq) for j in range(_last_kv_block(start, i, tq, tk) + 1)]
    qi = jnp.asarray(np.array([p[0] for p in pairs], np.int32))
    kj = jnp.asarray(np.array([p[1] for p in pairs], np.int32))
    kv_map = lambda bi, st, qi_r, kj_r: (bi, kj_r[st], 0)
    return pl.pallas_call(
        functools.partial(_mla_attn_kernel, start=start, tq=tq, tk=tk),
        grid_spec=pltpu.PrefetchScalarGridSpec(
            num_scalar_prefetch=2, grid=(b, len(pairs)),
            in_specs=[pl.BlockSpec((1, MLA_HEADS, tq, qd), lambda bi, st, qi_r, kj_r: (bi, 0, qi_r[st], 0)),
                      pl.BlockSpec((1, tk, MLA_KV_RANK), kv_map),
                      pl.BlockSpec((1, tk, MLA_ROPE), kv_map),
                      pl.BlockSpec(wuv.shape, lambda bi, st, qi_r, kj_r: (0, 0, 0))],
            out_specs=pl.BlockSpec((1, tq, MLA_HEADS * MLA_V), lambda bi, st, qi_r, kj_r: (bi, qi_r[st], 0)),
            scratch_shapes=[pltpu.VMEM((MLA_HEADS * tq, 1), F32),
                            pltpu.VMEM((MLA_HEADS * tq, 1), F32),
                            pltpu.VMEM((MLA_HEADS * tq, MLA_KV_RANK), F32)]),
        out_shape=jax.ShapeDtypeStruct((b, t, MLA_HEADS * MLA_V), BF16),
        compiler_params=_cparams(("parallel", "arbitrary")),
        name="mla_attn",
    )(qi, kj, q, ckv_all, kr_all, wuv)


def _split_bf16(x):
    hi = x.astype(BF16)
    return hi, (x - hi.astype(F32)).astype(BF16)


def _unit_lower_solve_many(a_list, rhs_list):
    n = rhs_list[0].shape[1]
    levels = int(math.log2(CHUNK))
    xs, ps = list(rhs_list), list(a_list)
    for lvl in range(levels):
        for h in range(len(xs)):
            p_hi, p_lo = _split_bf16(ps[h])
            lhs = jnp.concatenate([p_hi, p_hi, p_lo], 1)
            if lvl < levels - 1:
                r_hi, r_lo = _split_bf16(jnp.concatenate([xs[h], ps[h]], 1))
                both = _dot(lhs, jnp.concatenate([r_hi, r_lo, r_hi], 0))
                px, ps[h] = both[:, :n], both[:, n:]
            else:
                r_hi, r_lo = _split_bf16(xs[h])
                px = _dot(lhs, jnp.concatenate([r_hi, r_lo, r_hi], 0))
            xs[h] = xs[h] - px if lvl == 0 else xs[h] + px
    return xs


def _gdn_kernel(qkv_ref, gz_ref, gba_ref, convp_ref, convw_ref, alog_ref, dtb_ref, gnorm_ref, s0_ref,
                ob_ref, snew_ref, convn_ref, s_scr, ext_scr, *, n_chunks):
    n = pl.program_id(1)
    L = CHUNK
    tail = 8

    @pl.when(n == 0)
    def _():
        s_scr[...] = s0_ref[0]
        ext_scr[0:tail, :] = jnp.zeros((tail, GDN_CONV_DIM), F32)
        ext_scr[tail - (GDN_CONV - 1):tail, :] = convp_ref[0]

    cur = qkv_ref[0]
    ext_scr[tail:tail + L, :] = cur
    w = convw_ref[...]
    conv = ext_scr[tail - 3:tail - 3 + L, :] * w[0:1]
    conv = conv + ext_scr[tail - 2:tail - 2 + L, :] * w[1:2]
    conv = conv + ext_scr[tail - 1:tail - 1 + L, :] * w[2:3]
    conv = conv + cur * w[3:4]
    conv = _silu(conv)

    @pl.when(n == n_chunks - 1)
    def _():
        convn_ref[0] = ext_scr[tail + L - (GDN_CONV - 1):tail + L, :]

    ext_scr[0:tail, :] = cur[L - tail:, :]

    gba = gba_ref[0]
    beta_all = _sigmoid(gba)
    z = gba + dtb_ref[...]
    softplus = jnp.maximum(z, 0.0) + jnp.log1p(jnp.exp(-jnp.abs(z)))
    g_all = -jnp.exp(alog_ref[...]) * softplus
    ri = lax.broadcasted_iota(jnp.int32, (L, L), 0)
    ci = lax.broadcasted_iota(jnp.int32, (L, L), 1)
    incl = ci <= ri
    strict = ci < ri
    g_cum = _dot_hi(incl.astype(F32), g_all)
    g_cum_t = g_cum.T

    def l2n(x):
        return x * lax.rsqrt(jnp.sum(x * x, -1, keepdims=True) + 1e-6)

    heads = range(GDN_HEADS)
    q = [l2n(conv[:, h * GDN_DK:(h + 1) * GDN_DK]) * (GDN_DK ** -0.5) for h in heads]
    k = [l2n(conv[:, GDN_KEY_DIM + h * GDN_DK:GDN_KEY_DIM + (h + 1) * GDN_DK]) for h in heads]
    v = [conv[:, 2 * GDN_KEY_DIM + h * GDN_DV:2 * GDN_KEY_DIM + (h + 1) * GDN_DV] for h in heads]
    beta = [beta_all[:, h:h + 1] for h in heads]
    gc = [g_cum[:, GDN_HEADS + h:GDN_HEADS + h + 1] for h in heads]
    dmat = [jnp.exp(jnp.where(incl, gc[h] - g_cum_t[GDN_HEADS + h:GDN_HEADS + h + 1, :], -jnp.inf)) for h in heads]
    eg = [jnp.exp(gc[h]) for h in heads]
    qk_kk = []
    for h in heads:
        kb = k[h].astype(BF16)
        qk_kk.append(_dot_nt(jnp.concatenate([q[h].astype(BF16), kb], 0), kb))
    a_mat = [jnp.where(strict, beta[h] * qk_kk[h][L:] * dmat[h], 0.0) for h in heads]
    rhs = [jnp.concatenate([beta[h] * v[h], (beta[h] * eg[h]) * k[h]], -1) for h in heads]
    sol = _unit_lower_solve_many(a_mat, rhs)
    s_old = [s_scr[h] for h in heads]
    wq_s = [_dot(jnp.concatenate([sol[h][:, GDN_DV:], q[h]], 0).astype(BF16), s_old[h].astype(BF16)) for h in heads]
    upd = []
    for h in heads:
        delta = sol[h][:, :GDN_DV] - wq_s[h][:L]
        kd = k[h] * jnp.exp(gc[h][L - 1:L, :] - gc[h])
        lhs = jnp.concatenate([qk_kk[h][:L] * dmat[h], kd.T], 0)
        upd.append(_dot(lhs.astype(BF16), delta.astype(BF16)))
    for h in heads:
        o = eg[h] * wq_s[h][L:] + upd[h][:L]
        s_scr[h] = jnp.exp(gc[h][L - 1:L, :]) * s_old[h] + upd[h][L:]
        gz = gz_ref[0, :, h * GDN_DV:(h + 1) * GDN_DV]
        ob_ref[0, :, h * GDN_DV:(h + 1) * GDN_DV] = (_rms_norm(o, gnorm_ref[...]) * _silu(gz)).astype(BF16)

    @pl.when(n == n_chunks - 1)
    def _():
        snew_ref[0] = s_scr[...]


def _gdn(proj, conv_past, conv_w, alog128, dtb128, gnorm, s0):
    b, t, _ = proj.shape
    L = CHUNK
    full = lambda a: pl.BlockSpec(a.shape, lambda bi, n: (0,) * a.ndim)
    return pl.pallas_call(
        functools.partial(_gdn_kernel, n_chunks=t // L),
        grid=(b, t // L),
        in_specs=[pl.BlockSpec((1, L, GDN_CONV_DIM), lambda bi, n: (bi, n, C_GQKV // GDN_CONV_DIM)),
                  pl.BlockSpec((1, L, GDN_VAL_DIM), lambda bi, n: (bi, n, C_GZ // GDN_VAL_DIM)),
                  pl.BlockSpec((1, L, LANES), lambda bi, n: (bi, n, C_GBA // LANES)),
                  pl.BlockSpec((1, GDN_CONV - 1, GDN_CONV_DIM), lambda bi, n: (bi, 0, 0)),
                  full(conv_w), full(alog128), full(dtb128), full(gnorm),
                  pl.BlockSpec((1, GDN_HEADS, GDN_DK, GDN_DV), lambda bi, n: (bi, 0, 0, 0))],
        out_specs=[pl.BlockSpec((1, L, GDN_VAL_DIM), lambda bi, n: (bi, n, 0)),
                   pl.BlockSpec((1, GDN_HEADS, GDN_DK, GDN_DV), lambda bi, n: (bi, 0, 0, 0)),
                   pl.BlockSpec((1, GDN_CONV - 1, GDN_CONV_DIM), lambda bi, n: (bi, 0, 0))],
        out_shape=[jax.ShapeDtypeStruct((b, t, GDN_VAL_DIM), BF16),
                   jax.ShapeDtypeStruct((b, GDN_HEADS, GDN_DK, GDN_DV), F32),
                   jax.ShapeDtypeStruct((b, GDN_CONV - 1, GDN_CONV_DIM), F32)],
        scratch_shapes=[pltpu.VMEM((GDN_HEADS, GDN_DK, GDN_DV), F32),
                        pltpu.VMEM((8 + L, GDN_CONV_DIM), F32)],
        compiler_params=_cparams(("parallel", "arbitrary")),
        name="gdn",
    )(proj, proj, proj, conv_past, conv_w, alog128, dtb128, gnorm, s0)


def _cb_attn_kernel(q_ref, kprev_ref, kcur_ref, vprev_ref, vcur_ref, bias_ref, o_ref, *, tq, pad):
    i = pl.program_id(1)
    L = CHUNK
    width = CB_PAST_ROWS + L
    scale = CB_DH ** -0.5
    for c in range(tq // L):
        lo = c * L
        kwin = jnp.concatenate([kprev_ref[0, lo:, :], kcur_ref[0, :lo + L, :]], 0).astype(BF16)
        vwin = jnp.concatenate([vprev_ref[0, lo:, :], vcur_ref[0, :lo + L, :]], 0).astype(BF16)
        q = q_ref[0, lo:lo + L, :].astype(BF16)
        row = i * tq + lo + lax.broadcasted_iota(jnp.int32, (1, width), 1)
        valid = row >= pad
        for h in range(CB_HEADS):
            hs = slice(h * CB_DH, (h + 1) * CB_DH)
            s = _dot_nt(q[:, hs], kwin[:, hs]) * scale + bias_ref[h]
            s = jnp.where(valid, s, -jnp.inf)
            m = jnp.max(s, -1, keepdims=True)
            p = jnp.exp(s - m)
            p = (p / jnp.sum(p, -1, keepdims=True)).astype(BF16)
            o_ref[0, lo:lo + L, hs] = _dot(p, vwin[:, hs]).astype(BF16)


def _cb_attn(proj, kp, vp, bias, pad):
    b, t, _ = proj.shape
    tq = min(CB_PAST_ROWS, t)
    assert tq == CB_PAST_ROWS or tq == t
    assert CB_PAST_ROWS % tq == 0
    prev_spec = pl.BlockSpec((1, CB_PAST_ROWS, CB_DIM), lambda bi, i: (bi, i, 0))
    cur_spec = pl.BlockSpec((1, tq, CB_DIM), lambda bi, i: (bi, CB_PAST_ROWS // tq + i, 0))
    return pl.pallas_call(
        functools.partial(_cb_attn_kernel, tq=tq, pad=pad),
        grid=(b, t // tq),
        in_specs=[pl.BlockSpec((1, tq, CB_DIM), lambda bi, i: (bi, i, C_CB // CB_DIM)),
                  prev_spec, cur_spec, prev_spec, cur_spec,
                  pl.BlockSpec(bias.shape, lambda bi, i: (0, 0, 0))],
        out_specs=pl.BlockSpec((1, tq, CB_DIM), lambda bi, i: (bi, i, 0)),
        out_shape=jax.ShapeDtypeStruct((b, t, CB_DIM), BF16),
        compiler_params=_cparams(("parallel", "parallel")),
        name="cb_attn",
    )(proj, kp, kp, vp, vp, bias)


def _route(logits_t, rb):
    s = _sigmoid(logits_t)
    sb = s + rb
    rows = [sb[e:e + 1, :] for e in range(N_EXPERTS)]
    grp = []
    for g in range(N_GROUPS):
        r = rows[g * EXPERTS_PER_GROUP:(g + 1) * EXPERTS_PER_GROUP]
        best = None
        for a in range(EXPERTS_PER_GROUP):
            for c in range(a + 1, EXPERTS_PER_GROUP):
                pair = r[a] + r[c]
                best = pair if best is None else jnp.maximum(best, pair)
        grp.append(best)
    gmax = functools.reduce(jnp.maximum, grp)
    gsel = jnp.full(gmax.shape, N_GROUPS, jnp.int32)
    for g in reversed(range(N_GROUPS)):
        gsel = jnp.where(grp[g] == gmax, g, gsel)
    sel = []
    for e in range(N_EXPERTS):
        g = e // EXPERTS_PER_GROUP
        rank = jnp.zeros(gmax.shape, jnp.int32)
        for e2 in range(g * EXPERTS_PER_GROUP, (g + 1) * EXPERTS_PER_GROUP):
            if e2 == e:
                continue
            ahead = (rows[e2] >= rows[e]) if e2 < e else (rows[e2] > rows[e])
            rank = rank + ahead.astype(jnp.int32)
        sel.append(jnp.where((gsel == g) & (rank < 2), 1.0, 0.0))
    ssum = functools.reduce(lambda a, c: a + c, [sel[e] * s[e:e + 1, :] for e in range(N_EXPERTS)])
    zero = jnp.zeros(gmax.shape, F32)
    seen, w_lo, w_hi, e_lo, e_hi = zero, zero, zero, zero, zero
    for e in range(N_EXPERTS):
        gate_e = sel[e] * s[e:e + 1, :] / ssum
        first = sel[e] * jnp.where(seen == 0.0, 1.0, 0.0)
        second = sel[e] - first
        w_lo, w_hi = w_lo + first * gate_e, w_hi + second * gate_e
        e_lo, e_hi = e_lo + first * e, e_hi + second * e
        seen = seen + sel[e]
    return sel, w_lo, w_hi, e_lo, e_hi


def _to_slab(ref, val):
    tm = val.shape[0]
    for c in range(val.shape[1] // LANES):
        ref[pl.ds(c, tm, stride=SLAB_ROWS), :] = val[:, c * LANES:(c + 1) * LANES]


def _out_ln1_kernel(oa_ref, ob_ref, oc_ref, x_ref, wa_ref, wb_ref, wc_ref, g_ref, b_ref, rwt_ref, rb_ref,
                    x1_ref, ext_ref):
    y = _dot(oa_ref[...], wa_ref[...]) + _dot(ob_ref[...], wb_ref[...]) + _dot(oc_ref[...], wc_ref[...])
    x1 = _layer_norm(DEEPNORM_ALPHA * x_ref[...] + y, g_ref[...], b_ref[...])
    _to_slab(x1_ref, x1)
    logits_t = _dot_nt_hi(rwt_ref[...], x1)
    sel, w_lo, w_hi, e_lo, e_hi = _route(logits_t, rb_ref[...])
    tm = x1.shape[0]
    rows = sel + [w_lo, w_hi, e_lo, e_hi]
    ext = jnp.concatenate(rows + [jnp.zeros((LANES - len(rows), tm), F32)], 0)
    ext_ref[...] = ext.T


def _out_ln1(oa, ob, oc, x2d, wa, wb, wc, g, bb, rwt, rb):
    n, d = x2d.shape
    tm = min(256, n)
    full = lambda a: pl.BlockSpec(a.shape, lambda i: (0,) * a.ndim)
    row = lambda w: pl.BlockSpec((tm, w), lambda i: (i, 0))
    return pl.pallas_call(
        _out_ln1_kernel,
        grid=(n // tm,),
        in_specs=[row(oa.shape[1]), row(ob.shape[1]), row(oc.shape[1]), row(d),
                  full(wa), full(wb), full(wc), full(g), full(bb), full(rwt), full(rb)],
        out_specs=[pl.BlockSpec((tm * SLAB_ROWS, LANES), lambda i: (i, 0)), row(LANES)],
        out_shape=[jax.ShapeDtypeStruct((n * SLAB_ROWS, LANES), F32), jax.ShapeDtypeStruct((n, LANES), F32)],
        compiler_params=_cparams(("parallel",)),
        name="out_ln1",
    )(oa, ob, oc, x2d, wa, wb, wc, g, bb, rwt, rb)


def _row_copy(src_hbm, src_row, dst, dst_row, sem):
    return pltpu.make_async_copy(src_hbm.at[pl.ds(pl.multiple_of(src_row * SLAB_ROWS, SLAB_ROWS), SLAB_ROWS)],
                                 dst.at[pl.ds(pl.multiple_of(dst_row * SLAB_ROWS, SLAB_ROWS), SLAB_ROWS)], sem)


def _moe_gather_kernel(src_ref, x_hbm, o_hbm, sem, *, rows):
    base = pl.program_id(0) * rows

    def issue(r, c):
        _row_copy(x_hbm, src_ref[base + r], o_hbm, base + r, sem).start()
        return c

    lax.fori_loop(0, rows, issue, 0, unroll=8)

    def wait(r, c):
        _row_copy(x_hbm, 0, o_hbm, base + r, sem).wait()
        return c

    lax.fori_loop(0, rows, wait, 0, unroll=8)


def _moe_gather(src, x_slab, n_rows, rows):
    return pl.pallas_call(
        functools.partial(_moe_gather_kernel, rows=rows),
        grid_spec=pltpu.PrefetchScalarGridSpec(
            num_scalar_prefetch=1, grid=(n_rows // rows,),
            in_specs=[pl.BlockSpec(memory_space=pl.ANY)],
            out_specs=pl.BlockSpec(memory_space=pl.ANY),
            scratch_shapes=[pltpu.SemaphoreType.DMA(())]),
        out_shape=jax.ShapeDtypeStruct((n_rows * SLAB_ROWS, LANES), F32),
        compiler_params=_cparams(("arbitrary",)),
        name="moe_gather",
    )(src, x_slab)


def _moe_ffn_kernel(tile_e_ref, nact_ref, xs_ref, wg_ref, wu_ref, wd_ref, ys_ref, xb_ref, acc_ref):
    i = pl.program_id(0)
    f = pl.program_id(1)
    tm = xb_ref.shape[0]

    @pl.when(i < nact_ref[0])
    def _():
        @pl.when(f == 0)
        def _():
            for c in range(xb_ref.shape[1] // LANES):
                xb_ref[:, c * LANES:(c + 1) * LANES] = xs_ref[pl.ds(c, tm, stride=SLAB_ROWS), :].astype(BF16)

        xb = xb_ref[...]
        h = (_silu(_dot(xb, wg_ref[0])) * _dot(xb, wu_ref[0])).astype(BF16)
        y = _dot(h, wd_ref[0])

        @pl.when(f == 0)
        def _():
            acc_ref[...] = y

        @pl.when(f > 0)
        def _():
            acc_ref[...] += y

        @pl.when(f == pl.num_programs(1) - 1)
        def _():
            _to_slab(ys_ref, acc_ref[...])

    @pl.when((i >= nact_ref[0]) & (f == pl.num_programs(1) - 1))
    def _():
        ys_ref[...] = jnp.zeros(ys_ref.shape, F32)


def _moe_ffn(tile_e, nact, xs, wg, wu, wd, tm):
    n_rows = xs.shape[0] // SLAB_ROWS
    ne, d, ff = wg.shape
    tf = min(512, ff)
    nf = ff // tf
    row = lambda i, f, te, na: (jnp.minimum(i, na[0] - 1), 0)
    fe = lambda i, f, na: jnp.where(i < na[0], f, nf - 1)
    return pl.pallas_call(
        _moe_ffn_kernel,
        grid_spec=pltpu.PrefetchScalarGridSpec(
            num_scalar_prefetch=2, grid=(n_rows // tm, nf),
            in_specs=[pl.BlockSpec((tm * SLAB_ROWS, LANES), row),
                      pl.BlockSpec((1, d, tf), lambda i, f, te, na: (te[i], 0, fe(i, f, na))),
                      pl.BlockSpec((1, d, tf), lambda i, f, te, na: (te[i], 0, fe(i, f, na))),
                      pl.BlockSpec((1, tf, d), lambda i, f, te, na: (te[i], fe(i, f, na), 0))],
            out_specs=pl.BlockSpec((tm * SLAB_ROWS, LANES), lambda i, f, te, na: (i, 0)),
            scratch_shapes=[pltpu.VMEM((tm, d), BF16), pltpu.VMEM((tm, d), F32)]),
        out_shape=jax.ShapeDtypeStruct(xs.shape, F32),
        compiler_params=_cparams(("arbitrary", "arbitrary")),
        name="moe_ffn",
    )(tile_e, nact, xs, wg, wu, wd)


def _moe_combine_kernel(pos0_ref, pos1_ref, x1_ref, ext_ref, ys_hbm, g_ref, b_ref, o_ref,
                        buf0_ref, buf1_ref, h_ref, sem):
    tm = o_ref.shape[0]
    base = pl.program_id(0) * tm

    def issue(r, c):
        _row_copy(ys_hbm, pos0_ref[base + r], buf0_ref, r, sem.at[0]).start()
        _row_copy(ys_hbm, pos1_ref[base + r], buf1_ref, r, sem.at[1]).start()
        return c

    lax.fori_loop(0, tm, issue, 0, unroll=8)

    def wait(r, c):
        _row_copy(ys_hbm, 0, buf0_ref, r, sem.at[0]).wait()
        _row_copy(ys_hbm, 0, buf1_ref, r, sem.at[1]).wait()
        return c

    lax.fori_loop(0, tm, wait, 0, unroll=8)

    ext = ext_ref[...]
    w_lo = ext[:, EXT_W_LO:EXT_W_LO + 1]
    w_hi = ext[:, EXT_W_LO + 1:EXT_W_LO + 2]
    for c in range(o_ref.shape[1] // LANES):
        rows = pl.ds(c, tm, stride=SLAB_ROWS)
        y = w_lo * buf0_ref[rows, :] + w_hi * buf1_ref[rows, :]
        h_ref[:, c * LANES:(c + 1) * LANES] = DEEPNORM_ALPHA * x1_ref[rows, :] + y
    o_ref[...] = _layer_norm(h_ref[...], g_ref[...], b_ref[...])


def _moe_combine(pos0, pos1, x1_slab, ext, ys, g, bb):
    n = ext.shape[0]
    d = g.shape[1]
    tm = min(256, n)
    return pl.pallas_call(
        _moe_combine_kernel,
        grid_spec=pltpu.PrefetchScalarGridSpec(
            num_scalar_prefetch=2, grid=(n // tm,),
            in_specs=[pl.BlockSpec((tm * SLAB_ROWS, LANES), lambda i, p0, p1: (i, 0)),
                      pl.BlockSpec((tm, LANES), lambda i, p0, p1: (i, 0)),
                      pl.BlockSpec(memory_space=pl.ANY),
                      pl.BlockSpec(g.shape, lambda i, p0, p1: (0, 0)),
                      pl.BlockSpec(bb.shape, lambda i, p0, p1: (0, 0))],
            out_specs=pl.BlockSpec((tm, d), lambda i, p0, p1: (i, 0)),
            scratch_shapes=[pltpu.VMEM((tm * SLAB_ROWS, LANES), F32), pltpu.VMEM((tm * SLAB_ROWS, LANES), F32),
                            pltpu.VMEM((tm, d), F32), pltpu.SemaphoreType.DMA((2,))]),
        out_shape=jax.ShapeDtypeStruct((n, d), F32),
        compiler_params=_cparams(("arbitrary",)),
        name="moe_combine",
    )(pos0, pos1, x1_slab, ext, ys, g, bb)


def _moe_ln2(x1_slab, ext, wg, wu, wd, g, bb):
    n = ext.shape[0]
    tm = 512 if n >= 8192 else 256
    n_rows = 2 * n + N_EXPERTS * tm
    sel = (ext[:, EXT_SEL:EXT_SEL + N_EXPERTS] > 0.5).astype(jnp.int32)
    csum = jnp.cumsum(sel, axis=0)
    counts = csum[-1]
    padded = (counts + tm - 1) // tm * tm
    seg_end = jnp.cumsum(padded)
    slot = (seg_end - padded)[None, :] + csum - sel
    e_lo = ext[:, EXT_W_LO + 2].astype(jnp.int32)
    e_hi = ext[:, EXT_W_LO + 3].astype(jnp.int32)
    pos0 = jnp.take_along_axis(slot, e_lo[:, None], 1)[:, 0]
    pos1 = jnp.take_along_axis(slot, e_hi[:, None], 1)[:, 0]
    tok = jnp.arange(n, dtype=jnp.int32)
    src = jnp.zeros((n_rows,), jnp.int32).at[pos0].set(tok).at[pos1].set(tok)
    total = seg_end[-1:]
    nact = total // tm
    tiles = jnp.arange(n_rows // tm, dtype=jnp.int32)
    tile_e = jnp.minimum(jnp.searchsorted(seg_end, jnp.minimum(tiles, nact[0] - 1) * tm, side="right"),
                         N_EXPERTS - 1).astype(jnp.int32)

    xs = _moe_gather(src, x1_slab, n_rows, tm)
    ys = _moe_ffn(tile_e, nact, xs, wg, wu, wd, tm)
    return _moe_combine(pos0, pos1, x1_slab, ext, ys, g, bb)


def _prep_layer(w_in, q_norm_g, w_uq, kv_norm_g, w_uk, w_uv, conv_w, a_log, dt_bias, gdn_norm_g, rel_bias, w_out,
                ln1_g, ln1_b, w_gate, w_up, w_down, ln2_g, ln2_b):
    splits = np.cumsum([MLA_Q_RANK, MLA_KV_RANK, MLA_ROPE, GDN_KEY_DIM, GDN_KEY_DIM, GDN_VAL_DIM, GDN_VAL_DIM,
                        GDN_HEADS, GDN_HEADS])
    cq, ckv, kr, gq, gk, gv, gz, gb, ga, cqkv = jnp.split(w_in, splits, axis=1)
    half = MLA_ROPE // 2
    kr_sw = jnp.concatenate([kr[:, half:], kr[:, :half]], 1)
    gba = jnp.pad(jnp.concatenate([gb, ga], 1), ((0, 0), (0, LANES - 2 * GDN_HEADS)))
    w_in_p = jnp.concatenate([gq, gk, gv, gz, cq, ckv, kr, kr_sw, gba, cqkv], 1).astype(BF16)
    r = w_uq.shape[0]
    wq_nope = w_uq[:, :, :MLA_NOPE].reshape(r, MLA_HEADS * MLA_NOPE).astype(BF16)
    wq_r = w_uq[:, :, MLA_NOPE:]
    wq_rope = wq_r.reshape(r, MLA_HEADS * MLA_ROPE).astype(BF16)
    wq_rope_sw = jnp.concatenate([wq_r[..., half:], wq_r[..., :half]], -1).reshape(r, MLA_HEADS * MLA_ROPE).astype(BF16)
    wuk_t = jnp.transpose(w_uk, (1, 2, 0)).astype(BF16)
    wuv = jnp.transpose(w_uv, (1, 0, 2)).astype(BF16)
    lane_pad = lambda a: jnp.pad(a, (GDN_HEADS, LANES - 2 * GDN_HEADS))[None, :]
    rel = CB_PAST_ROWS + np.arange(CHUNK)[:, None] - np.arange(CB_PAST_ROWS + CHUNK)[None, :]
    bias = rel_bias[:, np.clip(rel, -REL_CLIP, REL_CLIP) + REL_CLIP].astype(F32)
    w_out_b = w_out.astype(BF16)
    na = MLA_HEADS * MLA_V
    return dict(
        w_in=w_in_p, q_norm_g=q_norm_g[None, :], kv_norm_g=kv_norm_g[None, :],
        wq_nope=wq_nope, wq_rope=wq_rope, wq_rope_sw=wq_rope_sw, wuk_t=wuk_t, wuv=wuv,
        conv_w=conv_w, alog=lane_pad(a_log), dtb=lane_pad(dt_bias), gnorm=gdn_norm_g[None, :], bias=bias,
        wo_a=w_out_b[:na], wo_b=w_out_b[na:na + GDN_VAL_DIM], wo_c=w_out_b[na + GDN_VAL_DIM:],
        ln1_g=ln1_g[None, :], ln1_b=ln1_b[None, :], ln2_g=ln2_g[None, :], ln2_b=ln2_b[None, :],
        w_gate=w_gate.astype(BF16), w_up=w_up.astype(BF16), w_down=w_down.astype(BF16))


def _rope_tables(start, t):
    pos = start + jnp.arange(t, dtype=jnp.int32)
    inv = ROPE_THETA ** (-jnp.arange(0, MLA_ROPE, 2, dtype=F32) / MLA_ROPE)
    ang = pos.astype(F32)[:, None] * inv[None, :]
    cos, sin = jnp.cos(ang), jnp.sin(ang)
    return jnp.concatenate([cos, cos], -1), jnp.concatenate([-sin, sin], -1)


def _layer(x, p, rwt, rb, ckv_past, krope_past, s_past, conv_past, cbk_past, cbv_past):
    b, t, d = x.shape
    start = ckv_past.shape[1]
    x2d = x.reshape(b * t, d)
    proj = _in_proj(x2d, p["w_in"]).reshape(b, t, IN_PAD)

    cos2, sin2 = _rope_tables(start, t)
    q, ckv_new, krope_new = _mla_prep(proj, cos2, sin2, p["q_norm_g"], p["kv_norm_g"], p["wq_nope"], p["wq_rope"],
                                      p["wq_rope_sw"], p["wuk_t"])
    if start:
        ckv_all = jnp.concatenate([ckv_past, ckv_new], 1)
        kr_all = jnp.concatenate([krope_past, krope_new], 1)
    else:
        ckv_all, kr_all = ckv_new, krope_new
    o_a = _mla_attn(q, ckv_all, kr_all, p["wuv"], start)

    o_b, s_new, conv_new = _gdn(proj, conv_past, p["conv_w"], p["alog"], p["dtb"], p["gnorm"], s_past)

    cb_k = proj[:, :, C_CB + CB_DIM:C_CB + 2 * CB_DIM]
    cb_v = proj[:, :, C_CB + 2 * CB_DIM:]
    p_rows = cbk_past.shape[1]
    pad = CB_PAST_ROWS - p_rows
    zeros = jnp.zeros((b, pad, CB_DIM), F32)
    kp = jnp.concatenate([zeros, cbk_past.reshape(b, p_rows, CB_DIM), cb_k], 1)
    vp = jnp.concatenate([zeros, cbv_past.reshape(b, p_rows, CB_DIM), cb_v], 1)
    o_c = _cb_attn(proj, kp, vp, p["bias"], pad)

    x1, gate = _out_ln1(o_a.reshape(b * t, -1), o_b.reshape(b * t, -1), o_c.reshape(b * t, -1), x2d,
                        p["wo_a"], p["wo_b"], p["wo_c"], p["ln1_g"], p["ln1_b"], rwt, rb)
    x2 = _moe_ln2(x1, gate, p["w_gate"], p["w_up"], p["w_down"], p["ln2_g"], p["ln2_b"])

    keep = min(CB_PAST_ROWS, t)
    state = (ckv_new, krope_new, s_new, conv_new,
             cb_k[:, -keep:].reshape(b, keep, CB_HEADS, CB_DH), cb_v[:, -keep:].reshape(b, keep, CB_HEADS, CB_DH))
    return x2.reshape(b, t, d), state


def kernel(x_prompt, x_sample, cache_mla_ckv, cache_mla_krope, state_gdn, state_gdn_conv, cache_cb_k, cache_cb_v,
           w_in, q_norm_g, w_uq, kv_norm_g, w_uk, w_uv, conv_w, a_log, dt_bias, gdn_norm_g, rel_bias, w_out,
           ln1_g, ln1_b, router_w, router_b, w_gate, w_up, w_down, ln2_g, ln2_b):
    depth = w_in.shape[0]
    layers = [_prep_layer(w_in[l], q_norm_g[l], w_uq[l], kv_norm_g[l], w_uk[l], w_uv[l], conv_w[l], a_log[l],
                          dt_bias[l], gdn_norm_g[l], rel_bias[l], w_out[l], ln1_g[l], ln1_b[l],
                          w_gate[l], w_up[l], w_down[l], ln2_g[l], ln2_b[l]) for l in range(depth)]
    rwt = router_w.T
    rb = router_b[:, None]

    def run_trunk(x, ckv_c, krope_c, s_c, conv_c, cbk_c, cbv_c):
        new = ([], [], [], [], [], [])
        for l in range(depth):
            x, st = _layer(x, layers[l], rwt, rb, ckv_c[l], krope_c[l], s_c[l], conv_c[l], cbk_c[l], cbv_c[l])
            for lst, a in zip(new, st):
                lst.append(a)
        return (x, *[jnp.stack(a) for a in new])

    bp = x_prompt.shape[0]
    empty = lambda *s: jnp.zeros((depth, bp) + s, F32)
    outs_p = run_trunk(x_prompt, empty(0, MLA_KV_RANK), empty(0, MLA_ROPE), empty(GDN_HEADS, GDN_DK, GDN_DV),
                       empty(GDN_CONV - 1, GDN_CONV_DIM), empty(0, CB_HEADS, CB_DH), empty(0, CB_HEADS, CB_DH))
    outs_s = run_trunk(x_sample, cache_mla_ckv, cache_mla_krope, state_gdn, state_gdn_conv, cache_cb_k, cache_cb_v)
    return (outs_p[0], outs_s[0], *outs_p[1:], *outs_s[1:])
```

```python
import functools
import math

import jax
import jax.numpy as jnp
import numpy as np
from jax import lax
from jax.experimental import pallas as pl
from jax.experimental.pallas import tpu as pltpu

F32 = jnp.float32
BF16 = jnp.bfloat16

CHUNK = 64
MLA_HEADS = 6
MLA_Q_RANK = 512
MLA_KV_RANK = 256
MLA_NOPE = 128
MLA_ROPE = 64
MLA_V = 128
ROPE_THETA = 10000.0
GDN_HEADS = 6
GDN_DK = 128
GDN_DV = 128
GDN_CONV = 4
GDN_KEY_DIM = GDN_HEADS * GDN_DK
GDN_VAL_DIM = GDN_HEADS * GDN_DV
GDN_CONV_DIM = 2 * GDN_KEY_DIM + GDN_VAL_DIM
CB_HEADS = 4
CB_DH = 128
CB_DIM = CB_HEADS * CB_DH
CB_PAST_ROWS = 8 * CHUNK
REL_CLIP = 256
N_EXPERTS = 16
N_GROUPS = 4
EXPERTS_PER_GROUP = N_EXPERTS // N_GROUPS
DEPTH = 2
DEEPNORM_ALPHA = (2 * DEPTH) ** 0.25

LANES = 128
C_GQKV = 0
C_GZ = C_GQKV + GDN_CONV_DIM
C_CQ = C_GZ + GDN_VAL_DIM
C_CKV = C_CQ + MLA_Q_RANK
C_KR = C_CKV + MLA_KV_RANK
C_GBA = C_KR + 2 * MLA_ROPE
C_CB = C_GBA + LANES
IN_PAD = C_CB + 3 * CB_DIM

SLAB_ROWS = 2048 // LANES
EXT_SEL = 0
EXT_W_LO = N_EXPERTS

VMEM_LIMIT = 56 * 1024 * 1024


def _cparams(sem):
    return pltpu.CompilerParams(dimension_semantics=sem, vmem_limit_bytes=VMEM_LIMIT)


def _dot(a, b):
    return jnp.dot(a, b, preferred_element_type=F32)


def _dot_nt(a, b):
    return lax.dot_general(a, b, (((1,), (1,)), ((), ())), preferred_element_type=F32)


def _dot_hi(a, b):
    return jnp.dot(a, b, preferred_element_type=F32, precision=lax.Precision.HIGHEST)


def _dot_nt_hi(a, b):
    return lax.dot_general(a, b, (((1,), (1,)), ((), ())), preferred_element_type=F32,
                           precision=lax.Precision.HIGHEST)


def _sigmoid(x):
    return 1.0 / (1.0 + jnp.exp(-x))


def _silu(x):
    return x * _sigmoid(x)


def _layer_norm(h, g, b, eps=1e-5):
    mu = jnp.mean(h, -1, keepdims=True)
    d = h - mu
    var = jnp.mean(d * d, -1, keepdims=True)
    return d * lax.rsqrt(var + eps) * g + b


def _rms_norm(x, g, eps=1e-6):
    return x * lax.rsqrt(jnp.mean(x * x, -1, keepdims=True) + eps) * g


def _inproj_kernel(x_ref, w_ref, o_ref, xb_ref):
    @pl.when(pl.program_id(1) == 0)
    def _():
        xb_ref[...] = x_ref[...].astype(BF16)

    o_ref[...] = _dot(xb_ref[...], w_ref[...])


def _in_proj(x2d, w):
    n, d = x2d.shape
    width = w.shape[1]
    tm = min(1024, n)
    tn = 512
    return pl.pallas_call(
        _inproj_kernel,
        grid=(n // tm, width // tn),
        in_specs=[pl.BlockSpec((tm, d), lambda i, j: (i, 0)),
                  pl.BlockSpec((d, tn), lambda i, j: (0, j))],
        out_specs=pl.BlockSpec((tm, tn), lambda i, j: (i, j)),
        out_shape=jax.ShapeDtypeStruct((n, width), F32),
        scratch_shapes=[pltpu.VMEM((tm, d), BF16)],
        compiler_params=_cparams(("parallel", "arbitrary")),
        name="in_proj",
    )(x2d, w)


def _mla_prep_kernel(cq_ref, ckv_ref, kr_ref, cosq_ref, sinq_ref, cosk_ref, sink_ref,
                     qg_ref, kvg_ref, wqn_ref, wqr_ref, wqs_ref, wuk_ref,
                     q_ref, ckvn_ref, krn_ref):
    cqn = _rms_norm(cq_ref[0], qg_ref[...]).astype(BF16)
    q_nope = _dot(cqn, wqn_ref[...]).astype(BF16)
    q_rope = _dot(cqn, wqr_ref[...])
    q_rope_sw = _dot(cqn, wqs_ref[...])
    q_rot = (q_rope * cosq_ref[...] + q_rope_sw * sinq_ref[...]).astype(BF16)
    for h in range(MLA_HEADS):
        q_lat = _dot(q_nope[:, h * MLA_NOPE:(h + 1) * MLA_NOPE], wuk_ref[h])
        q_ref[0, h, :, 0:MLA_KV_RANK] = q_lat.astype(BF16)
        q_ref[0, h, :, MLA_KV_RANK:] = q_rot[:, h * MLA_ROPE:(h + 1) * MLA_ROPE]
    ckvn_ref[0] = _rms_norm(ckv_ref[0], kvg_ref[...])
    kr = kr_ref[0]
    krn_ref[0] = kr[:, :MLA_ROPE] * cosk_ref[...] + kr[:, MLA_ROPE:] * sink_ref[...]


def _mla_prep(proj, cos2, sin2, q_norm_g, kv_norm_g, wq_nope, wq_rope, wq_rope_sw, wuk_t):
    b, t, _ = proj.shape
    tm = min(512, t)
    cosq = jnp.tile(cos2, (1, MLA_HEADS))
    sinq = jnp.tile(sin2, (1, MLA_HEADS))
    full = lambda a: pl.BlockSpec(a.shape, lambda bi, i: (0,) * a.ndim)
    row = lambda w: pl.BlockSpec((tm, w), lambda bi, i: (i, 0))
    qd = MLA_KV_RANK + MLA_ROPE
    return pl.pallas_call(
        _mla_prep_kernel,
        grid=(b, t // tm),
        in_specs=[pl.BlockSpec((1, tm, MLA_Q_RANK), lambda bi, i: (bi, i, C_CQ // MLA_Q_RANK)),
                  pl.BlockSpec((1, tm, MLA_KV_RANK), lambda bi, i: (bi, i, C_CKV // MLA_KV_RANK)),
                  pl.BlockSpec((1, tm, 2 * MLA_ROPE), lambda bi, i: (bi, i, C_KR // (2 * MLA_ROPE))),
                  row(MLA_HEADS * MLA_ROPE), row(MLA_HEADS * MLA_ROPE), row(MLA_ROPE), row(MLA_ROPE),
                  full(q_norm_g), full(kv_norm_g), full(wq_nope), full(wq_rope), full(wq_rope_sw),
                  full(wuk_t)],
        out_specs=[pl.BlockSpec((1, MLA_HEADS, tm, qd), lambda bi, i: (bi, 0, i, 0)),
                   pl.BlockSpec((1, tm, MLA_KV_RANK), lambda bi, i: (bi, i, 0)),
                   pl.BlockSpec((1, tm, MLA_ROPE), lambda bi, i: (bi, i, 0))],
        out_shape=[jax.ShapeDtypeStruct((b, MLA_HEADS, t, qd), BF16),
                   jax.ShapeDtypeStruct((b, t, MLA_KV_RANK), F32),
                   jax.ShapeDtypeStruct((b, t, MLA_ROPE), F32)],
        compiler_params=_cparams(("parallel", "parallel")),
        name="mla_prep",
    )(proj, proj, proj, cosq, sinq, cos2, sin2, q_norm_g, kv_norm_g, wq_nope, wq_rope, wq_rope_sw, wuk_t)


def _mla_attn_kernel(qi_ref, kj_ref, q_ref, ckv_ref, kr_ref, wuv_ref, o_ref, m_ref, l_ref, acc_ref, *, start, tq, tk):
    step = pl.program_id(1)
    i = qi_ref[step]
    j = kj_ref[step]

    @pl.when(j == 0)
    def _():
        m_ref[...] = jnp.full(m_ref.shape, -jnp.inf, F32)
        l_ref[...] = jnp.zeros(l_ref.shape, F32)
        acc_ref[...] = jnp.zeros(acc_ref.shape, F32)

    ckv = ckv_ref[0].astype(BF16)
    kr = kr_ref[0].astype(BF16)
    scale = (MLA_NOPE + MLA_ROPE) ** -0.5

    hpg = 1 if tq >= 256 else MLA_HEADS
    grows = hpg * tq

    def update(masked):
        if masked:
            q_pos = start + i * tq + lax.broadcasted_iota(jnp.int32, (grows, 1), 0) % tq
            k_pos = j * tk + lax.broadcasted_iota(jnp.int32, (1, tk), 1)
            visible = k_pos // CHUNK <= q_pos // CHUNK
        def scores(g):
            q = q_ref[0, g] if hpg == 1 else q_ref[0].reshape(grows, MLA_KV_RANK + MLA_ROPE)
            return _dot_nt(q[:, :MLA_KV_RANK], ckv) + _dot_nt(q[:, MLA_KV_RANK:], kr)

        groups = MLA_HEADS // hpg
        s_next = scores(0)
        for g in range(groups):
            s, s_next = s_next, (scores(g + 1) if g + 1 < groups else None)
            s = s * scale
            if masked:
                s = jnp.where(visible, s, -jnp.inf)
            rows = slice(g * grows, (g + 1) * grows)
            m_old = m_ref[rows]
            m_new = jnp.maximum(m_old, jnp.max(s, -1, keepdims=True))
            alpha = jnp.exp(m_old - m_new)
            p = jnp.exp(s - m_new)
            l_ref[rows] = alpha * l_ref[rows] + jnp.sum(p, -1, keepdims=True)
            acc_ref[rows] = alpha * acc_ref[rows] + _dot(p.astype(BF16), ckv)
            m_ref[rows] = m_new

    needs_mask = (j * tk + tk - 1) // CHUNK > (start + i * tq) // CHUNK
    pl.when(needs_mask)(lambda: update(True))
    pl.when(jnp.logical_not(needs_mask))(lambda: update(False))

    @pl.when(j == _last_kv_block(start, i, tq, tk))
    def _():
        o_lat = (acc_ref[...] / l_ref[...]).astype(BF16)
        for h in range(MLA_HEADS):
            o_ref[0, :, h * MLA_V:(h + 1) * MLA_V] = _dot(o_lat[h * tq:(h + 1) * tq], wuv_ref[h]).astype(BF16)


def _last_kv_block(start, i, tq, tk):
    return ((start + i * tq + tq - 1) // CHUNK * CHUNK) // tk


def _pick_tk(s):
    for cand in (512, 1024, 832, 768, 640, 576, 448, 384, 320, 256, 192, 128, 64):
        if s % cand == 0:
            return cand
    raise ValueError(f"unsupported key length {s}")


def _mla_attn(q, ckv_all, kr_all, wuv, start):
    b, _, t, qd = q.shape
    s = ckv_all.shape[1]
    tq = min(256, t)
    tk = _pick_tk(s)
    assert tk % CHUNK == 0 and s % tk == 0 and t % tq == 0
    pairs = [(i, j) for i in range(t // tq) for j in range(_last_kv_block(start, i, tq, tk) + 1)]
    qi = jnp.asarray(np.array([p[0] for p in pairs], np.int32))
    kj = jnp.asarray(np.array([p[1] for p in pairs], np.int32))
    kv_map = lambda bi, st, qi_r, kj_r: (bi, kj_r[st], 0)
    return pl.pallas_call(
        functools.partial(_mla_attn_kernel, start=start, tq=tq, tk=tk),
        grid_spec=pltpu.PrefetchScalarGridSpec(
            num_scalar_prefetch=2, grid=(b, len(pairs)),
            in_specs=[pl.BlockSpec((1, MLA_HEADS, tq, qd), lambda bi, st, qi_r, kj_r: (bi, 0, qi_r[st], 0)),
                      pl.BlockSpec((1, tk, MLA_KV_RANK), kv_map),
                      pl.BlockSpec((1, tk, MLA_ROPE), kv_map),
                      pl.BlockSpec(wuv.shape, lambda bi, st, qi_r, kj_r: (0, 0, 0))],
            out_specs=pl.BlockSpec((1, tq, MLA_HEADS * MLA_V), lambda bi, st, qi_r, kj_r: (bi, qi_r[st], 0)),
            scratch_shapes=[pltpu.VMEM((MLA_HEADS * tq, 1), F32),
                            pltpu.VMEM((MLA_HEADS * tq, 1), F32),
                            pltpu.VMEM((MLA_HEADS * tq, MLA_KV_RANK), F32)]),
        out_shape=jax.ShapeDtypeStruct((b, t, MLA_HEADS * MLA_V), BF16),
        compiler_params=_cparams(("parallel", "arbitrary")),
        name="mla_attn",
    )(qi, kj, q, ckv_all, kr_all, wuv)


def _split_bf16(x):
    hi = x.astype(BF16)
    return hi, (x - hi.astype(F32)).astype(BF16)


def _unit_lower_solve_many(a_list, rhs_list):
    n = rhs_list[0].shape[1]
    levels = int(math.log2(CHUNK))
    xs, ps = list(rhs_list), list(a_list)
    for lvl in range(levels):
        for h in range(len(xs)):
            p_hi, p_lo = _split_bf16(ps[h])
            lhs = jnp.concatenate([p_hi, p_hi, p_lo], 1)
            if lvl < levels - 1:
                r_hi, r_lo = _split_bf16(jnp.concatenate([xs[h], ps[h]], 1))
                both = _dot(lhs, jnp.concatenate([r_hi, r_lo, r_hi], 0))
                px, ps[h] = both[:, :n], both[:, n:]
            else:
                r_hi, r_lo = _split_bf16(xs[h])
                px = _dot(lhs, jnp.concatenate([r_hi, r_lo, r_hi], 0))
            xs[h] = xs[h] - px if lvl == 0 else xs[h] + px
    return xs


def _gdn_kernel(qkv_ref, gz_ref, gba_ref, convp_ref, convw_ref, alog_ref, dtb_ref, gnorm_ref, s0_ref,
                ob_ref, snew_ref, convn_ref, s_scr, ext_scr, *, n_chunks):
    n = pl.program_id(1)
    L = CHUNK
    tail = 8

    @pl.when(n == 0)
    def _():
        s_scr[...] = s0_ref[0]
        ext_scr[0:tail, :] = jnp.zeros((tail, GDN_CONV_DIM), F32)
        ext_scr[tail - (GDN_CONV - 1):tail, :] = convp_ref[0]

    cur = qkv_ref[0]
    ext_scr[tail:tail + L, :] = cur
    w = convw_ref[...]
    conv = ext_scr[tail - 3:tail - 3 + L, :] * w[0:1]
    conv = conv + ext_scr[tail - 2:tail - 2 + L, :] * w[1:2]
    conv = conv + ext_scr[tail - 1:tail - 1 + L, :] * w[2:3]
    conv = conv + cur * w[3:4]
    conv = _silu(conv)

    @pl.when(n == n_chunks - 1)
    def _():
        convn_ref[0] = ext_scr[tail + L - (GDN_CONV - 1):tail + L, :]

    ext_scr[0:tail, :] = cur[L - tail:, :]

    gba = gba_ref[0]
    beta_all = _sigmoid(gba)
    z = gba + dtb_ref[...]
    softplus = jnp.maximum(z, 0.0) + jnp.log1p(jnp.exp(-jnp.abs(z)))
    g_all = -jnp.exp(alog_ref[...]) * softplus
    ri = lax.broadcasted_iota(jnp.int32, (L, L), 0)
    ci = lax.broadcasted_iota(jnp.int32, (L, L), 1)
    incl = ci <= ri
    strict = ci < ri
    g_cum = _dot_hi(incl.astype(F32), g_all)
    g_cum_t = g_cum.T

    def l2n(x):
        return x * lax.rsqrt(jnp.sum(x * x, -1, keepdims=True) + 1e-6)

    heads = range(GDN_HEADS)
    q = [l2n(conv[:, h * GDN_DK:(h + 1) * GDN_DK]) * (GDN_DK ** -0.5) for h in heads]
    k = [l2n(conv[:, GDN_KEY_DIM + h * GDN_DK:GDN_KEY_DIM + (h + 1) * GDN_DK]) for h in heads]
    v = [conv[:, 2 * GDN_KEY_DIM + h * GDN_DV:2 * GDN_KEY_DIM + (h + 1) * GDN_DV] for h in heads]
    beta = [beta_all[:, h:h + 1] for h in heads]
    gc = [g_cum[:, GDN_HEADS + h:GDN_HEADS + h + 1] for h in heads]
    dmat = [jnp.exp(jnp.where(incl, gc[h] - g_cum_t[GDN_HEADS + h:GDN_HEADS + h + 1, :], -jnp.inf)) for h in heads]
    eg = [jnp.exp(gc[h]) for h in heads]
    qk_kk = []
    for h in heads:
        kb = k[h].astype(BF16)
        qk_kk.append(_dot_nt(jnp.concatenate([q[h].astype(BF16), kb], 0), kb))
    a_mat = [jnp.where(strict, beta[h] * qk_kk[h][L:] * dmat[h], 0.0) for h in heads]
    rhs = [jnp.concatenate([beta[h] * v[h], (beta[h] * eg[h]) * k[h]], -1) for h in heads]
    sol = _unit_lower_solve_many(a_mat, rhs)
    s_old = [s_scr[h] for h in heads]
    wq_s = [_dot(jnp.concatenate([sol[h][:, GDN_DV:], q[h]], 0).astype(BF16), s_old[h].astype(BF16)) for h in heads]
    upd = []
    for h in heads:
        delta = sol[h][:, :GDN_DV] - wq_s[h][:L]
        kd = k[h] * jnp.exp(gc[h][L - 1:L, :] - gc[h])
        lhs = jnp.concatenate([qk_kk[h][:L] * dmat[h], kd.T], 0)
        upd.append(_dot(lhs.astype(BF16), delta.astype(BF16)))
    for h in heads:
        o = eg[h] * wq_s[h][L:] + upd[h][:L]
        s_scr[h] = jnp.exp(gc[h][L - 1:L, :]) * s_old[h] + upd[h][L:]
        gz = gz_ref[0, :, h * GDN_DV:(h + 1) * GDN_DV]
        ob_ref[0, :, h * GDN_DV:(h + 1) * GDN_DV] = (_rms_norm(o, gnorm_ref[...]) * _silu(gz)).astype(BF16)

    @pl.when(n == n_chunks - 1)
    def _():
        snew_ref[0] = s_scr[...]


def _gdn(proj, conv_past, conv_w, alog128, dtb128, gnorm, s0):
    b, t, _ = proj.shape
    L = CHUNK
    full = lambda a: pl.BlockSpec(a.shape, lambda bi, n: (0,) * a.ndim)
    return pl.pallas_call(
        functools.partial(_gdn_kernel, n_chunks=t // L),
        grid=(b, t // L),
        in_specs=[pl.BlockSpec((1, L, GDN_CONV_DIM), lambda bi, n: (bi, n, C_GQKV // GDN_CONV_DIM)),
                  pl.BlockSpec((1, L, GDN_VAL_DIM), lambda bi, n: (bi, n, C_GZ // GDN_VAL_DIM)),
                  pl.BlockSpec((1, L, LANES), lambda bi, n: (bi, n, C_GBA // LANES)),
                  pl.BlockSpec((1, GDN_CONV - 1, GDN_CONV_DIM), lambda bi, n: (bi, 0, 0)),
                  full(conv_w), full(alog128), full(dtb128), full(gnorm),
                  pl.BlockSpec((1, GDN_HEADS, GDN_DK, GDN_DV), lambda bi, n: (bi, 0, 0, 0))],
        out_specs=[pl.BlockSpec((1, L, GDN_VAL_DIM), lambda bi, n: (bi, n, 0)),
                   pl.BlockSpec((1, GDN_HEADS, GDN_DK, GDN_DV), lambda bi, n: (bi, 0, 0, 0)),
                   pl.BlockSpec((1, GDN_CONV - 1, GDN_CONV_DIM), lambda bi, n: (bi, 0, 0))],
        out_shape=[jax.ShapeDtypeStruct((b, t, GDN_VAL_DIM), BF16),
                   jax.ShapeDtypeStruct((b, GDN_HEADS, GDN_DK, GDN_DV), F32),
                   jax.ShapeDtypeStruct((b, GDN_CONV - 1, GDN_CONV_DIM), F32)],
        scratch_shapes=[pltpu.VMEM((GDN_HEADS, GDN_DK, GDN_DV), F32),
                        pltpu.VMEM((8 + L, GDN_CONV_DIM), F32)],
        compiler_params=_cparams(("parallel", "arbitrary")),
        name="gdn",
    )(proj, proj, proj, conv_past, conv_w, alog128, dtb128, gnorm, s0)


def _cb_attn_kernel(q_ref, kprev_ref, kcur_ref, vprev_ref, vcur_ref, bias_ref, o_ref, *, tq, pad):
    i = pl.program_id(1)
    L = CHUNK
    width = CB_PAST_ROWS + L
    scale = CB_DH ** -0.5
    for c in range(tq // L):
        lo = c * L
        kwin = jnp.concatenate([kprev_ref[0, lo:, :], kcur_ref[0, :lo + L, :]], 0).astype(BF16)
        vwin = jnp.concatenate([vprev_ref[0, lo:, :], vcur_ref[0, :lo + L, :]], 0).astype(BF16)
        q = q_ref[0, lo:lo + L, :].astype(BF16)
        row = i * tq + lo + lax.broadcasted_iota(jnp.int32, (1, width), 1)
        valid = row >= pad
        for h in range(CB_HEADS):
            hs = slice(h * CB_DH, (h + 1) * CB_DH)
            s = _dot_nt(q[:, hs], kwin[:, hs]) * scale + bias_ref[h]
            s = jnp.where(valid, s, -jnp.inf)
            m = jnp.max(s, -1, keepdims=True)
            p = jnp.exp(s - m)
            p = (p / jnp.sum(p, -1, keepdims=True)).astype(BF16)
            o_ref[0, lo:lo + L, hs] = _dot(p, vwin[:, hs]).astype(BF16)


def _cb_attn(proj, kp, vp, bias, pad):
    b, t, _ = proj.shape
    tq = min(CB_PAST_ROWS, t)
    assert tq == CB_PAST_ROWS or tq == t
    assert CB_PAST_ROWS % tq == 0
    prev_spec = pl.BlockSpec((1, CB_PAST_ROWS, CB_DIM), lambda bi, i: (bi, i, 0))
    cur_spec = pl.BlockSpec((1, tq, CB_DIM), lambda bi, i: (bi, CB_PAST_ROWS // tq + i, 0))
    return pl.pallas_call(
        functools.partial(_cb_attn_kernel, tq=tq, pad=pad),
        grid=(b, t // tq),
        in_specs=[pl.BlockSpec((1, tq, CB_DIM), lambda bi, i: (bi, i, C_CB // CB_DIM)),
                  prev_spec, cur_spec, prev_spec, cur_spec,
                  pl.BlockSpec(bias.shape, lambda bi, i: (0, 0, 0))],
        out_specs=pl.BlockSpec((1, tq, CB_DIM), lambda bi, i: (bi, i, 0)),
        out_shape=jax.ShapeDtypeStruct((b, t, CB_DIM), BF16),
        compiler_params=_cparams(("parallel", "parallel")),
        name="cb_attn",
    )(proj, kp, kp, vp, vp, bias)


def _route(logits_t, rb):
    s = _sigmoid(logits_t)
    sb = s + rb
    rows = [sb[e:e + 1, :] for e in range(N_EXPERTS)]
    grp = []
    for g in range(N_GROUPS):
        r = rows[g * EXPERTS_PER_GROUP:(g + 1) * EXPERTS_PER_GROUP]
        best = None
        for a in range(EXPERTS_PER_GROUP):
            for c in range(a + 1, EXPERTS_PER_GROUP):
                pair = r[a] + r[c]
                best = pair if best is None else jnp.maximum(best, pair)
        grp.append(best)
    gmax = functools.reduce(jnp.maximum, grp)
    gsel = jnp.full(gmax.shape, N_GROUPS, jnp.int32)
    for g in reversed(range(N_GROUPS)):
        gsel = jnp.where(grp[g] == gmax, g, gsel)
    sel = []
    for e in range(N_EXPERTS):
        g = e // EXPERTS_PER_GROUP
        rank = jnp.zeros(gmax.shape, jnp.int32)
        for e2 in range(g * EXPERTS_PER_GROUP, (g + 1) * EXPERTS_PER_GROUP):
            if e2 == e:
                continue
            ahead = (rows[e2] >= rows[e]) if e2 < e else (rows[e2] > rows[e])
            rank = rank + ahead.astype(jnp.int32)
        sel.append(jnp.where((gsel == g) & (rank < 2), 1.0, 0.0))
    ssum = functools.reduce(lambda a, c: a + c, [sel[e] * s[e:e + 1, :] for e in range(N_EXPERTS)])
    zero = jnp.zeros(gmax.shape, F32)
    seen, w_lo, w_hi, e_lo, e_hi = zero, zero, zero, zero, zero
    for e in range(N_EXPERTS):
        gate_e = sel[e] * s[e:e + 1, :] / ssum
        first = sel[e] * jnp.where(seen == 0.0, 1.0, 0.0)
        second = sel[e] - first
        w_lo, w_hi = w_lo + first * gate_e, w_hi + second * gate_e
        e_lo, e_hi = e_lo + first * e, e_hi + second * e
        seen = seen + sel[e]
    return sel, w_lo, w_hi, e_lo, e_hi


def _to_slab(ref, val):
    tm = val.shape[0]
    for c in range(val.shape[1] // LANES):
        ref[pl.ds(c, tm, stride=SLAB_ROWS), :] = val[:, c * LANES:(c + 1) * LANES]


def _out_ln1_kernel(oa_ref, ob_ref, oc_ref, x_ref, wa_ref, wb_ref, wc_ref, g_ref, b_ref, rwt_ref, rb_ref,
                    x1_ref, ext_ref):
    y = _dot(oa_ref[...], wa_ref[...]) + _dot(ob_ref[...], wb_ref[...]) + _dot(oc_ref[...], wc_ref[...])
    x1 = _layer_norm(DEEPNORM_ALPHA * x_ref[...] + y, g_ref[...], b_ref[...])
    _to_slab(x1_ref, x1)
    logits_t = _dot_nt_hi(rwt_ref[...], x1)
    sel, w_lo, w_hi, e_lo, e_hi = _route(logits_t, rb_ref[...])
    tm = x1.shape[0]
    rows = sel + [w_lo, w_hi, e_lo, e_hi]
    ext = jnp.concatenate(rows + [jnp.zeros((LANES - len(rows), tm), F32)], 0)
    ext_ref[...] = ext.T


def _out_ln1(oa, ob, oc, x2d, wa, wb, wc, g, bb, rwt, rb):
    n, d = x2d.shape
    tm = min(256, n)
    full = lambda a: pl.BlockSpec(a.shape, lambda i: (0,) * a.ndim)
    row = lambda w: pl.BlockSpec((tm, w), lambda i: (i, 0))
    return pl.pallas_call(
        _out_ln1_kernel,
        grid=(n // tm,),
        in_specs=[row(oa.shape[1]), row(ob.shape[1]), row(oc.shape[1]), row(d),
                  full(wa), full(wb), full(wc), full(g), full(bb), full(rwt), full(rb)],
        out_specs=[pl.BlockSpec((tm * SLAB_ROWS, LANES), lambda i: (i, 0)), row(LANES)],
        out_shape=[jax.ShapeDtypeStruct((n * SLAB_ROWS, LANES), F32), jax.ShapeDtypeStruct((n, LANES), F32)],
        compiler_params=_cparams(("parallel",)),
        name="out_ln1",
    )(oa, ob, oc, x2d, wa, wb, wc, g, bb, rwt, rb)


def _row_copy(src_hbm, src_row, dst, dst_row, sem):
    return pltpu.make_async_copy(src_hbm.at[pl.ds(pl.multiple_of(src_row * SLAB_ROWS, SLAB_ROWS), SLAB_ROWS)],
                                 dst.at[pl.ds(pl.multiple_of(dst_row * SLAB_ROWS, SLAB_ROWS), SLAB_ROWS)], sem)


def _moe_gather_kernel(src_ref, x_hbm, o_ref, sem, *, rows):
    base = pl.program_id(0) * rows

    def issue(r, c):
        _row_copy(x_hbm, src_ref[base + r], o_ref, r, sem).start()
        return c

    lax.fori_loop(0, rows, issue, 0, unroll=8)

    def wait(r, c):
        _row_copy(x_hbm, 0, o_ref, r, sem).wait()
        return c

    lax.fori_loop(0, rows, wait, 0, unroll=8)


def _moe_gather(src, x_slab, n_rows, rows):
    return pl.pallas_call(
        functools.partial(_moe_gather_kernel, rows=rows),
        grid_spec=pltpu.PrefetchScalarGridSpec(
            num_scalar_prefetch=1, grid=(n_rows // rows,),
            in_specs=[pl.BlockSpec(memory_space=pl.ANY)],
            out_specs=pl.BlockSpec((rows * SLAB_ROWS, LANES), lambda i, s: (i, 0)),
            scratch_shapes=[pltpu.SemaphoreType.DMA(())]),
        out_shape=jax.ShapeDtypeStruct((n_rows * SLAB_ROWS, LANES), F32),
        compiler_params=_cparams(("arbitrary",)),
        name="moe_gather",
    )(src, x_slab)


def _moe_ffn_kernel(tile_e_ref, nact_ref, xs_ref, wg_ref, wu_ref, wd_ref, ys_ref, xb_ref, acc_ref):
    i = pl.program_id(0)
    f = pl.program_id(1)
    tm = xb_ref.shape[0]

    @pl.when(i < nact_ref[0])
    def _():
        @pl.when(f == 0)
        def _():
            for c in range(xb_ref.shape[1] // LANES):
                xb_ref[:, c * LANES:(c + 1) * LANES] = xs_ref[pl.ds(c, tm, stride=SLAB_ROWS), :].astype(BF16)

        xb = xb_ref[...]
        h = (_silu(_dot(xb, wg_ref[0])) * _dot(xb, wu_ref[0])).astype(BF16)
        y = _dot(h, wd_ref[0])

        @pl.when(f == 0)
        def _():
            acc_ref[...] = y

        @pl.when(f > 0)
        def _():
            acc_ref[...] += y

        @pl.when(f == pl.num_programs(1) - 1)
        def _():
            _to_slab(ys_ref, acc_ref[...])

    @pl.when((i >= nact_ref[0]) & (f == pl.num_programs(1) - 1))
    def _():
        ys_ref[...] = jnp.zeros(ys_ref.shape, F32)


def _moe_ffn(tile_e, nact, xs, wg, wu, wd, tm):
    n_rows = xs.shape[0] // SLAB_ROWS
    ne, d, ff = wg.shape
    tf = min(512, ff)
    nf = ff // tf
    row = lambda i, f, te, na: (jnp.maximum(jnp.minimum(i, na[0] - 1), 0), 0)
    fe = lambda i, f, na: jnp.where(i < na[0], f, nf - 1)
    return pl.pallas_call(
        _moe_ffn_kernel,
        grid_spec=pltpu.PrefetchScalarGridSpec(
            num_scalar_prefetch=2, grid=(n_rows // tm, nf),
            in_specs=[pl.BlockSpec((tm * SLAB_ROWS, LANES), row),
                      pl.BlockSpec((1, d, tf), lambda i, f, te, na: (te[i], 0, fe(i, f, na))),
                      pl.BlockSpec((1, d, tf), lambda i, f, te, na: (te[i], 0, fe(i, f, na))),
                      pl.BlockSpec((1, tf, d), lambda i, f, te, na: (te[i], fe(i, f, na), 0))],
            out_specs=pl.BlockSpec((tm * SLAB_ROWS, LANES), lambda i, f, te, na: (i, 0)),
            scratch_shapes=[pltpu.VMEM((tm, d), BF16), pltpu.VMEM((tm, d), F32)]),
        out_shape=jax.ShapeDtypeStruct(xs.shape, F32),
        compiler_params=_cparams(("arbitrary", "arbitrary")),
        name="moe_ffn",
    )(tile_e, nact, xs, wg, wu, wd)


def _moe_combine_kernel(pos0_ref, pos1_ref, x1_ref, ext_ref, ys_hbm, g_ref, b_ref, o_ref,
                        buf0_ref, buf1_ref, h_ref, sem):
    tm = o_ref.shape[0]
    base = pl.program_id(0) * tm

    def issue(r, c):
        _row_copy(ys_hbm, pos0_ref[base + r], buf0_ref, r, sem.at[0]).start()
        _row_copy(ys_hbm, pos1_ref[base + r], buf1_ref, r, sem.at[1]).start()
        return c

    lax.fori_loop(0, tm, issue, 0, unroll=8)

    def wait(r, c):
        _row_copy(ys_hbm, 0, buf0_ref, r, sem.at[0]).wait()
        _row_copy(ys_hbm, 0, buf1_ref, r, sem.at[1]).wait()
        return c

    lax.fori_loop(0, tm, wait, 0, unroll=8)

    ext = ext_ref[...]
    w_lo = ext[:, EXT_W_LO:EXT_W_LO + 1]
    w_hi = ext[:, EXT_W_LO + 1:EXT_W_LO + 2]
    for c in range(o_ref.shape[1] // LANES):
        rows = pl.ds(c, tm, stride=SLAB_ROWS)
        y = w_lo * buf0_ref[rows, :] + w_hi * buf1_ref[rows, :]
        h_ref[:, c * LANES:(c + 1) * LANES] = DEEPNORM_ALPHA * x1_ref[rows, :] + y
    o_ref[...] = _layer_norm(h_ref[...], g_ref[...], b_ref[...])


def _moe_combine(pos0, pos1, x1_slab, ext, ys, g, bb):
    n = ext.shape[0]
    d = g.shape[1]
    tm = min(256, n)
    return pl.pallas_call(
        _moe_combine_kernel,
        grid_spec=pltpu.PrefetchScalarGridSpec(
            num_scalar_prefetch=2, grid=(n // tm,),
            in_specs=[pl.BlockSpec((tm * SLAB_ROWS, LANES), lambda i, p0, p1: (i, 0)),
                      pl.BlockSpec((tm, LANES), lambda i, p0, p1: (i, 0)),
                      pl.BlockSpec(memory_space=pl.ANY),
                      pl.BlockSpec(g.shape, lambda i, p0, p1: (0, 0)),
                      pl.BlockSpec(bb.shape, lambda i, p0, p1: (0, 0))],
            out_specs=pl.BlockSpec((tm, d), lambda i, p0, p1: (i, 0)),
            scratch_shapes=[pltpu.VMEM((tm * SLAB_ROWS, LANES), F32), pltpu.VMEM((tm * SLAB_ROWS, LANES), F32),
                            pltpu.VMEM((tm, d), F32), pltpu.SemaphoreType.DMA((2,))]),
        out_shape=jax.ShapeDtypeStruct((n, d), F32),
        compiler_params=_cparams(("arbitrary",)),
        name="moe_combine",
    )(pos0, pos1, x1_slab, ext, ys, g, bb)


def _moe_ln2(x1_slab, ext, wg, wu, wd, g, bb):
    n = ext.shape[0]
    tm = 512 if n >= 8192 else 256
    n_rows = 2 * n + N_EXPERTS * tm
    sel = (ext[:, EXT_SEL:EXT_SEL + N_EXPERTS] > 0.5).astype(jnp.int32)
    csum = jnp.cumsum(sel, axis=0)
    counts = csum[-1]
    padded = (counts + tm - 1) // tm * tm
    seg_end = jnp.cumsum(padded)
    slot = (seg_end - padded)[None, :] + csum - sel
    e_lo = ext[:, EXT_W_LO + 2].astype(jnp.int32)
    e_hi = ext[:, EXT_W_LO + 3].astype(jnp.int32)
    pos0 = jnp.take_along_axis(slot, e_lo[:, None], 1)[:, 0]
    pos1 = jnp.take_along_axis(slot, e_hi[:, None], 1)[:, 0]
    tok = jnp.arange(n, dtype=jnp.int32)
    src = jnp.zeros((n_rows,), jnp.int32).at[pos0].set(tok).at[pos1].set(tok)
    total = seg_end[-1:]
    nact = total // tm
    tiles = jnp.arange(n_rows // tm, dtype=jnp.int32)
    tile_e = jnp.minimum(jnp.searchsorted(seg_end, jnp.minimum(tiles, nact[0] - 1) * tm, side="right"),
                         N_EXPERTS - 1).astype(jnp.int32)

    xs = _moe_gather(src, x1_slab, n_rows, tm)
    ys = _moe_ffn(tile_e, nact, xs, wg, wu, wd, tm)
    return _moe_combine(pos0, pos1, x1_slab, ext, ys, g, bb)


def _prep_layer(w_in, q_norm_g, w_uq, kv_norm_g, w_uk, w_uv, conv_w, a_log, dt_bias, gdn_norm_g, rel_bias, w_out,
                ln1_g, ln1_b, w_gate, w_up, w_down, ln2_g, ln2_b):
    splits = np.cumsum([MLA_Q_RANK, MLA_KV_RANK, MLA_ROPE, GDN_KEY_DIM, GDN_KEY_DIM, GDN_VAL_DIM, GDN_VAL_DIM,
                        GDN_HEADS, GDN_HEADS])
    cq, ckv, kr, gq, gk, gv, gz, gb, ga, cqkv = jnp.split(w_in, splits, axis=1)
    half = MLA_ROPE // 2
    kr_sw = jnp.concatenate([kr[:, half:], kr[:, :half]], 1)
    gba = jnp.pad(jnp.concatenate([gb, ga], 1), ((0, 0), (0, LANES - 2 * GDN_HEADS)))
    w_in_p = jnp.concatenate([gq, gk, gv, gz, cq, ckv, kr, kr_sw, gba, cqkv], 1).astype(BF16)
    r = w_uq.shape[0]
    wq_nope = w_uq[:, :, :MLA_NOPE].reshape(r, MLA_HEADS * MLA_NOPE).astype(BF16)
    wq_r = w_uq[:, :, MLA_NOPE:]
    wq_rope = wq_r.reshape(r, MLA_HEADS * MLA_ROPE).astype(BF16)
    wq_rope_sw = jnp.concatenate([wq_r[..., half:], wq_r[..., :half]], -1).reshape(r, MLA_HEADS * MLA_ROPE).astype(BF16)
    wuk_t = jnp.transpose(w_uk, (1, 2, 0)).astype(BF16)
    wuv = jnp.transpose(w_uv, (1, 0, 2)).astype(BF16)
    lane_pad = lambda a: jnp.pad(a, (GDN_HEADS, LANES - 2 * GDN_HEADS))[None, :]
    rel = CB_PAST_ROWS + np.arange(CHUNK)[:, None] - np.arange(CB_PAST_ROWS + CHUNK)[None, :]
    bias = rel_bias[:, np.clip(rel, -REL_CLIP, REL_CLIP) + REL_CLIP].astype(F32)
    w_out_b = w_out.astype(BF16)
    na = MLA_HEADS * MLA_V
    return dict(
        w_in=w_in_p, q_norm_g=q_norm_g[None, :], kv_norm_g=kv_norm_g[None, :],
        wq_nope=wq_nope, wq_rope=wq_rope, wq_rope_sw=wq_rope_sw, wuk_t=wuk_t, wuv=wuv,
        conv_w=conv_w, alog=lane_pad(a_log), dtb=lane_pad(dt_bias), gnorm=gdn_norm_g[None, :], bias=bias,
        wo_a=w_out_b[:na], wo_b=w_out_b[na:na + GDN_VAL_DIM], wo_c=w_out_b[na + GDN_VAL_DIM:],
        ln1_g=ln1_g[None, :], ln1_b=ln1_b[None, :], ln2_g=ln2_g[None, :], ln2_b=ln2_b[None, :],
        w_gate=w_gate.astype(BF16), w_up=w_up.astype(BF16), w_down=w_down.astype(BF16))


def _rope_tables(start, t):
    pos = start + jnp.arange(t, dtype=jnp.int32)
    inv = ROPE_THETA ** (-jnp.arange(0, MLA_ROPE, 2, dtype=F32) / MLA_ROPE)
    ang = pos.astype(F32)[:, None] * inv[None, :]
    cos, sin = jnp.cos(ang), jnp.sin(ang)
    return jnp.concatenate([cos, cos], -1), jnp.concatenate([-sin, sin], -1)


def _layer(x, p, rwt, rb, ckv_past, krope_past, s_past, conv_past, cbk_past, cbv_past):
    b, t, d = x.shape
    start = ckv_past.shape[1]
    x2d = x.reshape(b * t, d)
    proj = _in_proj(x2d, p["w_in"]).reshape(b, t, IN_PAD)

    cos2, sin2 = _rope_tables(start, t)
    q, ckv_new, krope_new = _mla_prep(proj, cos2, sin2, p["q_norm_g"], p["kv_norm_g"], p["wq_nope"], p["wq_rope"],
                                      p["wq_rope_sw"], p["wuk_t"])
    if start:
        ckv_all = jnp.concatenate([ckv_past, ckv_new], 1)
        kr_all = jnp.concatenate([krope_past, krope_new], 1)
    else:
        ckv_all, kr_all = ckv_new, krope_new
    o_a = _mla_attn(q, ckv_all, kr_all, p["wuv"], start)

    o_b, s_new, conv_new = _gdn(proj, conv_past, p["conv_w"], p["alog"], p["dtb"], p["gnorm"], s_past)

    cb_k = proj[:, :, C_CB + CB_DIM:C_CB + 2 * CB_DIM]
    cb_v = proj[:, :, C_CB + 2 * CB_DIM:]
    p_rows = cbk_past.shape[1]
    pad = CB_PAST_ROWS - p_rows
    zeros = jnp.zeros((b, pad, CB_DIM), F32)
    kp = jnp.concatenate([zeros, cbk_past.reshape(b, p_rows, CB_DIM), cb_k], 1)
    vp = jnp.concatenate([zeros, cbv_past.reshape(b, p_rows, CB_DIM), cb_v], 1)
    o_c = _cb_attn(proj, kp, vp, p["bias"], pad)

    x1, gate = _out_ln1(o_a.reshape(b * t, -1), o_b.reshape(b * t, -1), o_c.reshape(b * t, -1), x2d,
                        p["wo_a"], p["wo_b"], p["wo_c"], p["ln1_g"], p["ln1_b"], rwt, rb)
    x2 = _moe_ln2(x1, gate, p["w_gate"], p["w_up"], p["w_down"], p["ln2_g"], p["ln2_b"])

    keep = min(CB_PAST_ROWS, t)
    state = (ckv_new, krope_new, s_new, conv_new,
             cb_k[:, -keep:].reshape(b, keep, CB_HEADS, CB_DH), cb_v[:, -keep:].reshape(b, keep, CB_HEADS, CB_DH))
    return x2.reshape(b, t, d), state


def kernel(x_prompt, x_sample, cache_mla_ckv, cache_mla_krope, state_gdn, state_gdn_conv, cache_cb_k, cache_cb_v,
           w_in, q_norm_g, w_uq, kv_norm_g, w_uk, w_uv, conv_w, a_log, dt_bias, gdn_norm_g, rel_bias, w_out,
           ln1_g, ln1_b, router_w, router_b, w_gate, w_up, w_down, ln2_g, ln2_b):
    depth = w_in.shape[0]
    layers = [_prep_layer(w_in[l], q_norm_g[l], w_uq[l], kv_norm_g[l], w_uk[l], w_uv[l], conv_w[l], a_log[l],
                          dt_bias[l], gdn_norm_g[l], rel_bias[l], w_out[l], ln1_g[l], ln1_b[l],
                          w_gate[l], w_up[l], w_down[l], ln2_g[l], ln2_b[l]) for l in range(depth)]
    rwt = router_w.T
    rb = router_b[:, None]

    def run_trunk(x, ckv_c, krope_c, s_c, conv_c, cbk_c, cbv_c):
        new = ([], [], [], [], [], [])
        for l in range(depth):
            x, st = _layer(x, layers[l], rwt, rb, ckv_c[l], krope_c[l], s_c[l], conv_c[l], cbk_c[l], cbv_c[l])
            for lst, a in zip(new, st):
                lst.append(a)
        return (x, *[jnp.stack(a) for a in new])

    bp = x_prompt.shape[0]
    empty = lambda *s: jnp.zeros((depth, bp) + s, F32)
    outs_p = run_trunk(x_prompt, empty(0, MLA_KV_RANK), empty(0, MLA_ROPE), empty(GDN_HEADS, GDN_DK, GDN_DV),
                       empty(GDN_CONV - 1, GDN_CONV_DIM), empty(0, CB_HEADS, CB_DH), empty(0, CB_HEADS, CB_DH))
    outs_s = run_trunk(x_sample, cache_mla_ckv, cache_mla_krope, state_gdn, state_gdn_conv, cache_cb_k, cache_cb_v)
    return (outs_p[0], outs_s[0], *outs_p[1:], *outs_s[1:])
```

```python
import functools
import math

import jax
import jax.numpy as jnp
import numpy as np
from jax import lax
from jax.experimental import pallas as pl
from jax.experimental.pallas import tpu as pltpu

F32 = jnp.float32
BF16 = jnp.bfloat16

CHUNK = 64
MLA_HEADS = 6
MLA_Q_RANK = 512
MLA_KV_RANK = 256
MLA_NOPE = 128
MLA_ROPE = 64
MLA_V = 128
ROPE_THETA = 10000.0
GDN_HEADS = 6
GDN_DK = 128
GDN_DV = 128
GDN_CONV = 4
GDN_KEY_DIM = GDN_HEADS * GDN_DK
GDN_VAL_DIM = GDN_HEADS * GDN_DV
GDN_CONV_DIM = 2 * GDN_KEY_DIM + GDN_VAL_DIM
CB_HEADS = 4
CB_DH = 128
CB_DIM = CB_HEADS * CB_DH
CB_PAST_ROWS = 8 * CHUNK
REL_CLIP = 256
N_EXPERTS = 16
N_GROUPS = 4
EXPERTS_PER_GROUP = N_EXPERTS // N_GROUPS
DEPTH = 2
DEEPNORM_ALPHA = (2 * DEPTH) ** 0.25

LANES = 128
C_GQKV = 0
C_GZ = C_GQKV + GDN_CONV_DIM
C_CQ = C_GZ + GDN_VAL_DIM
C_CKV = C_CQ + MLA_Q_RANK
C_KR = C_CKV + MLA_KV_RANK
C_GBA = C_KR + 2 * MLA_ROPE
C_CB = C_GBA + LANES
IN_PAD = C_CB + 3 * CB_DIM

SLAB_ROWS = 2048 // LANES
EXT_SEL = 0
EXT_W_LO = N_EXPERTS

VMEM_LIMIT = 56 * 1024 * 1024


def _cparams(sem):
    return pltpu.CompilerParams(dimension_semantics=sem, vmem_limit_bytes=VMEM_LIMIT)


def _dot(a, b):
    return jnp.dot(a, b, preferred_element_type=F32)


def _dot_nt(a, b):
    return lax.dot_general(a, b, (((1,), (1,)), ((), ())), preferred_element_type=F32)


def _dot_hi(a, b):
    return jnp.dot(a, b, preferred_element_type=F32, precision=lax.Precision.HIGHEST)


def _dot_nt_hi(a, b):
    return lax.dot_general(a, b, (((1,), (1,)), ((), ())), preferred_element_type=F32,
                           precision=lax.Precision.HIGHEST)


def _sigmoid(x):
    return 1.0 / (1.0 + jnp.exp(-x))


def _silu(x):
    return x * _sigmoid(x)


def _layer_norm(h, g, b, eps=1e-5):
    mu = jnp.mean(h, -1, keepdims=True)
    d = h - mu
    var = jnp.mean(d * d, -1, keepdims=True)
    return d * lax.rsqrt(var + eps) * g + b


def _rms_norm(x, g, eps=1e-6):
    return x * lax.rsqrt(jnp.mean(x * x, -1, keepdims=True) + eps) * g


def _inproj_kernel(x_ref, w_ref, o_ref, xb_ref):
    @pl.when(pl.program_id(1) == 0)
    def _():
        xb_ref[...] = x_ref[...].astype(BF16)

    o_ref[...] = _dot(xb_ref[...], w_ref[...])


def _in_proj(x2d, w):
    n, d = x2d.shape
    width = w.shape[1]
    tm = min(1024, n)
    tn = 512
    return pl.pallas_call(
        _inproj_kernel,
        grid=(n // tm, width // tn),
        in_specs=[pl.BlockSpec((tm, d), lambda i, j: (i, 0)),
                  pl.BlockSpec((d, tn), lambda i, j: (0, j))],
        out_specs=pl.BlockSpec((tm, tn), lambda i, j: (i, j)),
        out_shape=jax.ShapeDtypeStruct((n, width), F32),
        scratch_shapes=[pltpu.VMEM((tm, d), BF16)],
        compiler_params=_cparams(("parallel", "arbitrary")),
        name="in_proj",
    )(x2d, w)


def _mla_prep_kernel(cq_ref, ckv_ref, kr_ref, cosq_ref, sinq_ref, cosk_ref, sink_ref,
                     qg_ref, kvg_ref, wqn_ref, wqr_ref, wqs_ref, wuk_ref,
                     q_ref, ckvn_ref, krn_ref):
    cqn = _rms_norm(cq_ref[0], qg_ref[...]).astype(BF16)
    q_nope = _dot(cqn, wqn_ref[...]).astype(BF16)
    q_rope = _dot(cqn, wqr_ref[...])
    q_rope_sw = _dot(cqn, wqs_ref[...])
    q_rot = (q_rope * cosq_ref[...] + q_rope_sw * sinq_ref[...]).astype(BF16)
    for h in range(MLA_HEADS):
        q_lat = _dot(q_nope[:, h * MLA_NOPE:(h + 1) * MLA_NOPE], wuk_ref[h])
        q_ref[0, h, :, 0:MLA_KV_RANK] = q_lat.astype(BF16)
        q_ref[0, h, :, MLA_KV_RANK:] = q_rot[:, h * MLA_ROPE:(h + 1) * MLA_ROPE]
    ckvn_ref[0] = _rms_norm(ckv_ref[0], kvg_ref[...])
    kr = kr_ref[0]
    krn_ref[0] = kr[:, :MLA_ROPE] * cosk_ref[...] + kr[:, MLA_ROPE:] * sink_ref[...]


def _mla_prep(proj, cos2, sin2, q_norm_g, kv_norm_g, wq_nope, wq_rope, wq_rope_sw, wuk_t):
    b, t, _ = proj.shape
    tm = min(512, t)
    cosq = jnp.tile(cos2, (1, MLA_HEADS))
    sinq = jnp.tile(sin2, (1, MLA_HEADS))
    full = lambda a: pl.BlockSpec(a.shape, lambda bi, i: (0,) * a.ndim)
    row = lambda w: pl.BlockSpec((tm, w), lambda bi, i: (i, 0))
    qd = MLA_KV_RANK + MLA_ROPE
    return pl.pallas_call(
        _mla_prep_kernel,
        grid=(b, t // tm),
        in_specs=[pl.BlockSpec((1, tm, MLA_Q_RANK), lambda bi, i: (bi, i, C_CQ // MLA_Q_RANK)),
                  pl.BlockSpec((1, tm, MLA_KV_RANK), lambda bi, i: (bi, i, C_CKV // MLA_KV_RANK)),
                  pl.BlockSpec((1, tm, 2 * MLA_ROPE), lambda bi, i: (bi, i, C_KR // (2 * MLA_ROPE))),
                  row(MLA_HEADS * MLA_ROPE), row(MLA_HEADS * MLA_ROPE), row(MLA_ROPE), row(MLA_ROPE),
                  full(q_norm_g), full(kv_norm_g), full(wq_nope), full(wq_rope), full(wq_rope_sw),
                  full(wuk_t)],
        out_specs=[pl.BlockSpec((1, MLA_HEADS, tm, qd), lambda bi, i: (bi, 0, i, 0)),
                   pl.BlockSpec((1, tm, MLA_KV_RANK), lambda bi, i: (bi, i, 0)),
                   pl.BlockSpec((1, tm, MLA_ROPE), lambda bi, i: (bi, i, 0))],
        out_shape=[jax.ShapeDtypeStruct((b, MLA_HEADS, t, qd), BF16),
                   jax.ShapeDtypeStruct((b, t, MLA_KV_RANK), F32),
                   jax.ShapeDtypeStruct((b, t, MLA_ROPE), F32)],
        compiler_params=_cparams(("parallel", "parallel")),
        name="mla_prep",
    )(proj, proj, proj, cosq, sinq, cos2, sin2, q_norm_g, kv_norm_g, wq_nope, wq_rope, wq_rope_sw, wuk_t)


def _mla_attn_kernel(qi_ref, kj_ref, q_ref, ckv_ref, kr_ref, wuv_ref, o_ref, m_ref, l_ref, acc_ref, *, start, tq, tk):
    step = pl.program_id(1)
    i = qi_ref[step]
    j = kj_ref[step]

    @pl.when(j == 0)
    def _():
        m_ref[...] = jnp.full(m_ref.shape, -jnp.inf, F32)
        l_ref[...] = jnp.zeros(l_ref.shape, F32)
        acc_ref[...] = jnp.zeros(acc_ref.shape, F32)

    ckv = ckv_ref[0].astype(BF16)
    kr = kr_ref[0].astype(BF16)
    scale = (MLA_NOPE + MLA_ROPE) ** -0.5

    hpg = 1 if tq >= 256 else MLA_HEADS
    grows = hpg * tq

    def update(masked):
        if masked:
            q_pos = start + i * tq + lax.broadcasted_iota(jnp.int32, (grows, 1), 0) % tq
            k_pos = j * tk + lax.broadcasted_iota(jnp.int32, (1, tk), 1)
            visible = k_pos // CHUNK <= q_pos // CHUNK
        def scores(g):
            q = q_ref[0, g] if hpg == 1 else q_ref[0].reshape(grows, MLA_KV_RANK + MLA_ROPE)
            return _dot_nt(q[:, :MLA_KV_RANK], ckv) + _dot_nt(q[:, MLA_KV_RANK:], kr)

        groups = MLA_HEADS // hpg
        s_next = scores(0)
        for g in range(groups):
            s, s_next = s_next, (scores(g + 1) if g + 1 < groups else None)
            s = s * scale
            if masked:
                s = jnp.where(visible, s, -jnp.inf)
            rows = slice(g * grows, (g + 1) * grows)
            m_old = m_ref[rows]
            m_new = jnp.maximum(m_old, jnp.max(s, -1, keepdims=True))
            alpha = jnp.exp(m_old - m_new)
            p = jnp.exp(s - m_new)
            l_ref[rows] = alpha * l_ref[rows] + jnp.sum(p, -1, keepdims=True)
            acc_ref[rows] = alpha * acc_ref[rows] + _dot(p.astype(BF16), ckv)
            m_ref[rows] = m_new

    needs_mask = (j * tk + tk - 1) // CHUNK > (start + i * tq) // CHUNK
    pl.when(needs_mask)(lambda: update(True))
    pl.when(jnp.logical_not(needs_mask))(lambda: update(False))

    @pl.when(j == _last_kv_block(start, i, tq, tk))
    def _():
        o_lat = (acc_ref[...] / l_ref[...]).astype(BF16)
        for h in range(MLA_HEADS):
            o_ref[0, :, h * MLA_V:(h + 1) * MLA_V] = _dot(o_lat[h * tq:(h + 1) * tq], wuv_ref[h]).astype(BF16)


def _mla_attn_cached_kernel(q_ref, pckv_ref, pkr_ref, nckv_ref, nkr_ref, wuv_ref, o_ref, m_ref, l_ref, acc_ref, *, nb):
    j = pl.program_id(1)
    tq = nckv_ref.shape[1]
    rows = MLA_HEADS * tq

    @pl.when(j == 0)
    def _():
        m_ref[...] = jnp.full(m_ref.shape, -jnp.inf, F32)
        l_ref[...] = jnp.zeros(l_ref.shape, F32)
        acc_ref[...] = jnp.zeros(acc_ref.shape, F32)

    def update(ckv_f32, kr_f32):
        ckv = ckv_f32.astype(BF16)
        kr = kr_f32.astype(BF16)
        q = q_ref[0].reshape(rows, MLA_KV_RANK + MLA_ROPE)
        s = (_dot_nt(q[:, :MLA_KV_RANK], ckv) + _dot_nt(q[:, MLA_KV_RANK:], kr)) * ((MLA_NOPE + MLA_ROPE) ** -0.5)
        m_old = m_ref[...]
        m_new = jnp.maximum(m_old, jnp.max(s, -1, keepdims=True))
        alpha = jnp.exp(m_old - m_new)
        p = jnp.exp(s - m_new)
        l_ref[...] = alpha * l_ref[...] + jnp.sum(p, -1, keepdims=True)
        acc_ref[...] = alpha * acc_ref[...] + _dot(p.astype(BF16), ckv)
        m_ref[...] = m_new

    pl.when(j < nb)(lambda: update(pckv_ref[0], pkr_ref[0]))

    @pl.when(j == nb)
    def _():
        update(nckv_ref[0], nkr_ref[0])
        o_lat = (acc_ref[...] / l_ref[...]).astype(BF16)
        for h in range(MLA_HEADS):
            o_ref[0, :, h * MLA_V:(h + 1) * MLA_V] = _dot(o_lat[h * tq:(h + 1) * tq], wuv_ref[h]).astype(BF16)


def _mla_attn_cached(q, ckv_cache, kr_cache, layer, ckv_new, kr_new, wuv):
    b, _, t, qd = q.shape
    past = ckv_cache.shape[1]
    assert t == CHUNK and past % CHUNK == 0
    tk = _pick_tk(past)
    nb = past // tk
    past_map = lambda bi, j: (layer * b + bi, jnp.minimum(j, nb - 1), 0)
    new_map = lambda bi, j: (bi, 0, 0)
    return pl.pallas_call(
        functools.partial(_mla_attn_cached_kernel, nb=nb),
        grid=(b, nb + 1),
        in_specs=[pl.BlockSpec((1, MLA_HEADS, t, qd), lambda bi, j: (bi, 0, 0, 0)),
                  pl.BlockSpec((1, tk, MLA_KV_RANK), past_map),
                  pl.BlockSpec((1, tk, MLA_ROPE), past_map),
                  pl.BlockSpec((1, t, MLA_KV_RANK), new_map),
                  pl.BlockSpec((1, t, MLA_ROPE), new_map),
                  pl.BlockSpec(wuv.shape, lambda bi, j: (0, 0, 0))],
        out_specs=pl.BlockSpec((1, t, MLA_HEADS * MLA_V), lambda bi, j: (bi, 0, 0)),
        out_shape=jax.ShapeDtypeStruct((b, t, MLA_HEADS * MLA_V), BF16),
        scratch_shapes=[pltpu.VMEM((MLA_HEADS * t, 1), F32),
                        pltpu.VMEM((MLA_HEADS * t, 1), F32),
                        pltpu.VMEM((MLA_HEADS * t, MLA_KV_RANK), F32)],
        compiler_params=_cparams(("parallel", "arbitrary")),
        name="mla_attn_cached",
    )(q, ckv_cache, kr_cache, ckv_new, kr_new, wuv)


def _last_kv_block(start, i, tq, tk):
    return ((start + i * tq + tq - 1) // CHUNK * CHUNK) // tk


def _pick_tk(s):
    for cand in (512, 1024, 832, 768, 640, 576, 448, 384, 320, 256, 192, 128, 64):
        if s % cand == 0:
            return cand
    raise ValueError(f"unsupported key length {s}")


def _mla_attn(q, ckv_all, kr_all, wuv, start):
    b, _, t, qd = q.shape
    s = ckv_all.shape[1]
    tq = min(256, t)
    tk = _pick_tk(s)
    assert tk % CHUNK == 0 and s % tk == 0 and t % tq == 0
    pairs = [(i, j) for i in range(t // tq) for j in range(_last_kv_block(start, i, tq, tk) + 1)]
    qi = jnp.asarray(np.array([p[0] for p in pairs], np.int32))
    kj = jnp.asarray(np.array([p[1] for p in pairs], np.int32))
    kv_map = lambda bi, st, qi_r, kj_r: (bi, kj_r[st], 0)
    return pl.pallas_call(
        functools.partial(_mla_attn_kernel, start=start, tq=tq, tk=tk),
        grid_spec=pltpu.PrefetchScalarGridSpec(
            num_scalar_prefetch=2, grid=(b, len(pairs)),
            in_specs=[pl.BlockSpec((1, MLA_HEADS, tq, qd), lambda bi, st, qi_r, kj_r: (bi, 0, qi_r[st], 0)),
                      pl.BlockSpec((1, tk, MLA_KV_RANK), kv_map),
                      pl.BlockSpec((1, tk, MLA_ROPE), kv_map),
                      pl.BlockSpec(wuv.shape, lambda bi, st, qi_r, kj_r: (0, 0, 0))],
            out_specs=pl.BlockSpec((1, tq, MLA_HEADS * MLA_V), lambda bi, st, qi_r, kj_r: (bi, qi_r[st], 0)),
            scratch_shapes=[pltpu.VMEM((MLA_HEADS * tq, 1), F32),
                            pltpu.VMEM((MLA_HEADS * tq, 1), F32),
                            pltpu.VMEM((MLA_HEADS * tq, MLA_KV_RANK), F32)]),
        out_shape=jax.ShapeDtypeStruct((b, t, MLA_HEADS * MLA_V), BF16),
        compiler_params=_cparams(("parallel", "arbitrary")),
        name="mla_attn",
    )(qi, kj, q, ckv_all, kr_all, wuv)


def _split_bf16(x):
    hi = x.astype(BF16)
    return hi, (x - hi.astype(F32)).astype(BF16)


def _unit_lower_solve_many(a_list, rhs_list):
    n = rhs_list[0].shape[1]
    levels = int(math.log2(CHUNK))
    xs, ps = list(rhs_list), list(a_list)
    for lvl in range(levels):
        for h in range(len(xs)):
            p_hi, p_lo = _split_bf16(ps[h])
            lhs = jnp.concatenate([p_hi, p_hi, p_lo], 1)
            if lvl < levels - 1:
                r_hi, r_lo = _split_bf16(jnp.concatenate([xs[h], ps[h]], 1))
                both = _dot(lhs, jnp.concatenate([r_hi, r_lo, r_hi], 0))
                px, ps[h] = both[:, :n], both[:, n:]
            else:
                r_hi, r_lo = _split_bf16(xs[h])
                px = _dot(lhs, jnp.concatenate([r_hi, r_lo, r_hi], 0))
            xs[h] = xs[h] - px if lvl == 0 else xs[h] + px
    return xs


def _gdn_kernel(qkv_ref, gz_ref, gba_ref, convp_ref, convw_ref, alog_ref, dtb_ref, gnorm_ref, s0_ref,
                ob_ref, snew_ref, convn_ref, s_scr, ext_scr, *, n_chunks):
    n = pl.program_id(1)
    L = CHUNK
    tail = 8

    @pl.when(n == 0)
    def _():
        s_scr[...] = s0_ref[0]
        ext_scr[0:tail, :] = jnp.zeros((tail, GDN_CONV_DIM), F32)
        ext_scr[tail - (GDN_CONV - 1):tail, :] = convp_ref[0]

    cur = qkv_ref[0]
    ext_scr[tail:tail + L, :] = cur
    w = convw_ref[...]
    conv = ext_scr[tail - 3:tail - 3 + L, :] * w[0:1]
    conv = conv + ext_scr[tail - 2:tail - 2 + L, :] * w[1:2]
    conv = conv + ext_scr[tail - 1:tail - 1 + L, :] * w[2:3]
    conv = conv + cur * w[3:4]
    conv = _silu(conv)

    @pl.when(n == n_chunks - 1)
    def _():
        convn_ref[0] = ext_scr[tail + L - (GDN_CONV - 1):tail + L, :]

    ext_scr[0:tail, :] = cur[L - tail:, :]

    gba = gba_ref[0]
    beta_all = _sigmoid(gba)
    z = gba + dtb_ref[...]
    softplus = jnp.maximum(z, 0.0) + jnp.log1p(jnp.exp(-jnp.abs(z)))
    g_all = -jnp.exp(alog_ref[...]) * softplus
    ri = lax.broadcasted_iota(jnp.int32, (L, L), 0)
    ci = lax.broadcasted_iota(jnp.int32, (L, L), 1)
    incl = ci <= ri
    strict = ci < ri
    g_cum = _dot_hi(incl.astype(F32), g_all)
    g_cum_t = g_cum.T

    def l2n(x):
        return x * lax.rsqrt(jnp.sum(x * x, -1, keepdims=True) + 1e-6)

    heads = range(GDN_HEADS)
    q = [l2n(conv[:, h * GDN_DK:(h + 1) * GDN_DK]) * (GDN_DK ** -0.5) for h in heads]
    k = [l2n(conv[:, GDN_KEY_DIM + h * GDN_DK:GDN_KEY_DIM + (h + 1) * GDN_DK]) for h in heads]
    v = [conv[:, 2 * GDN_KEY_DIM + h * GDN_DV:2 * GDN_KEY_DIM + (h + 1) * GDN_DV] for h in heads]
    beta = [beta_all[:, h:h + 1] for h in heads]
    gc = [g_cum[:, GDN_HEADS + h:GDN_HEADS + h + 1] for h in heads]
    dmat = [jnp.exp(jnp.where(incl, gc[h] - g_cum_t[GDN_HEADS + h:GDN_HEADS + h + 1, :], -jnp.inf)) for h in heads]
    eg = [jnp.exp(gc[h]) for h in heads]
    qk_kk = []
    for h in heads:
        kb = k[h].astype(BF16)
        qk_kk.append(_dot_nt(jnp.concatenate([q[h].astype(BF16), kb], 0), kb))
    a_mat = [jnp.where(strict, beta[h] * qk_kk[h][L:] * dmat[h], 0.0) for h in heads]
    rhs = [jnp.concatenate([beta[h] * v[h], (beta[h] * eg[h]) * k[h]], -1) for h in heads]
    sol = _unit_lower_solve_many(a_mat, rhs)
    s_old = [s_scr[h] for h in heads]
    wq_s = [_dot(jnp.concatenate([sol[h][:, GDN_DV:], q[h]], 0).astype(BF16), s_old[h].astype(BF16)) for h in heads]
    upd = []
    for h in heads:
        delta = sol[h][:, :GDN_DV] - wq_s[h][:L]
        kd = k[h] * jnp.exp(gc[h][L - 1:L, :] - gc[h])
        lhs = jnp.concatenate([qk_kk[h][:L] * dmat[h], kd.T], 0)
        upd.append(_dot(lhs.astype(BF16), delta.astype(BF16)))
    for h in heads:
        o = eg[h] * wq_s[h][L:] + upd[h][:L]
        s_scr[h] = jnp.exp(gc[h][L - 1:L, :]) * s_old[h] + upd[h][L:]
        gz = gz_ref[0, :, h * GDN_DV:(h + 1) * GDN_DV]
        ob_ref[0, :, h * GDN_DV:(h + 1) * GDN_DV] = (_rms_norm(o, gnorm_ref[...]) * _silu(gz)).astype(BF16)

    @pl.when(n == n_chunks - 1)
    def _():
        snew_ref[0] = s_scr[...]


def _gdn(proj, conv_past, conv_w, alog128, dtb128, gnorm, s0):
    b, t, _ = proj.shape
    L = CHUNK
    full = lambda a: pl.BlockSpec(a.shape, lambda bi, n: (0,) * a.ndim)
    return pl.pallas_call(
        functools.partial(_gdn_kernel, n_chunks=t // L),
        grid=(b, t // L),
        in_specs=[pl.BlockSpec((1, L, GDN_CONV_DIM), lambda bi, n: (bi, n, C_GQKV // GDN_CONV_DIM)),
                  pl.BlockSpec((1, L, GDN_VAL_DIM), lambda bi, n: (bi, n, C_GZ // GDN_VAL_DIM)),
                  pl.BlockSpec((1, L, LANES), lambda bi, n: (bi, n, C_GBA // LANES)),
                  pl.BlockSpec((1, GDN_CONV - 1, GDN_CONV_DIM), lambda bi, n: (bi, 0, 0)),
                  full(conv_w), full(alog128), full(dtb128), full(gnorm),
                  pl.BlockSpec((1, GDN_HEADS, GDN_DK, GDN_DV), lambda bi, n: (bi, 0, 0, 0))],
        out_specs=[pl.BlockSpec((1, L, GDN_VAL_DIM), lambda bi, n: (bi, n, 0)),
                   pl.BlockSpec((1, GDN_HEADS, GDN_DK, GDN_DV), lambda bi, n: (bi, 0, 0, 0)),
                   pl.BlockSpec((1, GDN_CONV - 1, GDN_CONV_DIM), lambda bi, n: (bi, 0, 0))],
        out_shape=[jax.ShapeDtypeStruct((b, t, GDN_VAL_DIM), BF16),
                   jax.ShapeDtypeStruct((b, GDN_HEADS, GDN_DK, GDN_DV), F32),
                   jax.ShapeDtypeStruct((b, GDN_CONV - 1, GDN_CONV_DIM), F32)],
        scratch_shapes=[pltpu.VMEM((GDN_HEADS, GDN_DK, GDN_DV), F32),
                        pltpu.VMEM((8 + L, GDN_CONV_DIM), F32)],
        compiler_params=_cparams(("parallel", "arbitrary")),
        name="gdn",
    )(proj, proj, proj, conv_past, conv_w, alog128, dtb128, gnorm, s0)


def _cb_attn_kernel(q_ref, kprev_ref, kcur_ref, vprev_ref, vcur_ref, bias_ref, o_ref, *, tq, pad):
    i = pl.program_id(1)
    L = CHUNK
    width = CB_PAST_ROWS + L
    scale = CB_DH ** -0.5
    for c in range(tq // L):
        lo = c * L
        kwin = jnp.concatenate([kprev_ref[0, lo:, :], kcur_ref[0, :lo + L, :]], 0).astype(BF16)
        vwin = jnp.concatenate([vprev_ref[0, lo:, :], vcur_ref[0, :lo + L, :]], 0).astype(BF16)
        q = q_ref[0, lo:lo + L, :].astype(BF16)
        row = i * tq + lo + lax.broadcasted_iota(jnp.int32, (1, width), 1)
        valid = row >= pad
        for h in range(CB_HEADS):
            hs = slice(h * CB_DH, (h + 1) * CB_DH)
            s = _dot_nt(q[:, hs], kwin[:, hs]) * scale + bias_ref[h]
            s = jnp.where(valid, s, -jnp.inf)
            m = jnp.max(s, -1, keepdims=True)
            p = jnp.exp(s - m)
            p = (p / jnp.sum(p, -1, keepdims=True)).astype(BF16)
            o_ref[0, lo:lo + L, hs] = _dot(p, vwin[:, hs]).astype(BF16)


def _cb_attn(proj, k_cache, v_cache, layer, bias):
    b, t, _ = proj.shape
    tq = min(CB_PAST_ROWS, t)
    kcol, vcol = C_CB // CB_DIM + 1, C_CB // CB_DIM + 2
    cur = lambda col: pl.BlockSpec((1, tq, CB_DIM), lambda bi, i: (bi, i, col))
    if k_cache is None:
        assert tq == CB_PAST_ROWS and t % tq == 0
        pad = CB_PAST_ROWS
        prev = lambda col: pl.BlockSpec((1, tq, CB_DIM), lambda bi, i: (bi, jnp.maximum(i - 1, 0), col))
        k_prev_arr, v_prev_arr, k_prev, v_prev = proj, proj, prev(kcol), prev(vcol)
    else:
        assert t == tq and k_cache.shape[1] == CB_PAST_ROWS
        pad = 0
        cache_spec = pl.BlockSpec((1, CB_PAST_ROWS, CB_DIM), lambda bi, i: (layer * b + bi, 0, 0))
        k_prev_arr, v_prev_arr, k_prev, v_prev = k_cache, v_cache, cache_spec, cache_spec
    return pl.pallas_call(
        functools.partial(_cb_attn_kernel, tq=tq, pad=pad),
        grid=(b, t // tq),
        in_specs=[cur(C_CB // CB_DIM), k_prev, cur(kcol), v_prev, cur(vcol),
                  pl.BlockSpec(bias.shape, lambda bi, i: (0, 0, 0))],
        out_specs=pl.BlockSpec((1, tq, CB_DIM), lambda bi, i: (bi, i, 0)),
        out_shape=jax.ShapeDtypeStruct((b, t, CB_DIM), BF16),
        compiler_params=_cparams(("parallel", "parallel")),
        name="cb_attn",
    )(proj, k_prev_arr, proj, v_prev_arr, proj, bias)


def _route(logits_t, rb):
    s = _sigmoid(logits_t)
    sb = s + rb
    rows = [sb[e:e + 1, :] for e in range(N_EXPERTS)]
    grp = []
    for g in range(N_GROUPS):
        r = rows[g * EXPERTS_PER_GROUP:(g + 1) * EXPERTS_PER_GROUP]
        best = None
        for a in range(EXPERTS_PER_GROUP):
            for c in range(a + 1, EXPERTS_PER_GROUP):
                pair = r[a] + r[c]
                best = pair if best is None else jnp.maximum(best, pair)
        grp.append(best)
    gmax = functools.reduce(jnp.maximum, grp)
    gsel = jnp.full(gmax.shape, N_GROUPS, jnp.int32)
    for g in reversed(range(N_GROUPS)):
        gsel = jnp.where(grp[g] == gmax, g, gsel)
    sel = []
    for e in range(N_EXPERTS):
        g = e // EXPERTS_PER_GROUP
        rank = jnp.zeros(gmax.shape, jnp.int32)
        for e2 in range(g * EXPERTS_PER_GROUP, (g + 1) * EXPERTS_PER_GROUP):
            if e2 == e:
                continue
            ahead = (rows[e2] >= rows[e]) if e2 < e else (rows[e2] > rows[e])
            rank = rank + ahead.astype(jnp.int32)
        sel.append(jnp.where((gsel == g) & (rank < 2), 1.0, 0.0))
    ssum = functools.reduce(lambda a, c: a + c, [sel[e] * s[e:e + 1, :] for e in range(N_EXPERTS)])
    zero = jnp.zeros(gmax.shape, F32)
    seen, w_lo, w_hi, e_lo, e_hi = zero, zero, zero, zero, zero
    for e in range(N_EXPERTS):
        gate_e = sel[e] * s[e:e + 1, :] / ssum
        first = sel[e] * jnp.where(seen == 0.0, 1.0, 0.0)
        second = sel[e] - first
        w_lo, w_hi = w_lo + first * gate_e, w_hi + second * gate_e
        e_lo, e_hi = e_lo + first * e, e_hi + second * e
        seen = seen + sel[e]
    return sel, w_lo, w_hi, e_lo, e_hi


def _to_slab(ref, val):
    tm = val.shape[0]
    for c in range(val.shape[1] // LANES):
        ref[pl.ds(c, tm, stride=SLAB_ROWS), :] = val[:, c * LANES:(c + 1) * LANES]


def _out_ln1_kernel(oa_ref, ob_ref, oc_ref, x_ref, wa_ref, wb_ref, wc_ref, g_ref, b_ref, rwt_ref, rb_ref,
                    x1_ref, ext_ref):
    y = _dot(oa_ref[...], wa_ref[...]) + _dot(ob_ref[...], wb_ref[...]) + _dot(oc_ref[...], wc_ref[...])
    x1 = _layer_norm(DEEPNORM_ALPHA * x_ref[...] + y, g_ref[...], b_ref[...])
    _to_slab(x1_ref, x1)
    logits_t = _dot_nt_hi(rwt_ref[...], x1)
    sel, w_lo, w_hi, e_lo, e_hi = _route(logits_t, rb_ref[...])
    tm = x1.shape[0]
    rows = sel + [w_lo, w_hi, e_lo, e_hi]
    ext = jnp.concatenate(rows + [jnp.zeros((LANES - len(rows), tm), F32)], 0)
    ext_ref[...] = ext.T


def _out_ln1(oa, ob, oc, x2d, wa, wb, wc, g, bb, rwt, rb):
    n, d = x2d.shape
    tm = min(256, n)
    full = lambda a: pl.BlockSpec(a.shape, lambda i: (0,) * a.ndim)
    row = lambda w: pl.BlockSpec((tm, w), lambda i: (i, 0))
    return pl.pallas_call(
        _out_ln1_kernel,
        grid=(n // tm,),
        in_specs=[row(oa.shape[1]), row(ob.shape[1]), row(oc.shape[1]), row(d),
                  full(wa), full(wb), full(wc), full(g), full(bb), full(rwt), full(rb)],
        out_specs=[pl.BlockSpec((tm * SLAB_ROWS, LANES), lambda i: (i, 0)), row(LANES)],
        out_shape=[jax.ShapeDtypeStruct((n * SLAB_ROWS, LANES), F32), jax.ShapeDtypeStruct((n, LANES), F32)],
        compiler_params=_cparams(("parallel",)),
        name="out_ln1",
    )(oa, ob, oc, x2d, wa, wb, wc, g, bb, rwt, rb)


def _row_copy(src_hbm, src_row, dst, dst_row, sem):
    return pltpu.make_async_copy(src_hbm.at[pl.ds(pl.multiple_of(src_row * SLAB_ROWS, SLAB_ROWS), SLAB_ROWS)],
                                 dst.at[pl.ds(pl.multiple_of(dst_row * SLAB_ROWS, SLAB_ROWS), SLAB_ROWS)], sem)


def _moe_gather_kernel(src_ref, x_hbm, o_ref, sem, *, rows):
    base = pl.program_id(0) * rows

    def issue(r, c):
        _row_copy(x_hbm, src_ref[base + r], o_ref, r, sem).start()
        return c

    lax.fori_loop(0, rows, issue, 0, unroll=8)
    pltpu.make_async_copy(x_hbm.at[pl.ds(0, rows * SLAB_ROWS)], o_ref, sem).wait()


def _moe_gather(src, x_slab, n_rows, rows):
    return pl.pallas_call(
        functools.partial(_moe_gather_kernel, rows=rows),
        grid_spec=pltpu.PrefetchScalarGridSpec(
            num_scalar_prefetch=1, grid=(n_rows // rows,),
            in_specs=[pl.BlockSpec(memory_space=pl.ANY)],
            out_specs=pl.BlockSpec((rows * SLAB_ROWS, LANES), lambda i, s: (i, 0)),
            scratch_shapes=[pltpu.SemaphoreType.DMA(())]),
        out_shape=jax.ShapeDtypeStruct((n_rows * SLAB_ROWS, LANES), F32),
        compiler_params=_cparams(("arbitrary",)),
        name="moe_gather",
    )(src, x_slab)


def _moe_ffn_kernel(tile_e_ref, nact_ref, xs_ref, wg_ref, wu_ref, wd_ref, ys_ref, xb_ref, acc_ref):
    i = pl.program_id(0)
    f = pl.program_id(1)
    tm = xb_ref.shape[0]

    @pl.when(i < nact_ref[0])
    def _():
        @pl.when(f == 0)
        def _():
            for c in range(xb_ref.shape[1] // LANES):
                xb_ref[:, c * LANES:(c + 1) * LANES] = xs_ref[pl.ds(c, tm, stride=SLAB_ROWS), :].astype(BF16)

        xb = xb_ref[...]
        h = (_silu(_dot(xb, wg_ref[0])) * _dot(xb, wu_ref[0])).astype(BF16)
        y = _dot(h, wd_ref[0])

        @pl.when(f == 0)
        def _():
            acc_ref[...] = y

        @pl.when(f > 0)
        def _():
            acc_ref[...] += y

        @pl.when(f == pl.num_programs(1) - 1)
        def _():
            _to_slab(ys_ref, acc_ref[...])

    @pl.when((i >= nact_ref[0]) & (f == pl.num_programs(1) - 1))
    def _():
        ys_ref[...] = jnp.zeros(ys_ref.shape, F32)


def _moe_ffn(tile_e, nact, xs, wg, wu, wd, tm, layer):
    n_rows = xs.shape[0] // SLAB_ROWS
    _, d, ff = wg.shape
    tf = min(512, ff)
    nf = ff // tf
    row = lambda i, f, te, na: (jnp.maximum(jnp.minimum(i, na[0] - 1), 0), 0)
    fe = lambda i, f, na: jnp.where(i < na[0], f, nf - 1)
    ex = lambda i, te: layer * N_EXPERTS + te[i]
    return pl.pallas_call(
        _moe_ffn_kernel,
        grid_spec=pltpu.PrefetchScalarGridSpec(
            num_scalar_prefetch=2, grid=(n_rows // tm, nf),
            in_specs=[pl.BlockSpec((tm * SLAB_ROWS, LANES), row),
                      pl.BlockSpec((1, d, tf), lambda i, f, te, na: (ex(i, te), 0, fe(i, f, na))),
                      pl.BlockSpec((1, d, tf), lambda i, f, te, na: (ex(i, te), 0, fe(i, f, na))),
                      pl.BlockSpec((1, tf, d), lambda i, f, te, na: (ex(i, te), fe(i, f, na), 0))],
            out_specs=pl.BlockSpec((tm * SLAB_ROWS, LANES), lambda i, f, te, na: (i, 0)),
            scratch_shapes=[pltpu.VMEM((tm, d), BF16), pltpu.VMEM((tm, d), F32)]),
        out_shape=jax.ShapeDtypeStruct(xs.shape, F32),
        compiler_params=_cparams(("arbitrary", "arbitrary")),
        name="moe_ffn",
    )(tile_e, nact, xs, wg, wu, wd)


def _moe_combine_kernel(pos0_ref, pos1_ref, x1_ref, ext_ref, ys_hbm, g_ref, b_ref, o_ref,
                        buf0_ref, buf1_ref, h_ref, sem):
    tm = o_ref.shape[0]
    base = pl.program_id(0) * tm

    def issue(r, c):
        _row_copy(ys_hbm, pos0_ref[base + r], buf0_ref, r, sem.at[0]).start()
        _row_copy(ys_hbm, pos1_ref[base + r], buf1_ref, r, sem.at[1]).start()
        return c

    lax.fori_loop(0, tm, issue, 0, unroll=8)
    pltpu.make_async_copy(ys_hbm.at[pl.ds(0, tm * SLAB_ROWS)], buf0_ref, sem.at[0]).wait()
    pltpu.make_async_copy(ys_hbm.at[pl.ds(0, tm * SLAB_ROWS)], buf1_ref, sem.at[1]).wait()

    ext = ext_ref[...]
    w_lo = ext[:, EXT_W_LO:EXT_W_LO + 1]
    w_hi = ext[:, EXT_W_LO + 1:EXT_W_LO + 2]
    for c in range(o_ref.shape[1] // LANES):
        rows = pl.ds(c, tm, stride=SLAB_ROWS)
        y = w_lo * buf0_ref[rows, :] + w_hi * buf1_ref[rows, :]
        h_ref[:, c * LANES:(c + 1) * LANES] = DEEPNORM_ALPHA * x1_ref[rows, :] + y
    o_ref[...] = _layer_norm(h_ref[...], g_ref[...], b_ref[...])


def _moe_combine(pos0, pos1, x1_slab, ext, ys, g, bb):
    n = ext.shape[0]
    d = g.shape[1]
    tm = min(256, n)
    return pl.pallas_call(
        _moe_combine_kernel,
        grid_spec=pltpu.PrefetchScalarGridSpec(
            num_scalar_prefetch=2, grid=(n // tm,),
            in_specs=[pl.BlockSpec((tm * SLAB_ROWS, LANES), lambda i, p0, p1: (i, 0)),
                      pl.BlockSpec((tm, LANES), lambda i, p0, p1: (i, 0)),
                      pl.BlockSpec(memory_space=pl.ANY),
                      pl.BlockSpec(g.shape, lambda i, p0, p1: (0, 0)),
                      pl.BlockSpec(bb.shape, lambda i, p0, p1: (0, 0))],
            out_specs=pl.BlockSpec((tm, d), lambda i, p0, p1: (i, 0)),
            scratch_shapes=[pltpu.VMEM((tm * SLAB_ROWS, LANES), F32), pltpu.VMEM((tm * SLAB_ROWS, LANES), F32),
                            pltpu.VMEM((tm, d), F32), pltpu.SemaphoreType.DMA((2,))]),
        out_shape=jax.ShapeDtypeStruct((n, d), F32),
        compiler_params=_cparams(("arbitrary",)),
        name="moe_combine",
    )(pos0, pos1, x1_slab, ext, ys, g, bb)


def _moe_src_kernel(pos0_ref, pos1_ref, src_ref, *, n, n_rows):
    def clear(p, c):
        src_ref[p] = 0
        return c

    lax.fori_loop(0, n_rows, clear, 0, unroll=8)

    def put(t, c):
        src_ref[pos0_ref[t]] = t
        src_ref[pos1_ref[t]] = t
        return c

    lax.fori_loop(0, n, put, 0, unroll=8)


def _moe_src(pos0, pos1, n_rows):
    n = pos0.shape[0]
    return pl.pallas_call(
        functools.partial(_moe_src_kernel, n=n, n_rows=n_rows),
        grid_spec=pltpu.PrefetchScalarGridSpec(
            num_scalar_prefetch=2, grid=(1,), in_specs=[],
            out_specs=pl.BlockSpec(memory_space=pltpu.SMEM)),
        out_shape=jax.ShapeDtypeStruct((n_rows,), jnp.int32),
        name="moe_src",
    )(pos0, pos1)


def _moe_ln2(x1_slab, ext, wg, wu, wd, layer, g, bb):
    n = ext.shape[0]
    tm = 512 if n >= 8192 else 256
    n_rows = 2 * n + N_EXPERTS * tm
    sel = (ext[:, EXT_SEL:EXT_SEL + N_EXPERTS] > 0.5).astype(jnp.int32)
    csum = jnp.cumsum(sel, axis=0)
    padded = (csum[-1] + tm - 1) // tm * tm
    seg_end = jnp.cumsum(padded)
    slot = (seg_end - padded)[None, :] + csum - sel
    experts = jnp.arange(N_EXPERTS, dtype=jnp.int32)[None, :]
    e_lo = ext[:, EXT_W_LO + 2].astype(jnp.int32)[:, None]
    e_hi = ext[:, EXT_W_LO + 3].astype(jnp.int32)[:, None]
    pos0 = jnp.sum(jnp.where(experts == e_lo, slot, 0), axis=1)
    pos1 = jnp.sum(jnp.where(experts == e_hi, slot, 0), axis=1)
    nact = seg_end[-1:] // tm
    tiles = jnp.arange(n_rows // tm, dtype=jnp.int32)
    first_row = jnp.minimum(tiles, nact[0] - 1) * tm
    tile_e = jnp.minimum(jnp.sum((seg_end[None, :] <= first_row[:, None]).astype(jnp.int32), axis=1), N_EXPERTS - 1)

    src = _moe_src(pos0, pos1, n_rows)
    xs = _moe_gather(src, x1_slab, n_rows, tm)
    ys = _moe_ffn(tile_e, nact, xs, wg, wu, wd, tm, layer)
    return _moe_combine(pos0, pos1, x1_slab, ext, ys, g, bb)


def _prep_layer(w_in, q_norm_g, w_uq, kv_norm_g, w_uk, w_uv, conv_w, a_log, dt_bias, gdn_norm_g, rel_bias, w_out,
                ln1_g, ln1_b, ln2_g, ln2_b):
    splits = np.cumsum([MLA_Q_RANK, MLA_KV_RANK, MLA_ROPE, GDN_KEY_DIM, GDN_KEY_DIM, GDN_VAL_DIM, GDN_VAL_DIM,
                        GDN_HEADS, GDN_HEADS])
    cq, ckv, kr, gq, gk, gv, gz, gb, ga, cqkv = jnp.split(w_in, splits, axis=1)
    half = MLA_ROPE // 2
    kr_sw = jnp.concatenate([kr[:, half:], kr[:, :half]], 1)
    gba = jnp.pad(jnp.concatenate([gb, ga], 1), ((0, 0), (0, LANES - 2 * GDN_HEADS)))
    w_in_p = jnp.concatenate([gq, gk, gv, gz, cq, ckv, kr, kr_sw, gba, cqkv], 1).astype(BF16)
    r = w_uq.shape[0]
    wq_nope = w_uq[:, :, :MLA_NOPE].reshape(r, MLA_HEADS * MLA_NOPE).astype(BF16)
    wq_r = w_uq[:, :, MLA_NOPE:]
    wq_rope = wq_r.reshape(r, MLA_HEADS * MLA_ROPE).astype(BF16)
    wq_rope_sw = jnp.concatenate([wq_r[..., half:], wq_r[..., :half]], -1).reshape(r, MLA_HEADS * MLA_ROPE).astype(BF16)
    wuk_t = jnp.transpose(w_uk, (1, 2, 0)).astype(BF16)
    wuv = jnp.transpose(w_uv, (1, 0, 2)).astype(BF16)
    lane_pad = lambda a: jnp.pad(a, (GDN_HEADS, LANES - 2 * GDN_HEADS))[None, :]
    rel = CB_PAST_ROWS + np.arange(CHUNK)[:, None] - np.arange(CB_PAST_ROWS + CHUNK)[None, :]
    bias = rel_bias[:, np.clip(rel, -REL_CLIP, REL_CLIP) + REL_CLIP].astype(F32)
    w_out_b = w_out.astype(BF16)
    na = MLA_HEADS * MLA_V
    return dict(
        w_in=w_in_p, q_norm_g=q_norm_g[None, :], kv_norm_g=kv_norm_g[None, :],
        wq_nope=wq_nope, wq_rope=wq_rope, wq_rope_sw=wq_rope_sw, wuk_t=wuk_t, wuv=wuv,
        conv_w=conv_w, alog=lane_pad(a_log), dtb=lane_pad(dt_bias), gnorm=gdn_norm_g[None, :], bias=bias,
        wo_a=w_out_b[:na], wo_b=w_out_b[na:na + GDN_VAL_DIM], wo_c=w_out_b[na + GDN_VAL_DIM:],
        ln1_g=ln1_g[None, :], ln1_b=ln1_b[None, :], ln2_g=ln2_g[None, :], ln2_b=ln2_b[None, :])


def _rope_tables(start, t):
    pos = start + jnp.arange(t, dtype=jnp.int32)
    inv = ROPE_THETA ** (-jnp.arange(0, MLA_ROPE, 2, dtype=F32) / MLA_ROPE)
    ang = pos.astype(F32)[:, None] * inv[None, :]
    cos, sin = jnp.cos(ang), jnp.sin(ang)
    return jnp.concatenate([cos, cos], -1), jnp.concatenate([-sin, sin], -1)


def _layer(x, p, layer, shared, caches):
    b, t, d = x.shape
    x2d = x.reshape(b * t, d)
    proj = _in_proj(x2d, p["w_in"]).reshape(b, t, IN_PAD)

    start = 0 if caches is None else caches["ckv"].shape[1]
    cos2, sin2 = _rope_tables(start, t)
    q, ckv_new, krope_new = _mla_prep(proj, cos2, sin2, p["q_norm_g"], p["kv_norm_g"], p["wq_nope"], p["wq_rope"],
                                      p["wq_rope_sw"], p["wuk_t"])
    if caches is None:
        o_a = _mla_attn(q, ckv_new, krope_new, p["wuv"], 0)
        s_past = jnp.zeros((b, GDN_HEADS, GDN_DK, GDN_DV), F32)
        conv_past = jnp.zeros((b, GDN_CONV - 1, GDN_CONV_DIM), F32)
        o_c = _cb_attn(proj, None, None, layer, p["bias"])
    else:
        o_a = _mla_attn_cached(q, caches["ckv"], caches["krope"], layer, ckv_new, krope_new, p["wuv"])
        s_past, conv_past = caches["gdn"][layer], caches["conv"][layer]
        o_c = _cb_attn(proj, caches["cb_k"], caches["cb_v"], layer, p["bias"])
    o_b, s_new, conv_new = _gdn(proj, conv_past, p["conv_w"], p["alog"], p["dtb"], p["gnorm"], s_past)

    x1, ext = _out_ln1(o_a.reshape(b * t, -1), o_b.reshape(b * t, -1), o_c.reshape(b * t, -1), x2d,
                       p["wo_a"], p["wo_b"], p["wo_c"], p["ln1_g"], p["ln1_b"], shared["rwt"], shared["rb"])
    x2 = _moe_ln2(x1, ext, shared["w_gate"], shared["w_up"], shared["w_down"], layer, p["ln2_g"], p["ln2_b"])

    keep = min(CB_PAST_ROWS, t)
    cb_new = lambda col: proj[:, t - keep:, col:col + CB_DIM].reshape(b, keep, CB_HEADS, CB_DH)
    state = (ckv_new, krope_new, s_new, conv_new, cb_new(C_CB + CB_DIM), cb_new(C_CB + 2 * CB_DIM))
    return x2.reshape(b, t, d), state


def kernel(x_prompt, x_sample, cache_mla_ckv, cache_mla_krope, state_gdn, state_gdn_conv, cache_cb_k, cache_cb_v,
           w_in, q_norm_g, w_uq, kv_norm_g, w_uk, w_uv, conv_w, a_log, dt_bias, gdn_norm_g, rel_bias, w_out,
           ln1_g, ln1_b, router_w, router_b, w_gate, w_up, w_down, ln2_g, ln2_b):
    depth = w_in.shape[0]
    layers = [_prep_layer(w_in[l], q_norm_g[l], w_uq[l], kv_norm_g[l], w_uk[l], w_uv[l], conv_w[l], a_log[l],
                          dt_bias[l], gdn_norm_g[l], rel_bias[l], w_out[l], ln1_g[l], ln1_b[l],
                          ln2_g[l], ln2_b[l]) for l in range(depth)]
    stack = lambda w: w.astype(BF16).reshape((depth * N_EXPERTS,) + w.shape[2:])
    shared = dict(rwt=router_w.T, rb=router_b[:, None], w_gate=stack(w_gate), w_up=stack(w_up), w_down=stack(w_down))
    merge = lambda a: a.reshape((a.shape[0] * a.shape[1],) + a.shape[2:])
    rows = cache_cb_k.shape[2]
    caches = dict(ckv=merge(cache_mla_ckv), krope=merge(cache_mla_krope), gdn=state_gdn, conv=state_gdn_conv,
                  cb_k=merge(cache_cb_k).reshape(-1, rows, CB_DIM), cb_v=merge(cache_cb_v).reshape(-1, rows, CB_DIM))

    def run_trunk(x, trunk_caches):
        new = ([], [], [], [], [], [])
        for l in range(depth):
            x, st = _layer(x, layers[l], l, shared, trunk_caches)
            for lst, a in zip(new, st):
                lst.append(a)
        return (x, *[jnp.stack(a) for a in new])

    outs_p = run_trunk(x_prompt, None)
    outs_s = run_trunk(x_sample, caches)
    return (outs_p[0], outs_s[0], *outs_p[1:], *outs_s[1:])
```

```python
import functools
import math

import jax
import jax.numpy as jnp
import numpy as np
from jax import lax
from jax.experimental import pallas as pl
from jax.experimental.pallas import tpu as pltpu

F32 = jnp.float32
BF16 = jnp.bfloat16

CHUNK = 64
MLA_HEADS = 6
MLA_Q_RANK = 512
MLA_KV_RANK = 256
MLA_NOPE = 128
MLA_ROPE = 64
MLA_V = 128
ROPE_THETA = 10000.0
GDN_HEADS = 6
GDN_DK = 128
GDN_DV = 128
GDN_CONV = 4
GDN_KEY_DIM = GDN_HEADS * GDN_DK
GDN_VAL_DIM = GDN_HEADS * GDN_DV
GDN_CONV_DIM = 2 * GDN_KEY_DIM + GDN_VAL_DIM
CB_HEADS = 4
CB_DH = 128
CB_DIM = CB_HEADS * CB_DH
CB_PAST_ROWS = 8 * CHUNK
REL_CLIP = 256
N_EXPERTS = 16
N_GROUPS = 4
EXPERTS_PER_GROUP = N_EXPERTS // N_GROUPS
DEPTH = 2
DEEPNORM_ALPHA = (2 * DEPTH) ** 0.25

LANES = 128
C_GQKV = 0
C_GZ = C_GQKV + GDN_CONV_DIM
C_CQ = C_GZ + GDN_VAL_DIM
C_CKV = C_CQ + MLA_Q_RANK
C_KR = C_CKV + MLA_KV_RANK
C_GBA = C_KR + 2 * MLA_ROPE
C_CB = C_GBA + LANES
IN_PAD = C_CB + 3 * CB_DIM

SLAB_ROWS = 2048 // LANES
EXT_SEL = 0
EXT_W_LO = N_EXPERTS

VMEM_LIMIT = 56 * 1024 * 1024


def _cparams(sem):
    return pltpu.CompilerParams(dimension_semantics=sem, vmem_limit_bytes=VMEM_LIMIT)


def _dot(a, b):
    return jnp.dot(a, b, preferred_element_type=F32)


def _dot_nt(a, b):
    return lax.dot_general(a, b, (((1,), (1,)), ((), ())), preferred_element_type=F32)


def _dot_hi(a, b):
    return jnp.dot(a, b, preferred_element_type=F32, precision=lax.Precision.HIGHEST)


def _dot_nt_hi(a, b):
    return lax.dot_general(a, b, (((1,), (1,)), ((), ())), preferred_element_type=F32,
                           precision=lax.Precision.HIGHEST)


def _sigmoid(x):
    return 1.0 / (1.0 + jnp.exp(-x))


def _silu(x):
    return x * _sigmoid(x)


def _layer_norm(h, g, b, eps=1e-5):
    mu = jnp.mean(h, -1, keepdims=True)
    d = h - mu
    var = jnp.mean(d * d, -1, keepdims=True)
    return d * lax.rsqrt(var + eps) * g + b


def _rms_norm(x, g, eps=1e-6):
    return x * lax.rsqrt(jnp.mean(x * x, -1, keepdims=True) + eps) * g


def _inproj_kernel(x_ref, w_ref, o_ref, xb_ref):
    @pl.when(pl.program_id(1) == 0)
    def _():
        xb_ref[...] = x_ref[...].astype(BF16)

    o_ref[...] = _dot(xb_ref[...], w_ref[0])


def _in_proj(x2d, w, layer):
    n, d = x2d.shape
    width = w.shape[2]
    tm = min(1024, n)
    tn = 512
    return pl.pallas_call(
        _inproj_kernel,
        grid=(n // tm, width // tn),
        in_specs=[pl.BlockSpec((tm, d), lambda i, j: (i, 0)),
                  pl.BlockSpec((1, d, tn), lambda i, j: (layer, 0, j))],
        out_specs=pl.BlockSpec((tm, tn), lambda i, j: (i, j)),
        out_shape=jax.ShapeDtypeStruct((n, width), F32),
        scratch_shapes=[pltpu.VMEM((tm, d), BF16)],
        compiler_params=_cparams(("parallel", "arbitrary")),
        name="in_proj",
    )(x2d, w)


def _mla_prep_kernel(cq_ref, ckv_ref, kr_ref, cosq_ref, sinq_ref, cosk_ref, sink_ref,
                     qg_ref, kvg_ref, wqn_ref, wqr_ref, wqs_ref, wuk_ref,
                     q_ref, ckvn_ref, krn_ref):
    cqn = _rms_norm(cq_ref[0], qg_ref[...]).astype(BF16)
    q_nope = _dot(cqn, wqn_ref[...]).astype(BF16)
    q_rope = _dot(cqn, wqr_ref[...])
    q_rope_sw = _dot(cqn, wqs_ref[...])
    q_rot = (q_rope * cosq_ref[...] + q_rope_sw * sinq_ref[...]).astype(BF16)
    for h in range(MLA_HEADS):
        q_lat = _dot(q_nope[:, h * MLA_NOPE:(h + 1) * MLA_NOPE], wuk_ref[h])
        q_ref[0, h, :, 0:MLA_KV_RANK] = q_lat.astype(BF16)
        q_ref[0, h, :, MLA_KV_RANK:] = q_rot[:, h * MLA_ROPE:(h + 1) * MLA_ROPE]
    ckvn_ref[0] = _rms_norm(ckv_ref[0], kvg_ref[...])
    kr = kr_ref[0]
    krn_ref[0] = kr[:, :MLA_ROPE] * cosk_ref[...] + kr[:, MLA_ROPE:] * sink_ref[...]


def _mla_prep(proj, cos2, sin2, q_norm_g, kv_norm_g, wq_nope, wq_rope, wq_rope_sw, wuk_t):
    b, t, _ = proj.shape
    tm = min(512, t)
    cosq = jnp.tile(cos2, (1, MLA_HEADS))
    sinq = jnp.tile(sin2, (1, MLA_HEADS))
    full = lambda a: pl.BlockSpec(a.shape, lambda bi, i: (0,) * a.ndim)
    row = lambda w: pl.BlockSpec((tm, w), lambda bi, i: (i, 0))
    qd = MLA_KV_RANK + MLA_ROPE
    return pl.pallas_call(
        _mla_prep_kernel,
        grid=(b, t // tm),
        in_specs=[pl.BlockSpec((1, tm, MLA_Q_RANK), lambda bi, i: (bi, i, C_CQ // MLA_Q_RANK)),
                  pl.BlockSpec((1, tm, MLA_KV_RANK), lambda bi, i: (bi, i, C_CKV // MLA_KV_RANK)),
                  pl.BlockSpec((1, tm, 2 * MLA_ROPE), lambda bi, i: (bi, i, C_KR // (2 * MLA_ROPE))),
                  row(MLA_HEADS * MLA_ROPE), row(MLA_HEADS * MLA_ROPE), row(MLA_ROPE), row(MLA_ROPE),
                  full(q_norm_g), full(kv_norm_g), full(wq_nope), full(wq_rope), full(wq_rope_sw),
                  full(wuk_t)],
        out_specs=[pl.BlockSpec((1, MLA_HEADS, tm, qd), lambda bi, i: (bi, 0, i, 0)),
                   pl.BlockSpec((1, tm, MLA_KV_RANK), lambda bi, i: (bi, i, 0)),
                   pl.BlockSpec((1, tm, MLA_ROPE), lambda bi, i: (bi, i, 0))],
        out_shape=[jax.ShapeDtypeStruct((b, MLA_HEADS, t, qd), BF16),
                   jax.ShapeDtypeStruct((b, t, MLA_KV_RANK), F32),
                   jax.ShapeDtypeStruct((b, t, MLA_ROPE), F32)],
        compiler_params=_cparams(("parallel", "parallel")),
        name="mla_prep",
    )(proj, proj, proj, cosq, sinq, cos2, sin2, q_norm_g, kv_norm_g, wq_nope, wq_rope, wq_rope_sw, wuk_t)


def _mla_attn_kernel(qi_ref, kj_ref, q_ref, ckv_ref, kr_ref, wuv_ref, o_ref, m_ref, l_ref, acc_ref, *, start, tq, tk):
    step = pl.program_id(1)
    i = qi_ref[step]
    j = kj_ref[step]

    @pl.when(j == 0)
    def _():
        m_ref[...] = jnp.full(m_ref.shape, -jnp.inf, F32)
        l_ref[...] = jnp.zeros(l_ref.shape, F32)
        acc_ref[...] = jnp.zeros(acc_ref.shape, F32)

    ckv = ckv_ref[0].astype(BF16)
    kr = kr_ref[0].astype(BF16)
    scale = (MLA_NOPE + MLA_ROPE) ** -0.5

    hpg = 1 if tq >= 256 else MLA_HEADS
    grows = hpg * tq

    def update(masked):
        if masked:
            q_pos = start + i * tq + lax.broadcasted_iota(jnp.int32, (grows, 1), 0) % tq
            k_pos = j * tk + lax.broadcasted_iota(jnp.int32, (1, tk), 1)
            visible = k_pos // CHUNK <= q_pos // CHUNK
        def scores(g):
            q = q_ref[0, g] if hpg == 1 else q_ref[0].reshape(grows, MLA_KV_RANK + MLA_ROPE)
            return _dot_nt(q[:, :MLA_KV_RANK], ckv) + _dot_nt(q[:, MLA_KV_RANK:], kr)

        groups = MLA_HEADS // hpg
        s_next = scores(0)
        for g in range(groups):
            s, s_next = s_next, (scores(g + 1) if g + 1 < groups else None)
            s = s * scale
            if masked:
                s = jnp.where(visible, s, -jnp.inf)
            rows = slice(g * grows, (g + 1) * grows)
            m_old = m_ref[rows]
            m_new = jnp.maximum(m_old, jnp.max(s, -1, keepdims=True))
            alpha = jnp.exp(m_old - m_new)
            p = jnp.exp(s - m_new)
            l_ref[rows] = alpha * l_ref[rows] + jnp.sum(p, -1, keepdims=True)
            acc_ref[rows] = alpha * acc_ref[rows] + _dot(p.astype(BF16), ckv)
            m_ref[rows] = m_new

    needs_mask = (j * tk + tk - 1) // CHUNK > (start + i * tq) // CHUNK
    pl.when(needs_mask)(lambda: update(True))
    pl.when(jnp.logical_not(needs_mask))(lambda: update(False))

    @pl.when(j == _last_kv_block(start, i, tq, tk))
    def _():
        o_lat = (acc_ref[...] / l_ref[...]).astype(BF16)
        for h in range(MLA_HEADS):
            o_ref[0, :, h * MLA_V:(h + 1) * MLA_V] = _dot(o_lat[h * tq:(h + 1) * tq], wuv_ref[h]).astype(BF16)


def _mla_attn_cached_kernel(q_ref, pckv_ref, pkr_ref, nckv_ref, nkr_ref, wuv_ref, o_ref, m_ref, l_ref, acc_ref, *, nb):
    j = pl.program_id(1)
    tq = nckv_ref.shape[1]
    rows = MLA_HEADS * tq

    @pl.when(j == 0)
    def _():
        m_ref[...] = jnp.full(m_ref.shape, -jnp.inf, F32)
        l_ref[...] = jnp.zeros(l_ref.shape, F32)
        acc_ref[...] = jnp.zeros(acc_ref.shape, F32)

    def update(ckv_f32, kr_f32):
        ckv = ckv_f32.astype(BF16)
        kr = kr_f32.astype(BF16)
        q = q_ref[0].reshape(rows, MLA_KV_RANK + MLA_ROPE)
        s = (_dot_nt(q[:, :MLA_KV_RANK], ckv) + _dot_nt(q[:, MLA_KV_RANK:], kr)) * ((MLA_NOPE + MLA_ROPE) ** -0.5)
        m_old = m_ref[...]
        m_new = jnp.maximum(m_old, jnp.max(s, -1, keepdims=True))
        alpha = jnp.exp(m_old - m_new)
        p = jnp.exp(s - m_new)
        l_ref[...] = alpha * l_ref[...] + jnp.sum(p, -1, keepdims=True)
        acc_ref[...] = alpha * acc_ref[...] + _dot(p.astype(BF16), ckv)
        m_ref[...] = m_new

    pl.when(j < nb)(lambda: update(pckv_ref[0], pkr_ref[0]))

    @pl.when(j == nb)
    def _():
        update(nckv_ref[0], nkr_ref[0])
        o_lat = (acc_ref[...] / l_ref[...]).astype(BF16)
        for h in range(MLA_HEADS):
            o_ref[0, :, h * MLA_V:(h + 1) * MLA_V] = _dot(o_lat[h * tq:(h + 1) * tq], wuv_ref[h]).astype(BF16)


def _mla_attn_cached(q, ckv_cache, kr_cache, layer, ckv_new, kr_new, wuv):
    b, _, t, qd = q.shape
    past = ckv_cache.shape[1]
    assert t == CHUNK and past % CHUNK == 0
    tk = _pick_tk(past)
    nb = past // tk
    past_map = lambda bi, j: (layer * b + bi, jnp.minimum(j, nb - 1), 0)
    new_map = lambda bi, j: (bi, 0, 0)
    return pl.pallas_call(
        functools.partial(_mla_attn_cached_kernel, nb=nb),
        grid=(b, nb + 1),
        in_specs=[pl.BlockSpec((1, MLA_HEADS, t, qd), lambda bi, j: (bi, 0, 0, 0)),
                  pl.BlockSpec((1, tk, MLA_KV_RANK), past_map),
                  pl.BlockSpec((1, tk, MLA_ROPE), past_map),
                  pl.BlockSpec((1, t, MLA_KV_RANK), new_map),
                  pl.BlockSpec((1, t, MLA_ROPE), new_map),
                  pl.BlockSpec(wuv.shape, lambda bi, j: (0, 0, 0))],
        out_specs=pl.BlockSpec((1, t, MLA_HEADS * MLA_V), lambda bi, j: (bi, 0, 0)),
        out_shape=jax.ShapeDtypeStruct((b, t, MLA_HEADS * MLA_V), BF16),
        scratch_shapes=[pltpu.VMEM((MLA_HEADS * t, 1), F32),
                        pltpu.VMEM((MLA_HEADS * t, 1), F32),
                        pltpu.VMEM((MLA_HEADS * t, MLA_KV_RANK), F32)],
        compiler_params=_cparams(("parallel", "arbitrary")),
        name="mla_attn_cached",
    )(q, ckv_cache, kr_cache, ckv_new, kr_new, wuv)


def _last_kv_block(start, i, tq, tk):
    return ((start + i * tq + tq - 1) // CHUNK * CHUNK) // tk


def _pick_tk(s):
    for cand in (512, 1024, 832, 768, 640, 576, 448, 384, 320, 256, 192, 128, 64):
        if s % cand == 0:
            return cand
    raise ValueError(f"unsupported key length {s}")


def _mla_attn(q, ckv_all, kr_all, wuv, start):
    b, _, t, qd = q.shape
    s = ckv_all.shape[1]
    tq = min(256, t)
    tk = _pick_tk(s)
    assert tk % CHUNK == 0 and s % tk == 0 and t % tq == 0
    pairs = [(i, j) for i in range(t // tq) for j in range(_last_kv_block(start, i, tq, tk) + 1)]
    qi = jnp.asarray(np.array([p[0] for p in pairs], np.int32))
    kj = jnp.asarray(np.array([p[1] for p in pairs], np.int32))
    kv_map = lambda bi, st, qi_r, kj_r: (bi, kj_r[st], 0)
    return pl.pallas_call(
        functools.partial(_mla_attn_kernel, start=start, tq=tq, tk=tk),
        grid_spec=pltpu.PrefetchScalarGridSpec(
            num_scalar_prefetch=2, grid=(b, len(pairs)),
            in_specs=[pl.BlockSpec((1, MLA_HEADS, tq, qd), lambda bi, st, qi_r, kj_r: (bi, 0, qi_r[st], 0)),
                      pl.BlockSpec((1, tk, MLA_KV_RANK), kv_map),
                      pl.BlockSpec((1, tk, MLA_ROPE), kv_map),
                      pl.BlockSpec(wuv.shape, lambda bi, st, qi_r, kj_r: (0, 0, 0))],
            out_specs=pl.BlockSpec((1, tq, MLA_HEADS * MLA_V), lambda bi, st, qi_r, kj_r: (bi, qi_r[st], 0)),
            scratch_shapes=[pltpu.VMEM((MLA_HEADS * tq, 1), F32),
                            pltpu.VMEM((MLA_HEADS * tq, 1), F32),
                            pltpu.VMEM((MLA_HEADS * tq, MLA_KV_RANK), F32)]),
        out_shape=jax.ShapeDtypeStruct((b, t, MLA_HEADS * MLA_V), BF16),
        compiler_params=_cparams(("parallel", "arbitrary")),
        name="mla_attn",
    )(qi, kj, q, ckv_all, kr_all, wuv)


def _split_bf16(x):
    hi = x.astype(BF16)
    return hi, (x - hi.astype(F32)).astype(BF16)


def _unit_lower_solve_many(a_list, rhs_list):
    n = rhs_list[0].shape[1]
    levels = int(math.log2(CHUNK))
    xs, ps = list(rhs_list), list(a_list)
    for lvl in range(levels):
        for h in range(len(xs)):
            p_hi, p_lo = _split_bf16(ps[h])
            lhs = jnp.concatenate([p_hi, p_hi, p_lo], 1)
            if lvl < levels - 1:
                r_hi, r_lo = _split_bf16(jnp.concatenate([xs[h], ps[h]], 1))
                both = _dot(lhs, jnp.concatenate([r_hi, r_lo, r_hi], 0))
                px, ps[h] = both[:, :n], both[:, n:]
            else:
                r_hi, r_lo = _split_bf16(xs[h])
                px = _dot(lhs, jnp.concatenate([r_hi, r_lo, r_hi], 0))
            xs[h] = xs[h] - px if lvl == 0 else xs[h] + px
    return xs


def _gdn_kernel(qkv_ref, gz_ref, gba_ref, convp_ref, convw_ref, alog_ref, dtb_ref, gnorm_ref, s0_ref,
                ob_ref, snew_ref, convn_ref, s_scr, ext_scr, *, n_chunks):
    n = pl.program_id(1)
    L = CHUNK
    tail = 8

    @pl.when(n == 0)
    def _():
        s_scr[...] = s0_ref[0]
        ext_scr[0:tail, :] = jnp.zeros((tail, GDN_CONV_DIM), F32)
        ext_scr[tail - (GDN_CONV - 1):tail, :] = convp_ref[0]

    cur = qkv_ref[0]
    ext_scr[tail:tail + L, :] = cur
    w = convw_ref[...]
    conv = ext_scr[tail - 3:tail - 3 + L, :] * w[0:1]
    conv = conv + ext_scr[tail - 2:tail - 2 + L, :] * w[1:2]
    conv = conv + ext_scr[tail - 1:tail - 1 + L, :] * w[2:3]
    conv = conv + cur * w[3:4]
    conv = _silu(conv)

    @pl.when(n == n_chunks - 1)
    def _():
        convn_ref[0] = ext_scr[tail + L - (GDN_CONV - 1):tail + L, :]

    ext_scr[0:tail, :] = cur[L - tail:, :]

    gba = gba_ref[0]
    beta_all = _sigmoid(gba)
    z = gba + dtb_ref[...]
    softplus = jnp.maximum(z, 0.0) + jnp.log1p(jnp.exp(-jnp.abs(z)))
    g_all = -jnp.exp(alog_ref[...]) * softplus
    ri = lax.broadcasted_iota(jnp.int32, (L, L), 0)
    ci = lax.broadcasted_iota(jnp.int32, (L, L), 1)
    incl = ci <= ri
    strict = ci < ri
    g_cum = _dot_hi(incl.astype(F32), g_all)
    g_cum_t = g_cum.T

    def l2n(x):
        return x * lax.rsqrt(jnp.sum(x * x, -1, keepdims=True) + 1e-6)

    heads = range(GDN_HEADS)
    q = [l2n(conv[:, h * GDN_DK:(h + 1) * GDN_DK]) * (GDN_DK ** -0.5) for h in heads]
    k = [l2n(conv[:, GDN_KEY_DIM + h * GDN_DK:GDN_KEY_DIM + (h + 1) * GDN_DK]) for h in heads]
    v = [conv[:, 2 * GDN_KEY_DIM + h * GDN_DV:2 * GDN_KEY_DIM + (h + 1) * GDN_DV] for h in heads]
    beta = [beta_all[:, h:h + 1] for h in heads]
    gc = [g_cum[:, GDN_HEADS + h:GDN_HEADS + h + 1] for h in heads]
    dmat = [jnp.exp(jnp.where(incl, gc[h] - g_cum_t[GDN_HEADS + h:GDN_HEADS + h + 1, :], -jnp.inf)) for h in heads]
    eg = [jnp.exp(gc[h]) for h in heads]
    qk_kk = []
    for h in heads:
        kb = k[h].astype(BF16)
        qk_kk.append(_dot_nt(jnp.concatenate([q[h].astype(BF16), kb], 0), kb))
    a_mat = [jnp.where(strict, beta[h] * qk_kk[h][L:] * dmat[h], 0.0) for h in heads]
    rhs = [jnp.concatenate([beta[h] * v[h], (beta[h] * eg[h]) * k[h]], -1) for h in heads]
    sol = _unit_lower_solve_many(a_mat, rhs)
    s_old = [s_scr[h] for h in heads]
    wq_s = [_dot(jnp.concatenate([sol[h][:, GDN_DV:], q[h]], 0).astype(BF16), s_old[h].astype(BF16)) for h in heads]
    upd = []
    for h in heads:
        delta = sol[h][:, :GDN_DV] - wq_s[h][:L]
        kd = k[h] * jnp.exp(gc[h][L - 1:L, :] - gc[h])
        lhs = jnp.concatenate([qk_kk[h][:L] * dmat[h], kd.T], 0)
        upd.append(_dot(lhs.astype(BF16), delta.astype(BF16)))
    for h in heads:
        o = eg[h] * wq_s[h][L:] + upd[h][:L]
        s_scr[h] = jnp.exp(gc[h][L - 1:L, :]) * s_old[h] + upd[h][L:]
        gz = gz_ref[0, :, h * GDN_DV:(h + 1) * GDN_DV]
        ob_ref[0, :, h * GDN_DV:(h + 1) * GDN_DV] = (_rms_norm(o, gnorm_ref[...]) * _silu(gz)).astype(BF16)

    @pl.when(n == n_chunks - 1)
    def _():
        snew_ref[0] = s_scr[...]


def _gdn(proj, conv_past, conv_w, alog128, dtb128, gnorm, s0):
    b, t, _ = proj.shape
    L = CHUNK
    full = lambda a: pl.BlockSpec(a.shape, lambda bi, n: (0,) * a.ndim)
    return pl.pallas_call(
        functools.partial(_gdn_kernel, n_chunks=t // L),
        grid=(b, t // L),
        in_specs=[pl.BlockSpec((1, L, GDN_CONV_DIM), lambda bi, n: (bi, n, C_GQKV // GDN_CONV_DIM)),
                  pl.BlockSpec((1, L, GDN_VAL_DIM), lambda bi, n: (bi, n, C_GZ // GDN_VAL_DIM)),
                  pl.BlockSpec((1, L, LANES), lambda bi, n: (bi, n, C_GBA // LANES)),
                  pl.BlockSpec((1, GDN_CONV - 1, GDN_CONV_DIM), lambda bi, n: (bi, 0, 0)),
                  full(conv_w), full(alog128), full(dtb128), full(gnorm),
                  pl.BlockSpec((1, GDN_HEADS, GDN_DK, GDN_DV), lambda bi, n: (bi, 0, 0, 0))],
        out_specs=[pl.BlockSpec((1, L, GDN_VAL_DIM), lambda bi, n: (bi, n, 0)),
                   pl.BlockSpec((1, GDN_HEADS, GDN_DK, GDN_DV), lambda bi, n: (bi, 0, 0, 0)),
                   pl.BlockSpec((1, GDN_CONV - 1, GDN_CONV_DIM), lambda bi, n: (bi, 0, 0))],
        out_shape=[jax.ShapeDtypeStruct((b, t, GDN_VAL_DIM), BF16),
                   jax.ShapeDtypeStruct((b, GDN_HEADS, GDN_DK, GDN_DV), F32),
                   jax.ShapeDtypeStruct((b, GDN_CONV - 1, GDN_CONV_DIM), F32)],
        scratch_shapes=[pltpu.VMEM((GDN_HEADS, GDN_DK, GDN_DV), F32),
                        pltpu.VMEM((8 + L, GDN_CONV_DIM), F32)],
        compiler_params=_cparams(("parallel", "arbitrary")),
        name="gdn",
    )(proj, proj, proj, conv_past, conv_w, alog128, dtb128, gnorm, s0)


def _cb_attn_kernel(q_ref, kprev_ref, kcur_ref, vprev_ref, vcur_ref, bias_ref, o_ref, *, tq, pad):
    i = pl.program_id(1)
    L = CHUNK
    width = CB_PAST_ROWS + L
    scale = CB_DH ** -0.5
    for c in range(tq // L):
        lo = c * L
        kwin = jnp.concatenate([kprev_ref[0, lo:, :], kcur_ref[0, :lo + L, :]], 0).astype(BF16)
        vwin = jnp.concatenate([vprev_ref[0, lo:, :], vcur_ref[0, :lo + L, :]], 0).astype(BF16)
        q = q_ref[0, lo:lo + L, :].astype(BF16)
        row = i * tq + lo + lax.broadcasted_iota(jnp.int32, (1, width), 1)
        valid = row >= pad
        for h in range(CB_HEADS):
            hs = slice(h * CB_DH, (h + 1) * CB_DH)
            s = _dot_nt(q[:, hs], kwin[:, hs]) * scale + bias_ref[h]
            s = jnp.where(valid, s, -jnp.inf)
            m = jnp.max(s, -1, keepdims=True)
            p = jnp.exp(s - m)
            p = (p / jnp.sum(p, -1, keepdims=True)).astype(BF16)
            o_ref[0, lo:lo + L, hs] = _dot(p, vwin[:, hs]).astype(BF16)


def _cb_attn(proj, k_cache, v_cache, layer, bias):
    b, t, _ = proj.shape
    tq = min(CB_PAST_ROWS, t)
    kcol, vcol = C_CB // CB_DIM + 1, C_CB // CB_DIM + 2
    cur = lambda col: pl.BlockSpec((1, tq, CB_DIM), lambda bi, i: (bi, i, col))
    if k_cache is None:
        assert tq == CB_PAST_ROWS and t % tq == 0
        pad = CB_PAST_ROWS
        prev = lambda col: pl.BlockSpec((1, tq, CB_DIM), lambda bi, i: (bi, jnp.maximum(i - 1, 0), col))
        k_prev_arr, v_prev_arr, k_prev, v_prev = proj, proj, prev(kcol), prev(vcol)
    else:
        assert t == tq and k_cache.shape[1] == CB_PAST_ROWS
        pad = 0
        cache_spec = pl.BlockSpec((1, CB_PAST_ROWS, CB_DIM), lambda bi, i: (layer * b + bi, 0, 0))
        k_prev_arr, v_prev_arr, k_prev, v_prev = k_cache, v_cache, cache_spec, cache_spec
    return pl.pallas_call(
        functools.partial(_cb_attn_kernel, tq=tq, pad=pad),
        grid=(b, t // tq),
        in_specs=[cur(C_CB // CB_DIM), k_prev, cur(kcol), v_prev, cur(vcol),
                  pl.BlockSpec(bias.shape, lambda bi, i: (0, 0, 0))],
        out_specs=pl.BlockSpec((1, tq, CB_DIM), lambda bi, i: (bi, i, 0)),
        out_shape=jax.ShapeDtypeStruct((b, t, CB_DIM), BF16),
        compiler_params=_cparams(("parallel", "parallel")),
        name="cb_attn",
    )(proj, k_prev_arr, proj, v_prev_arr, proj, bias)


def _route(logits_t, rb):
    s = _sigmoid(logits_t)
    sb = s + rb
    rows = [sb[e:e + 1, :] for e in range(N_EXPERTS)]
    grp = []
    for g in range(N_GROUPS):
        r = rows[g * EXPERTS_PER_GROUP:(g + 1) * EXPERTS_PER_GROUP]
        best = None
        for a in range(EXPERTS_PER_GROUP):
            for c in range(a + 1, EXPERTS_PER_GROUP):
                pair = r[a] + r[c]
                best = pair if best is None else jnp.maximum(best, pair)
        grp.append(best)
    gmax = functools.reduce(jnp.maximum, grp)
    gsel = jnp.full(gmax.shape, N_GROUPS, jnp.int32)
    for g in reversed(range(N_GROUPS)):
        gsel = jnp.where(grp[g] == gmax, g, gsel)
    sel = []
    for e in range(N_EXPERTS):
        g = e // EXPERTS_PER_GROUP
        rank = jnp.zeros(gmax.shape, jnp.int32)
        for e2 in range(g * EXPERTS_PER_GROUP, (g + 1) * EXPERTS_PER_GROUP):
            if e2 == e:
                continue
            ahead = (rows[e2] >= rows[e]) if e2 < e else (rows[e2] > rows[e])
            rank = rank + ahead.astype(jnp.int32)
        sel.append(jnp.where((gsel == g) & (rank < 2), 1.0, 0.0))
    ssum = functools.reduce(lambda a, c: a + c, [sel[e] * s[e:e + 1, :] for e in range(N_EXPERTS)])
    zero = jnp.zeros(gmax.shape, F32)
    seen, w_lo, w_hi, e_lo, e_hi = zero, zero, zero, zero, zero
    for e in range(N_EXPERTS):
        gate_e = sel[e] * s[e:e + 1, :] / ssum
        first = sel[e] * jnp.where(seen == 0.0, 1.0, 0.0)
        second = sel[e] - first
        w_lo, w_hi = w_lo + first * gate_e, w_hi + second * gate_e
        e_lo, e_hi = e_lo + first * e, e_hi + second * e
        seen = seen + sel[e]
    return sel, w_lo, w_hi, e_lo, e_hi


def _to_slab(ref, val, tok0=0):
    tm = val.shape[0]
    for c in range(val.shape[1] // LANES):
        ref[pl.ds(tok0 * SLAB_ROWS + c, tm, stride=SLAB_ROWS), :] = val[:, c * LANES:(c + 1) * LANES]


def _out_ln1_kernel(oa_ref, ob_ref, oc_ref, x_ref, wa_ref, wb_ref, wc_ref, g_ref, b_ref, rwt_ref, rb_ref,
                    x1_ref, ext_ref):
    tm = x_ref.shape[0]
    halves = [slice(k * (tm // 2), (k + 1) * (tm // 2)) for k in range(2)]
    ys = [_dot(oa_ref[r, :], wa_ref[...]) + _dot(ob_ref[r, :], wb_ref[...]) + _dot(oc_ref[r, :], wc_ref[...])
          for r in halves]
    for r, y in zip(halves, ys):
        x1 = _layer_norm(DEEPNORM_ALPHA * x_ref[r, :] + y, g_ref[...], b_ref[...])
        _to_slab(x1_ref, x1, r.start)
        logits_t = _dot_nt_hi(rwt_ref[...], x1)
        sel, w_lo, w_hi, e_lo, e_hi = _route(logits_t, rb_ref[...])
        rows = sel + [w_lo, w_hi, e_lo, e_hi]
        ext = jnp.concatenate(rows + [jnp.zeros((LANES - len(rows), x1.shape[0]), F32)], 0)
        ext_ref[r, :] = ext.T


def _out_ln1(oa, ob, oc, x2d, wa, wb, wc, g, bb, rwt, rb):
    n, d = x2d.shape
    tm = min(256, n)
    full = lambda a: pl.BlockSpec(a.shape, lambda i: (0,) * a.ndim)
    row = lambda w: pl.BlockSpec((tm, w), lambda i: (i, 0))
    return pl.pallas_call(
        _out_ln1_kernel,
        grid=(n // tm,),
        in_specs=[row(oa.shape[1]), row(ob.shape[1]), row(oc.shape[1]), row(d),
                  full(wa), full(wb), full(wc), full(g), full(bb), full(rwt), full(rb)],
        out_specs=[pl.BlockSpec((tm * SLAB_ROWS, LANES), lambda i: (i, 0)), row(LANES)],
        out_shape=[jax.ShapeDtypeStruct((n * SLAB_ROWS, LANES), F32), jax.ShapeDtypeStruct((n, LANES), F32)],
        compiler_params=_cparams(("parallel",)),
        name="out_ln1",
    )(oa, ob, oc, x2d, wa, wb, wc, g, bb, rwt, rb)


def _row_copy(src_hbm, src_row, dst, dst_row, sem):
    return pltpu.make_async_copy(src_hbm.at[pl.ds(pl.multiple_of(src_row * SLAB_ROWS, SLAB_ROWS), SLAB_ROWS)],
                                 dst.at[pl.ds(pl.multiple_of(dst_row * SLAB_ROWS, SLAB_ROWS), SLAB_ROWS)], sem)


def _moe_gather_kernel(src_ref, x_hbm, o_ref, sem, *, rows):
    base = pl.program_id(0) * rows

    def issue(r, c):
        _row_copy(x_hbm, src_ref[base + r], o_ref, r, sem).start()
        return c

    lax.fori_loop(0, rows, issue, 0, unroll=8)
    pltpu.make_async_copy(x_hbm.at[pl.ds(0, rows * SLAB_ROWS)], o_ref, sem).wait()


def _moe_gather(src, x_slab, n_rows, rows):
    return pl.pallas_call(
        functools.partial(_moe_gather_kernel, rows=rows),
        grid_spec=pltpu.PrefetchScalarGridSpec(
            num_scalar_prefetch=1, grid=(n_rows // rows,),
            in_specs=[pl.BlockSpec(memory_space=pl.ANY)],
            out_specs=pl.BlockSpec((rows * SLAB_ROWS, LANES), lambda i, s: (i, 0)),
            scratch_shapes=[pltpu.SemaphoreType.DMA(())]),
        out_shape=jax.ShapeDtypeStruct((n_rows * SLAB_ROWS, LANES), F32),
        compiler_params=_cparams(("arbitrary",)),
        name="moe_gather",
    )(src, x_slab)


def _moe_ffn_kernel(tile_e_ref, nact_ref, xs_ref, wg_ref, wu_ref, wd_ref, ys_ref, xb_ref, acc_ref):
    i = pl.program_id(0)
    f = pl.program_id(1)
    tm = xb_ref.shape[0]

    @pl.when(i < nact_ref[0])
    def _():
        @pl.when(f == 0)
        def _():
            for c in range(xb_ref.shape[1] // LANES):
                xb_ref[:, c * LANES:(c + 1) * LANES] = xs_ref[pl.ds(c, tm, stride=SLAB_ROWS), :].astype(BF16)

        xb = xb_ref[...]
        h = (_silu(_dot(xb, wg_ref[0])) * _dot(xb, wu_ref[0])).astype(BF16)
        y = _dot(h, wd_ref[0])

        @pl.when(f == 0)
        def _():
            acc_ref[...] = y

        @pl.when(f > 0)
        def _():
            acc_ref[...] += y

        @pl.when(f == pl.num_programs(1) - 1)
        def _():
            _to_slab(ys_ref, acc_ref[...])

    @pl.when((i >= nact_ref[0]) & (f == pl.num_programs(1) - 1))
    def _():
        ys_ref[...] = jnp.zeros(ys_ref.shape, F32)


def _moe_ffn(tile_e, nact, xs, wg, wu, wd, tm, layer):
    n_rows = xs.shape[0] // SLAB_ROWS
    _, d, ff = wg.shape
    tf = min(512, ff)
    nf = ff // tf
    row = lambda i, f, te, na: (jnp.maximum(jnp.minimum(i, na[0] - 1), 0), 0)
    fe = lambda i, f, na: jnp.where(i < na[0], f, nf - 1)
    ex = lambda i, te: layer * N_EXPERTS + te[i]
    return pl.pallas_call(
        _moe_ffn_kernel,
        grid_spec=pltpu.PrefetchScalarGridSpec(
            num_scalar_prefetch=2, grid=(n_rows // tm, nf),
            in_specs=[pl.BlockSpec((tm * SLAB_ROWS, LANES), row),
                      pl.BlockSpec((1, d, tf), lambda i, f, te, na: (ex(i, te), 0, fe(i, f, na))),
                      pl.BlockSpec((1, d, tf), lambda i, f, te, na: (ex(i, te), 0, fe(i, f, na))),
                      pl.BlockSpec((1, tf, d), lambda i, f, te, na: (ex(i, te), fe(i, f, na), 0))],
            out_specs=pl.BlockSpec((tm * SLAB_ROWS, LANES), lambda i, f, te, na: (i, 0)),
            scratch_shapes=[pltpu.VMEM((tm, d), BF16), pltpu.VMEM((tm, d), F32)]),
        out_shape=jax.ShapeDtypeStruct(xs.shape, F32),
        compiler_params=_cparams(("arbitrary", "arbitrary")),
        name="moe_ffn",
    )(tile_e, nact, xs, wg, wu, wd)


def _moe_combine_kernel(pos0_ref, pos1_ref, x1_ref, ext_ref, ys_hbm, g_ref, b_ref, o_ref,
                        buf0_ref, buf1_ref, h_ref, sem):
    tm = o_ref.shape[0]
    base = pl.program_id(0) * tm

    def issue(r, c):
        _row_copy(ys_hbm, pos0_ref[base + r], buf0_ref, r, sem.at[0]).start()
        _row_copy(ys_hbm, pos1_ref[base + r], buf1_ref, r, sem.at[1]).start()
        return c

    lax.fori_loop(0, tm, issue, 0, unroll=8)
    pltpu.make_async_copy(ys_hbm.at[pl.ds(0, tm * SLAB_ROWS)], buf0_ref, sem.at[0]).wait()
    pltpu.make_async_copy(ys_hbm.at[pl.ds(0, tm * SLAB_ROWS)], buf1_ref, sem.at[1]).wait()

    ext = ext_ref[...]
    w_lo = ext[:, EXT_W_LO:EXT_W_LO + 1]
    w_hi = ext[:, EXT_W_LO + 1:EXT_W_LO + 2]
    for c in range(o_ref.shape[1] // LANES):
        rows = pl.ds(c, tm, stride=SLAB_ROWS)
        y = w_lo * buf0_ref[rows, :] + w_hi * buf1_ref[rows, :]
        h_ref[:, c * LANES:(c + 1) * LANES] = DEEPNORM_ALPHA * x1_ref[rows, :] + y
    o_ref[...] = _layer_norm(h_ref[...], g_ref[...], b_ref[...])


def _moe_combine(pos0, pos1, x1_slab, ext, ys, g, bb):
    n = ext.shape[0]
    d = g.shape[1]
    tm = min(512, n)
    return pl.pallas_call(
        _moe_combine_kernel,
        grid_spec=pltpu.PrefetchScalarGridSpec(
            num_scalar_prefetch=2, grid=(n // tm,),
            in_specs=[pl.BlockSpec((tm * SLAB_ROWS, LANES), lambda i, p0, p1: (i, 0)),
                      pl.BlockSpec((tm, LANES), lambda i, p0, p1: (i, 0)),
                      pl.BlockSpec(memory_space=pl.ANY),
                      pl.BlockSpec(g.shape, lambda i, p0, p1: (0, 0)),
                      pl.BlockSpec(bb.shape, lambda i, p0, p1: (0, 0))],
            out_specs=pl.BlockSpec((tm, d), lambda i, p0, p1: (i, 0)),
            scratch_shapes=[pltpu.VMEM((tm * SLAB_ROWS, LANES), F32), pltpu.VMEM((tm * SLAB_ROWS, LANES), F32),
                            pltpu.VMEM((tm, d), F32), pltpu.SemaphoreType.DMA((2,))]),
        out_shape=jax.ShapeDtypeStruct((n, d), F32),
        compiler_params=_cparams(("arbitrary",)),
        name="moe_combine",
    )(pos0, pos1, x1_slab, ext, ys, g, bb)


def _moe_src_kernel(pos0_ref, pos1_ref, src_ref, *, n, n_rows):
    def clear(p, c):
        src_ref[p] = 0
        return c

    lax.fori_loop(0, n_rows, clear, 0, unroll=8)

    def put(t, c):
        src_ref[pos0_ref[t]] = t
        src_ref[pos1_ref[t]] = t
        return c

    lax.fori_loop(0, n, put, 0, unroll=8)


def _moe_src(pos0, pos1, n_rows):
    n = pos0.shape[0]
    return pl.pallas_call(
        functools.partial(_moe_src_kernel, n=n, n_rows=n_rows),
        grid_spec=pltpu.PrefetchScalarGridSpec(
            num_scalar_prefetch=2, grid=(1,), in_specs=[],
            out_specs=pl.BlockSpec(memory_space=pltpu.SMEM)),
        out_shape=jax.ShapeDtypeStruct((n_rows,), jnp.int32),
        name="moe_src",
    )(pos0, pos1)


def _moe_ln2(x1_slab, ext, wg, wu, wd, layer, g, bb):
    n = ext.shape[0]
    tm = 512 if n >= 8192 else 256
    n_rows = 2 * n + N_EXPERTS * tm
    sel = (ext[:, EXT_SEL:EXT_SEL + N_EXPERTS] > 0.5).astype(jnp.int32)
    csum = jnp.cumsum(sel, axis=0)
    padded = (csum[-1] + tm - 1) // tm * tm
    seg_end = jnp.cumsum(padded)
    slot = (seg_end - padded)[None, :] + csum - sel
    experts = jnp.arange(N_EXPERTS, dtype=jnp.int32)[None, :]
    e_lo = ext[:, EXT_W_LO + 2].astype(jnp.int32)[:, None]
    e_hi = ext[:, EXT_W_LO + 3].astype(jnp.int32)[:, None]
    pos0 = jnp.sum(jnp.where(experts == e_lo, slot, 0), axis=1)
    pos1 = jnp.sum(jnp.where(experts == e_hi, slot, 0), axis=1)
    nact = seg_end[-1:] // tm
    tiles = jnp.arange(n_rows // tm, dtype=jnp.int32)
    first_row = jnp.minimum(tiles, nact[0] - 1) * tm
    tile_e = jnp.minimum(jnp.sum((seg_end[None, :] <= first_row[:, None]).astype(jnp.int32), axis=1), N_EXPERTS - 1)

    src = _moe_src(pos0, pos1, n_rows)
    xs = _moe_gather(src, x1_slab, n_rows, 2 * tm)
    ys = _moe_ffn(tile_e, nact, xs, wg, wu, wd, tm, layer)
    return _moe_combine(pos0, pos1, x1_slab, ext, ys, g, bb)


_HALF_ROPE = MLA_ROPE // 2
_SRC_KR = MLA_Q_RANK + MLA_KV_RANK
_SRC_GQKV = _SRC_KR + MLA_ROPE
_SRC_GZ = _SRC_GQKV + GDN_CONV_DIM
_SRC_GBA = _SRC_GZ + GDN_VAL_DIM
_SRC_CB = _SRC_GBA + 2 * GDN_HEADS
W_IN_SEGMENTS = (
    (C_GQKV, _SRC_GQKV, GDN_CONV_DIM), (C_GZ, _SRC_GZ, GDN_VAL_DIM), (C_CQ, 0, MLA_Q_RANK),
    (C_CKV, MLA_Q_RANK, MLA_KV_RANK), (C_KR, _SRC_KR, MLA_ROPE),
    (C_KR + MLA_ROPE, _SRC_KR + _HALF_ROPE, _HALF_ROPE), (C_KR + MLA_ROPE + _HALF_ROPE, _SRC_KR, _HALF_ROPE),
    (C_GBA, _SRC_GBA, 2 * GDN_HEADS), (C_CB, _SRC_CB, 3 * CB_DIM))


def _w_in_relayout_kernel(w_ref, o_ref):
    o_ref[0, :, C_GBA:C_GBA + LANES] = jnp.zeros((o_ref.shape[1], LANES), BF16)
    for dst, src, width in W_IN_SEGMENTS:
        o_ref[0, :, dst:dst + width] = w_ref[0, :, src:src + width].astype(BF16)


def _w_in_relayout(w_in):
    depth, d, width = w_in.shape
    tm = 256
    return pl.pallas_call(
        _w_in_relayout_kernel,
        grid=(depth, d // tm),
        in_specs=[pl.BlockSpec((1, tm, width), lambda l, i: (l, i, 0))],
        out_specs=pl.BlockSpec((1, tm, IN_PAD), lambda l, i: (l, i, 0)),
        out_shape=jax.ShapeDtypeStruct((depth, d, IN_PAD), BF16),
        compiler_params=_cparams(("parallel", "parallel")),
        name="w_in_relayout",
    )(w_in)


def _prep_layer(q_norm_g, w_uq, kv_norm_g, w_uk, w_uv, conv_w, a_log, dt_bias, gdn_norm_g, rel_bias, w_out,
                ln1_g, ln1_b, ln2_g, ln2_b):
    half = MLA_ROPE // 2
    r = w_uq.shape[0]
    wq_nope = w_uq[:, :, :MLA_NOPE].reshape(r, MLA_HEADS * MLA_NOPE).astype(BF16)
    wq_r = w_uq[:, :, MLA_NOPE:]
    wq_rope = wq_r.reshape(r, MLA_HEADS * MLA_ROPE).astype(BF16)
    wq_rope_sw = jnp.concatenate([wq_r[..., half:], wq_r[..., :half]], -1).reshape(r, MLA_HEADS * MLA_ROPE).astype(BF16)
    wuk_t = jnp.transpose(w_uk, (1, 2, 0)).astype(BF16)
    wuv = jnp.transpose(w_uv, (1, 0, 2)).astype(BF16)
    lane_pad = lambda a: jnp.pad(a, (GDN_HEADS, LANES - 2 * GDN_HEADS))[None, :]
    bias = rel_bias
    w_out_b = w_out.astype(BF16)
    na = MLA_HEADS * MLA_V
    return dict(
        q_norm_g=q_norm_g[None, :], kv_norm_g=kv_norm_g[None, :],
        wq_nope=wq_nope, wq_rope=wq_rope, wq_rope_sw=wq_rope_sw, wuk_t=wuk_t, wuv=wuv,
        conv_w=conv_w, alog=lane_pad(a_log), dtb=lane_pad(dt_bias), gnorm=gdn_norm_g[None, :], bias=bias,
        wo_a=w_out_b[:na], wo_b=w_out_b[na:na + GDN_VAL_DIM], wo_c=w_out_b[na + GDN_VAL_DIM:],
        ln1_g=ln1_g[None, :], ln1_b=ln1_b[None, :], ln2_g=ln2_g[None, :], ln2_b=ln2_b[None, :])


def _rope_tables(start, t):
    pos = start + jnp.arange(t, dtype=jnp.int32)
    inv = ROPE_THETA ** (-jnp.arange(0, MLA_ROPE, 2, dtype=F32) / MLA_ROPE)
    ang = pos.astype(F32)[:, None] * inv[None, :]
    cos, sin = jnp.cos(ang), jnp.sin(ang)
    return jnp.concatenate([cos, cos], -1), jnp.concatenate([-sin, sin], -1)


def _layer(x, p, layer, shared, caches):
    b, t, d = x.shape
    x2d = x.reshape(b * t, d)
    proj = _in_proj(x2d, shared["w_in"], layer).reshape(b, t, IN_PAD)

    start = 0 if caches is None else caches["ckv"].shape[1]
    cos2, sin2 = _rope_tables(start, t)
    q, ckv_new, krope_new = _mla_prep(proj, cos2, sin2, p["q_norm_g"], p["kv_norm_g"], p["wq_nope"], p["wq_rope"],
                                      p["wq_rope_sw"], p["wuk_t"])
    if caches is None:
        o_a = _mla_attn(q, ckv_new, krope_new, p["wuv"], 0)
        s_past = jnp.zeros((b, GDN_HEADS, GDN_DK, GDN_DV), F32)
        conv_past = jnp.zeros((b, GDN_CONV - 1, GDN_CONV_DIM), F32)
        o_c = _cb_attn(proj, None, None, layer, p["bias"])
    else:
        o_a = _mla_attn_cached(q, caches["ckv"], caches["krope"], layer, ckv_new, krope_new, p["wuv"])
        s_past, conv_past = caches["gdn"][layer], caches["conv"][layer]
        o_c = _cb_attn(proj, caches["cb_k"], caches["cb_v"], layer, p["bias"])
    o_b, s_new, conv_new = _gdn(proj, conv_past, p["conv_w"], p["alog"], p["dtb"], p["gnorm"], s_past)

    x1, ext = _out_ln1(o_a.reshape(b * t, -1), o_b.reshape(b * t, -1), o_c.reshape(b * t, -1), x2d,
                       p["wo_a"], p["wo_b"], p["wo_c"], p["ln1_g"], p["ln1_b"], shared["rwt"], shared["rb"])
    x2 = _moe_ln2(x1, ext, shared["w_gate"], shared["w_up"], shared["w_down"], layer, p["ln2_g"], p["ln2_b"])

    keep = min(CB_PAST_ROWS, t)
    cb_new = lambda col: proj[:, t - keep:, col:col + CB_DIM].reshape(b, keep, CB_HEADS, CB_DH)
    state = (ckv_new, krope_new, s_new, conv_new, cb_new(C_CB + CB_DIM), cb_new(C_CB + 2 * CB_DIM))
    return x2.reshape(b, t, d), state


def kernel(x_prompt, x_sample, cache_mla_ckv, cache_mla_krope, state_gdn, state_gdn_conv, cache_cb_k, cache_cb_v,
           w_in, q_norm_g, w_uq, kv_norm_g, w_uk, w_uv, conv_w, a_log, dt_bias, gdn_norm_g, rel_bias, w_out,
           ln1_g, ln1_b, router_w, router_b, w_gate, w_up, w_down, ln2_g, ln2_b):
    depth = w_in.shape[0]
    rel = CB_PAST_ROWS + np.arange(CHUNK)[:, None] - np.arange(CB_PAST_ROWS + CHUNK)[None, :]
    band_bias = rel_bias[:, :, np.clip(rel, -REL_CLIP, REL_CLIP) + REL_CLIP].astype(F32)
    layers = [_prep_layer(q_norm_g[l], w_uq[l], kv_norm_g[l], w_uk[l], w_uv[l], conv_w[l], a_log[l],
                          dt_bias[l], gdn_norm_g[l], band_bias[l], w_out[l], ln1_g[l], ln1_b[l],
                          ln2_g[l], ln2_b[l]) for l in range(depth)]
    stack = lambda w: w.astype(BF16).reshape((depth * N_EXPERTS,) + w.shape[2:])
    shared = dict(w_in=_w_in_relayout(w_in), rwt=router_w.T, rb=router_b[:, None], w_gate=stack(w_gate), w_up=stack(w_up), w_down=stack(w_down))
    merge = lambda a: a.reshape((a.shape[0] * a.shape[1],) + a.shape[2:])
    rows = cache_cb_k.shape[2]
    caches = dict(ckv=merge(cache_mla_ckv), krope=merge(cache_mla_krope), gdn=state_gdn, conv=state_gdn_conv,
                  cb_k=merge(cache_cb_k).reshape(-1, rows, CB_DIM), cb_v=merge(cache_cb_v).reshape(-1, rows, CB_DIM))

    def run_trunk(x, trunk_caches):
        new = ([], [], [], [], [], [])
        for l in range(depth):
            x, st = _layer(x, layers[l], l, shared, trunk_caches)
            for lst, a in zip(new, st):
                lst.append(a)
        return (x, *[jnp.stack(a) for a in new])

    outs_p = run_trunk(x_prompt, None)
    outs_s = run_trunk(x_sample, caches)
    return (outs_p[0], outs_s[0], *outs_p[1:], *outs_s[1:])
```

```python
import functools
import math

import jax
import jax.numpy as jnp
import numpy as np
from jax import lax
from jax.experimental import pallas as pl
from jax.experimental.pallas import tpu as pltpu

F32 = jnp.float32
BF16 = jnp.bfloat16

CHUNK = 64
MLA_HEADS = 6
MLA_Q_RANK = 512
MLA_KV_RANK = 256
MLA_NOPE = 128
MLA_ROPE = 64
MLA_V = 128
ROPE_THETA = 10000.0
GDN_HEADS = 6
GDN_DK = 128
GDN_DV = 128
GDN_CONV = 4
GDN_KEY_DIM = GDN_HEADS * GDN_DK
GDN_VAL_DIM = GDN_HEADS * GDN_DV
GDN_CONV_DIM = 2 * GDN_KEY_DIM + GDN_VAL_DIM
CB_HEADS = 4
CB_DH = 128
CB_DIM = CB_HEADS * CB_DH
CB_PAST_ROWS = 8 * CHUNK
REL_CLIP = 256
N_EXPERTS = 16
N_GROUPS = 4
EXPERTS_PER_GROUP = N_EXPERTS // N_GROUPS
DEPTH = 2
DEEPNORM_ALPHA = (2 * DEPTH) ** 0.25

LANES = 128
C_GQKV = 0
C_GZ = C_GQKV + GDN_CONV_DIM
C_CQ = C_GZ + GDN_VAL_DIM
C_CKV = C_CQ + MLA_Q_RANK
C_KR = C_CKV + MLA_KV_RANK
C_GBA = C_KR + 2 * MLA_ROPE
C_CB = C_GBA + LANES
IN_PAD = C_CB + 3 * CB_DIM

SLAB_ROWS = 2048 // LANES
EXT_SEL = 0
EXT_W_LO = N_EXPERTS

VMEM_LIMIT = 56 * 1024 * 1024


def _cparams(sem):
    return pltpu.CompilerParams(dimension_semantics=sem, vmem_limit_bytes=VMEM_LIMIT)


def _dot(a, b):
    return jnp.dot(a, b, preferred_element_type=F32)


def _dot_nt(a, b):
    return lax.dot_general(a, b, (((1,), (1,)), ((), ())), preferred_element_type=F32)


def _dot_hi(a, b):
    return jnp.dot(a, b, preferred_element_type=F32, precision=lax.Precision.HIGHEST)


def _dot_nt_hi(a, b):
    return lax.dot_general(a, b, (((1,), (1,)), ((), ())), preferred_element_type=F32,
                           precision=lax.Precision.HIGHEST)


def _sigmoid(x):
    return 1.0 / (1.0 + jnp.exp(-x))


def _silu(x):
    return x * _sigmoid(x)


def _layer_norm(h, g, b, eps=1e-5):
    mu = jnp.mean(h, -1, keepdims=True)
    d = h - mu
    var = jnp.mean(d * d, -1, keepdims=True)
    return d * lax.rsqrt(var + eps) * g + b


def _rms_norm(x, g, eps=1e-6):
    return x * lax.rsqrt(jnp.mean(x * x, -1, keepdims=True) + eps) * g


def _inproj_kernel(x_ref, w_ref, o_ref, xb_ref):
    @pl.when(pl.program_id(1) == 0)
    def _():
        xb_ref[...] = x_ref[...].astype(BF16)

    o_ref[...] = _dot(xb_ref[...], w_ref[0])


def _in_proj(x2d, w, layer):
    n, d = x2d.shape
    width = w.shape[2]
    tm = min(1024, n)
    tn = 512
    return pl.pallas_call(
        _inproj_kernel,
        grid=(n // tm, width // tn),
        in_specs=[pl.BlockSpec((tm, d), lambda i, j: (i, 0)),
                  pl.BlockSpec((1, d, tn), lambda i, j: (layer, 0, j))],
        out_specs=pl.BlockSpec((tm, tn), lambda i, j: (i, j)),
        out_shape=jax.ShapeDtypeStruct((n, width), F32),
        scratch_shapes=[pltpu.VMEM((tm, d), BF16)],
        compiler_params=_cparams(("parallel", "arbitrary")),
        name="in_proj",
    )(x2d, w)


def _mla_prep_kernel(cq_ref, ckv_ref, kr_ref, cosq_ref, sinq_ref, cosk_ref, sink_ref,
                     qg_ref, kvg_ref, wqn_ref, wqr_ref, wqs_ref, wuk_ref,
                     q_ref, ckvn_ref, krn_ref):
    cqn = _rms_norm(cq_ref[0], qg_ref[...]).astype(BF16)
    q_nope = _dot(cqn, wqn_ref[...]).astype(BF16)
    q_rope = _dot(cqn, wqr_ref[...])
    q_rope_sw = _dot(cqn, wqs_ref[...])
    q_rot = (q_rope * cosq_ref[...] + q_rope_sw * sinq_ref[...]).astype(BF16)
    for h in range(MLA_HEADS):
        q_lat = _dot(q_nope[:, h * MLA_NOPE:(h + 1) * MLA_NOPE], wuk_ref[h])
        q_ref[0, h, :, 0:MLA_KV_RANK] = q_lat.astype(BF16)
        q_ref[0, h, :, MLA_KV_RANK:] = q_rot[:, h * MLA_ROPE:(h + 1) * MLA_ROPE]
    ckvn_ref[0] = _rms_norm(ckv_ref[0], kvg_ref[...])
    kr = kr_ref[0]
    krn_ref[0] = kr[:, :MLA_ROPE] * cosk_ref[...] + kr[:, MLA_ROPE:] * sink_ref[...]


def _mla_prep(proj, cos2, sin2, q_norm_g, kv_norm_g, wq_nope, wq_rope, wq_rope_sw, wuk_t):
    b, t, _ = proj.shape
    tm = min(512, t)
    cosq = jnp.tile(cos2, (1, MLA_HEADS))
    sinq = jnp.tile(sin2, (1, MLA_HEADS))
    full = lambda a: pl.BlockSpec(a.shape, lambda bi, i: (0,) * a.ndim)
    row = lambda w: pl.BlockSpec((tm, w), lambda bi, i: (i, 0))
    qd = MLA_KV_RANK + MLA_ROPE
    return pl.pallas_call(
        _mla_prep_kernel,
        grid=(b, t // tm),
        in_specs=[pl.BlockSpec((1, tm, MLA_Q_RANK), lambda bi, i: (bi, i, C_CQ // MLA_Q_RANK)),
                  pl.BlockSpec((1, tm, MLA_KV_RANK), lambda bi, i: (bi, i, C_CKV // MLA_KV_RANK)),
                  pl.BlockSpec((1, tm, 2 * MLA_ROPE), lambda bi, i: (bi, i, C_KR // (2 * MLA_ROPE))),
                  row(MLA_HEADS * MLA_ROPE), row(MLA_HEADS * MLA_ROPE), row(MLA_ROPE), row(MLA_ROPE),
                  full(q_norm_g), full(kv_norm_g), full(wq_nope), full(wq_rope), full(wq_rope_sw),
                  full(wuk_t)],
        out_specs=[pl.BlockSpec((1, MLA_HEADS, tm, qd), lambda bi, i: (bi, 0, i, 0)),
                   pl.BlockSpec((1, tm, MLA_KV_RANK), lambda bi, i: (bi, i, 0)),
                   pl.BlockSpec((1, tm, MLA_ROPE), lambda bi, i: (bi, i, 0))],
        out_shape=[jax.ShapeDtypeStruct((b, MLA_HEADS, t, qd), BF16),
                   jax.ShapeDtypeStruct((b, t, MLA_KV_RANK), F32),
                   jax.ShapeDtypeStruct((b, t, MLA_ROPE), F32)],
        compiler_params=_cparams(("parallel", "parallel")),
        name="mla_prep",
    )(proj, proj, proj, cosq, sinq, cos2, sin2, q_norm_g, kv_norm_g, wq_nope, wq_rope, wq_rope_sw, wuk_t)


def _mla_attn_kernel(qi_ref, kj_ref, q_ref, ckv_ref, kr_ref, wuv_ref, o_ref, m_ref, l_ref, acc_ref, *, start, tq, tk):
    step = pl.program_id(1)
    i = qi_ref[step]
    j = kj_ref[step]

    @pl.when(j == 0)
    def _():
        m_ref[...] = jnp.full(m_ref.shape, -jnp.inf, F32)
        l_ref[...] = jnp.zeros(l_ref.shape, F32)
        acc_ref[...] = jnp.zeros(acc_ref.shape, F32)

    ckv = ckv_ref[0].astype(BF16)
    kr = kr_ref[0].astype(BF16)
    scale = (MLA_NOPE + MLA_ROPE) ** -0.5

    hpg = 1 if tq >= 256 else MLA_HEADS
    grows = hpg * tq

    def update(masked):
        if masked:
            q_pos = start + i * tq + lax.broadcasted_iota(jnp.int32, (grows, 1), 0) % tq
            k_pos = j * tk + lax.broadcasted_iota(jnp.int32, (1, tk), 1)
            visible = k_pos // CHUNK <= q_pos // CHUNK
        def scores(g):
            q = q_ref[0, g] if hpg == 1 else q_ref[0].reshape(grows, MLA_KV_RANK + MLA_ROPE)
            return _dot_nt(q[:, :MLA_KV_RANK], ckv) + _dot_nt(q[:, MLA_KV_RANK:], kr)

        groups = MLA_HEADS // hpg
        s_next = scores(0)
        for g in range(groups):
            s, s_next = s_next, (scores(g + 1) if g + 1 < groups else None)
            s = s * scale
            if masked:
                s = jnp.where(visible, s, -jnp.inf)
            rows = slice(g * grows, (g + 1) * grows)
            m_old = m_ref[rows]
            m_new = jnp.maximum(m_old, jnp.max(s, -1, keepdims=True))
            alpha = jnp.exp(m_old - m_new)
            p = jnp.exp(s - m_new)
            l_ref[rows] = alpha * l_ref[rows] + jnp.sum(p, -1, keepdims=True)
            acc_ref[rows] = alpha * acc_ref[rows] + _dot(p.astype(BF16), ckv)
            m_ref[rows] = m_new

    needs_mask = (j * tk + tk - 1) // CHUNK > (start + i * tq) // CHUNK
    pl.when(needs_mask)(lambda: update(True))
    pl.when(jnp.logical_not(needs_mask))(lambda: update(False))

    @pl.when(j == _last_kv_block(start, i, tq, tk))
    def _():
        o_lat = (acc_ref[...] / l_ref[...]).astype(BF16)
        for h in range(MLA_HEADS):
            o_ref[0, :, h * MLA_V:(h + 1) * MLA_V] = _dot(o_lat[h * tq:(h + 1) * tq], wuv_ref[h]).astype(BF16)


def _mla_attn_cached_kernel(q_ref, pckv_ref, pkr_ref, nckv_ref, nkr_ref, wuv_ref, o_ref, m_ref, l_ref, acc_ref, *, nb):
    j = pl.program_id(1)
    tq = nckv_ref.shape[1]
    rows = MLA_HEADS * tq

    @pl.when(j == 0)
    def _():
        m_ref[...] = jnp.full(m_ref.shape, -jnp.inf, F32)
        l_ref[...] = jnp.zeros(l_ref.shape, F32)
        acc_ref[...] = jnp.zeros(acc_ref.shape, F32)

    def update(ckv_f32, kr_f32):
        ckv = ckv_f32.astype(BF16)
        kr = kr_f32.astype(BF16)
        q = q_ref[0].reshape(rows, MLA_KV_RANK + MLA_ROPE)
        s = (_dot_nt(q[:, :MLA_KV_RANK], ckv) + _dot_nt(q[:, MLA_KV_RANK:], kr)) * ((MLA_NOPE + MLA_ROPE) ** -0.5)
        m_old = m_ref[...]
        m_new = jnp.maximum(m_old, jnp.max(s, -1, keepdims=True))
        alpha = jnp.exp(m_old - m_new)
        p = jnp.exp(s - m_new)
        l_ref[...] = alpha * l_ref[...] + jnp.sum(p, -1, keepdims=True)
        acc_ref[...] = alpha * acc_ref[...] + _dot(p.astype(BF16), ckv)
        m_ref[...] = m_new

    pl.when(j < nb)(lambda: update(pckv_ref[0], pkr_ref[0]))

    @pl.when(j == nb)
    def _():
        update(nckv_ref[0], nkr_ref[0])
        o_lat = (acc_ref[...] / l_ref[...]).astype(BF16)
        for h in range(MLA_HEADS):
            o_ref[0, :, h * MLA_V:(h + 1) * MLA_V] = _dot(o_lat[h * tq:(h + 1) * tq], wuv_ref[h]).astype(BF16)


def _mla_attn_cached(q, ckv_cache, kr_cache, layer, ckv_new, kr_new, wuv):
    b, _, t, qd = q.shape
    past = ckv_cache.shape[1]
    assert t == CHUNK and past % CHUNK == 0
    tk = next((c for c in (2048, 1024) if past % c == 0), None) or _pick_tk(past)
    nb = past // tk
    past_map = lambda bi, j: (layer * b + bi, jnp.minimum(j, nb - 1), 0)
    new_map = lambda bi, j: (bi, 0, 0)
    return pl.pallas_call(
        functools.partial(_mla_attn_cached_kernel, nb=nb),
        grid=(b, nb + 1),
        in_specs=[pl.BlockSpec((1, MLA_HEADS, t, qd), lambda bi, j: (bi, 0, 0, 0)),
                  pl.BlockSpec((1, tk, MLA_KV_RANK), past_map),
                  pl.BlockSpec((1, tk, MLA_ROPE), past_map),
                  pl.BlockSpec((1, t, MLA_KV_RANK), new_map),
                  pl.BlockSpec((1, t, MLA_ROPE), new_map),
                  pl.BlockSpec(wuv.shape, lambda bi, j: (0, 0, 0))],
        out_specs=pl.BlockSpec((1, t, MLA_HEADS * MLA_V), lambda bi, j: (bi, 0, 0)),
        out_shape=jax.ShapeDtypeStruct((b, t, MLA_HEADS * MLA_V), BF16),
        scratch_shapes=[pltpu.VMEM((MLA_HEADS * t, 1), F32),
                        pltpu.VMEM((MLA_HEADS * t, 1), F32),
                        pltpu.VMEM((MLA_HEADS * t, MLA_KV_RANK), F32)],
        compiler_params=_cparams(("parallel", "arbitrary")),
        name="mla_attn_cached",
    )(q, ckv_cache, kr_cache, ckv_new, kr_new, wuv)


def _last_kv_block(start, i, tq, tk):
    return ((start + i * tq + tq - 1) // CHUNK * CHUNK) // tk


def _pick_tk(s):
    for cand in (512, 1024, 832, 768, 640, 576, 448, 384, 320, 256, 192, 128, 64):
        if s % cand == 0:
            return cand
    raise ValueError(f"unsupported key length {s}")


def _mla_attn(q, ckv_all, kr_all, wuv, start):
    b, _, t, qd = q.shape
    s = ckv_all.shape[1]
    tq = min(256, t)
    tk = _pick_tk(s)
    assert tk % CHUNK == 0 and s % tk == 0 and t % tq == 0
    pairs = [(i, j) for i in range(t // tq) for j in range(_last_kv_block(start, i, tq, tk) + 1)]
    qi = jnp.asarray(np.array([p[0] for p in pairs], np.int32))
    kj = jnp.asarray(np.array([p[1] for p in pairs], np.int32))
    kv_map = lambda bi, st, qi_r, kj_r: (bi, kj_r[st], 0)
    return pl.pallas_call(
        functools.partial(_mla_attn_kernel, start=start, tq=tq, tk=tk),
        grid_spec=pltpu.PrefetchScalarGridSpec(
            num_scalar_prefetch=2, grid=(b, len(pairs)),
            in_specs=[pl.BlockSpec((1, MLA_HEADS, tq, qd), lambda bi, st, qi_r, kj_r: (bi, 0, qi_r[st], 0)),
                      pl.BlockSpec((1, tk, MLA_KV_RANK), kv_map),
                      pl.BlockSpec((1, tk, MLA_ROPE), kv_map),
                      pl.BlockSpec(wuv.shape, lambda bi, st, qi_r, kj_r: (0, 0, 0))],
            out_specs=pl.BlockSpec((1, tq, MLA_HEADS * MLA_V), lambda bi, st, qi_r, kj_r: (bi, qi_r[st], 0)),
            scratch_shapes=[pltpu.VMEM((MLA_HEADS * tq, 1), F32),
                            pltpu.VMEM((MLA_HEADS * tq, 1), F32),
                            pltpu.VMEM((MLA_HEADS * tq, MLA_KV_RANK), F32)]),
        out_shape=jax.ShapeDtypeStruct((b, t, MLA_HEADS * MLA_V), BF16),
        compiler_params=_cparams(("parallel", "arbitrary")),
        name="mla_attn",
    )(qi, kj, q, ckv_all, kr_all, wuv)


def _split_bf16(x):
    hi = x.astype(BF16)
    return hi, (x - hi.astype(F32)).astype(BF16)


def _unit_lower_solve_many(a_list, rhs_list):
    n = rhs_list[0].shape[1]
    levels = int(math.log2(CHUNK))
    xs, ps = list(rhs_list), list(a_list)
    for lvl in range(levels):
        for h in range(len(xs)):
            p_hi, p_lo = _split_bf16(ps[h])
            lhs = jnp.concatenate([p_hi, p_hi, p_lo], 1)
            if lvl < levels - 1:
                r_hi, r_lo = _split_bf16(jnp.concatenate([xs[h], ps[h]], 1))
                both = _dot(lhs, jnp.concatenate([r_hi, r_lo, r_hi], 0))
                px, ps[h] = both[:, :n], both[:, n:]
            else:
                r_hi, r_lo = _split_bf16(xs[h])
                px = _dot(lhs, jnp.concatenate([r_hi, r_lo, r_hi], 0))
            xs[h] = xs[h] - px if lvl == 0 else xs[h] + px
    return xs


def _gdn_kernel(qkv_ref, gz_ref, gba_ref, convp_ref, convw_ref, alog_ref, dtb_ref, gnorm_ref, s0_ref,
                ob_ref, snew_ref, convn_ref, s_scr, ext_scr, *, n_chunks):
    n = pl.program_id(1)
    L = CHUNK
    tail = 8

    @pl.when(n == 0)
    def _():
        s_scr[...] = s0_ref[0]
        ext_scr[0:tail, :] = jnp.zeros((tail, GDN_CONV_DIM), F32)
        ext_scr[tail - (GDN_CONV - 1):tail, :] = convp_ref[0]

    cur = qkv_ref[0]
    ext_scr[tail:tail + L, :] = cur
    w = convw_ref[...]
    conv = ext_scr[tail - 3:tail - 3 + L, :] * w[0:1]
    conv = conv + ext_scr[tail - 2:tail - 2 + L, :] * w[1:2]
    conv = conv + ext_scr[tail - 1:tail - 1 + L, :] * w[2:3]
    conv = conv + cur * w[3:4]
    conv = _silu(conv)

    @pl.when(n == n_chunks - 1)
    def _():
        convn_ref[0] = ext_scr[tail + L - (GDN_CONV - 1):tail + L, :]

    ext_scr[0:tail, :] = cur[L - tail:, :]

    gba = gba_ref[0]
    beta_all = _sigmoid(gba)
    z = gba + dtb_ref[...]
    softplus = jnp.maximum(z, 0.0) + jnp.log1p(jnp.exp(-jnp.abs(z)))
    g_all = -jnp.exp(alog_ref[...]) * softplus
    ri = lax.broadcasted_iota(jnp.int32, (L, L), 0)
    ci = lax.broadcasted_iota(jnp.int32, (L, L), 1)
    incl = ci <= ri
    strict = ci < ri
    g_cum = _dot_hi(incl.astype(F32), g_all)
    g_cum_t = g_cum.T

    def l2n(x):
        return x * lax.rsqrt(jnp.sum(x * x, -1, keepdims=True) + 1e-6)

    heads = range(GDN_HEADS)
    q = [l2n(conv[:, h * GDN_DK:(h + 1) * GDN_DK]) * (GDN_DK ** -0.5) for h in heads]
    k = [l2n(conv[:, GDN_KEY_DIM + h * GDN_DK:GDN_KEY_DIM + (h + 1) * GDN_DK]) for h in heads]
    v = [conv[:, 2 * GDN_KEY_DIM + h * GDN_DV:2 * GDN_KEY_DIM + (h + 1) * GDN_DV] for h in heads]
    beta = [beta_all[:, h:h + 1] for h in heads]
    gc = [g_cum[:, GDN_HEADS + h:GDN_HEADS + h + 1] for h in heads]
    dmat = [jnp.exp(jnp.where(incl, gc[h] - g_cum_t[GDN_HEADS + h:GDN_HEADS + h + 1, :], -jnp.inf)) for h in heads]
    eg = [jnp.exp(gc[h]) for h in heads]
    qk_kk = []
    for h in heads:
        kb = k[h].astype(BF16)
        qk_kk.append(_dot_nt(jnp.concatenate([q[h].astype(BF16), kb], 0), kb))
    a_mat = [jnp.where(strict, beta[h] * qk_kk[h][L:] * dmat[h], 0.0) for h in heads]
    rhs = [jnp.concatenate([beta[h] * v[h], (beta[h] * eg[h]) * k[h]], -1) for h in heads]
    sol = _unit_lower_solve_many(a_mat, rhs)
    s_old = [s_scr[h] for h in heads]
    wq_s = [_dot(jnp.concatenate([sol[h][:, GDN_DV:], q[h]], 0).astype(BF16), s_old[h].astype(BF16)) for h in heads]
    upd = []
    for h in heads:
        delta = sol[h][:, :GDN_DV] - wq_s[h][:L]
        kd = k[h] * jnp.exp(gc[h][L - 1:L, :] - gc[h])
        lhs = jnp.concatenate([qk_kk[h][:L] * dmat[h], kd.T], 0)
        upd.append(_dot(lhs.astype(BF16), delta.astype(BF16)))
    for h in heads:
        o = eg[h] * wq_s[h][L:] + upd[h][:L]
        s_scr[h] = jnp.exp(gc[h][L - 1:L, :]) * s_old[h] + upd[h][L:]
        gz = gz_ref[0, :, h * GDN_DV:(h + 1) * GDN_DV]
        ob_ref[0, :, h * GDN_DV:(h + 1) * GDN_DV] = (_rms_norm(o, gnorm_ref[...]) * _silu(gz)).astype(BF16)

    @pl.when(n == n_chunks - 1)
    def _():
        snew_ref[0] = s_scr[...]


def _gdn(proj, conv_past, conv_w, alog128, dtb128, gnorm, s0):
    b, t, _ = proj.shape
    L = CHUNK
    full = lambda a: pl.BlockSpec(a.shape, lambda bi, n: (0,) * a.ndim)
    return pl.pallas_call(
        functools.partial(_gdn_kernel, n_chunks=t // L),
        grid=(b, t // L),
        in_specs=[pl.BlockSpec((1, L, GDN_CONV_DIM), lambda bi, n: (bi, n, C_GQKV // GDN_CONV_DIM)),
                  pl.BlockSpec((1, L, GDN_VAL_DIM), lambda bi, n: (bi, n, C_GZ // GDN_VAL_DIM)),
                  pl.BlockSpec((1, L, LANES), lambda bi, n: (bi, n, C_GBA // LANES)),
                  pl.BlockSpec((1, GDN_CONV - 1, GDN_CONV_DIM), lambda bi, n: (bi, 0, 0)),
                  full(conv_w), full(alog128), full(dtb128), full(gnorm),
                  pl.BlockSpec((1, GDN_HEADS, GDN_DK, GDN_DV), lambda bi, n: (bi, 0, 0, 0))],
        out_specs=[pl.BlockSpec((1, L, GDN_VAL_DIM), lambda bi, n: (bi, n, 0)),
                   pl.BlockSpec((1, GDN_HEADS, GDN_DK, GDN_DV), lambda bi, n: (bi, 0, 0, 0)),
                   pl.BlockSpec((1, GDN_CONV - 1, GDN_CONV_DIM), lambda bi, n: (bi, 0, 0))],
        out_shape=[jax.ShapeDtypeStruct((b, t, GDN_VAL_DIM), BF16),
                   jax.ShapeDtypeStruct((b, GDN_HEADS, GDN_DK, GDN_DV), F32),
                   jax.ShapeDtypeStruct((b, GDN_CONV - 1, GDN_CONV_DIM), F32)],
        scratch_shapes=[pltpu.VMEM((GDN_HEADS, GDN_DK, GDN_DV), F32),
                        pltpu.VMEM((8 + L, GDN_CONV_DIM), F32)],
        compiler_params=_cparams(("parallel", "arbitrary")),
        name="gdn",
    )(proj, proj, proj, conv_past, conv_w, alog128, dtb128, gnorm, s0)


def _cb_attn_kernel(q_ref, kprev_ref, kcur_ref, vprev_ref, vcur_ref, bias_ref, o_ref, *, tq, pad):
    i = pl.program_id(1)
    L = CHUNK
    width = CB_PAST_ROWS + L
    scale = CB_DH ** -0.5
    for c in range(tq // L):
        lo = c * L
        kwin = jnp.concatenate([kprev_ref[0, lo:, :], kcur_ref[0, :lo + L, :]], 0).astype(BF16)
        vwin = jnp.concatenate([vprev_ref[0, lo:, :], vcur_ref[0, :lo + L, :]], 0).astype(BF16)
        q = q_ref[0, lo:lo + L, :].astype(BF16)
        row = i * tq + lo + lax.broadcasted_iota(jnp.int32, (1, width), 1)
        valid = row >= pad
        heads = [slice(h * CB_DH, (h + 1) * CB_DH) for h in range(CB_HEADS)]
        scores = [_dot_nt(q[:, hs], kwin[:, hs]) for hs in heads]
        probs = []
        for h in range(CB_HEADS):
            s = scores[h] * scale + bias_ref[h]
            s = jnp.where(valid, s, -jnp.inf)
            m = jnp.max(s, -1, keepdims=True)
            p = jnp.exp(s - m)
            probs.append((p / jnp.sum(p, -1, keepdims=True)).astype(BF16))
        for h, hs in enumerate(heads):
            o_ref[0, lo:lo + L, hs] = _dot(probs[h], vwin[:, hs]).astype(BF16)


def _cb_attn(proj, k_cache, v_cache, layer, bias):
    b, t, _ = proj.shape
    tq = min(CB_PAST_ROWS, t)
    kcol, vcol = C_CB // CB_DIM + 1, C_CB // CB_DIM + 2
    cur = lambda col: pl.BlockSpec((1, tq, CB_DIM), lambda bi, i: (bi, i, col))
    if k_cache is None:
        assert tq == CB_PAST_ROWS and t % tq == 0
        pad = CB_PAST_ROWS
        prev = lambda col: pl.BlockSpec((1, tq, CB_DIM), lambda bi, i: (bi, jnp.maximum(i - 1, 0), col))
        k_prev_arr, v_prev_arr, k_prev, v_prev = proj, proj, prev(kcol), prev(vcol)
    else:
        assert t == tq and k_cache.shape[1] == CB_PAST_ROWS
        pad = 0
        cache_spec = pl.BlockSpec((1, CB_PAST_ROWS, CB_DIM), lambda bi, i: (layer * b + bi, 0, 0))
        k_prev_arr, v_prev_arr, k_prev, v_prev = k_cache, v_cache, cache_spec, cache_spec
    return pl.pallas_call(
        functools.partial(_cb_attn_kernel, tq=tq, pad=pad),
        grid=(b, t // tq),
        in_specs=[cur(C_CB // CB_DIM), k_prev, cur(kcol), v_prev, cur(vcol),
                  pl.BlockSpec(bias.shape, lambda bi, i: (0, 0, 0))],
        out_specs=pl.BlockSpec((1, tq, CB_DIM), lambda bi, i: (bi, i, 0)),
        out_shape=jax.ShapeDtypeStruct((b, t, CB_DIM), BF16),
        compiler_params=_cparams(("parallel", "parallel")),
        name="cb_attn",
    )(proj, k_prev_arr, proj, v_prev_arr, proj, bias)


def _route(logits_t, rb):
    s = _sigmoid(logits_t)
    sb = s + rb
    rows = [sb[e:e + 1, :] for e in range(N_EXPERTS)]
    grp = []
    for g in range(N_GROUPS):
        r = rows[g * EXPERTS_PER_GROUP:(g + 1) * EXPERTS_PER_GROUP]
        best = None
        for a in range(EXPERTS_PER_GROUP):
            for c in range(a + 1, EXPERTS_PER_GROUP):
                pair = r[a] + r[c]
                best = pair if best is None else jnp.maximum(best, pair)
        grp.append(best)
    gmax = functools.reduce(jnp.maximum, grp)
    gsel = jnp.full(gmax.shape, N_GROUPS, jnp.int32)
    for g in reversed(range(N_GROUPS)):
        gsel = jnp.where(grp[g] == gmax, g, gsel)
    sel = []
    for e in range(N_EXPERTS):
        g = e // EXPERTS_PER_GROUP
        rank = jnp.zeros(gmax.shape, jnp.int32)
        for e2 in range(g * EXPERTS_PER_GROUP, (g + 1) * EXPERTS_PER_GROUP):
            if e2 == e:
                continue
            ahead = (rows[e2] >= rows[e]) if e2 < e else (rows[e2] > rows[e])
            rank = rank + ahead.astype(jnp.int32)
        sel.append(jnp.where((gsel == g) & (rank < 2), 1.0, 0.0))
    ssum = functools.reduce(lambda a, c: a + c, [sel[e] * s[e:e + 1, :] for e in range(N_EXPERTS)])
    zero = jnp.zeros(gmax.shape, F32)
    seen, w_lo, w_hi, e_lo, e_hi = zero, zero, zero, zero, zero
    for e in range(N_EXPERTS):
        gate_e = sel[e] * s[e:e + 1, :] / ssum
        first = sel[e] * jnp.where(seen == 0.0, 1.0, 0.0)
        second = sel[e] - first
        w_lo, w_hi = w_lo + first * gate_e, w_hi + second * gate_e
        e_lo, e_hi = e_lo + first * e, e_hi + second * e
        seen = seen + sel[e]
    return sel, w_lo, w_hi, e_lo, e_hi


def _to_slab(ref, val, tok0=0):
    tm = val.shape[0]
    for c in range(val.shape[1] // LANES):
        ref[pl.ds(tok0 * SLAB_ROWS + c, tm, stride=SLAB_ROWS), :] = val[:, c * LANES:(c + 1) * LANES]


def _out_ln1_kernel(oa_ref, ob_ref, oc_ref, x_ref, wa_ref, wb_ref, wc_ref, g_ref, b_ref, rwt_ref, rb_ref,
                    x1_ref, ext_ref):
    tm = x_ref.shape[0]
    halves = [slice(k * (tm // 2), (k + 1) * (tm // 2)) for k in range(2)]
    ys = [_dot(oa_ref[r, :], wa_ref[...]) + _dot(ob_ref[r, :], wb_ref[...]) + _dot(oc_ref[r, :], wc_ref[...])
          for r in halves]
    for r, y in zip(halves, ys):
        x1 = _layer_norm(DEEPNORM_ALPHA * x_ref[r, :] + y, g_ref[...], b_ref[...])
        _to_slab(x1_ref, x1, r.start)
        logits_t = _dot_nt_hi(rwt_ref[...], x1)
        sel, w_lo, w_hi, e_lo, e_hi = _route(logits_t, rb_ref[...])
        rows = sel + [w_lo, w_hi, e_lo, e_hi]
        ext = jnp.concatenate(rows + [jnp.zeros((LANES - len(rows), x1.shape[0]), F32)], 0)
        ext_ref[r, :] = ext.T


def _out_ln1(oa, ob, oc, x2d, wa, wb, wc, g, bb, rwt, rb):
    n, d = x2d.shape
    tm = min(256, n)
    full = lambda a: pl.BlockSpec(a.shape, lambda i: (0,) * a.ndim)
    row = lambda w: pl.BlockSpec((tm, w), lambda i: (i, 0))
    return pl.pallas_call(
        _out_ln1_kernel,
        grid=(n // tm,),
        in_specs=[row(oa.shape[1]), row(ob.shape[1]), row(oc.shape[1]), row(d),
                  full(wa), full(wb), full(wc), full(g), full(bb), full(rwt), full(rb)],
        out_specs=[pl.BlockSpec((tm * SLAB_ROWS, LANES), lambda i: (i, 0)), row(LANES)],
        out_shape=[jax.ShapeDtypeStruct((n * SLAB_ROWS, LANES), F32), jax.ShapeDtypeStruct((n, LANES), F32)],
        compiler_params=_cparams(("parallel",)),
        name="out_ln1",
    )(oa, ob, oc, x2d, wa, wb, wc, g, bb, rwt, rb)


def _row_copy(src_hbm, src_row, dst, dst_row, sem):
    return pltpu.make_async_copy(src_hbm.at[pl.ds(pl.multiple_of(src_row * SLAB_ROWS, SLAB_ROWS), SLAB_ROWS)],
                                 dst.at[pl.ds(pl.multiple_of(dst_row * SLAB_ROWS, SLAB_ROWS), SLAB_ROWS)], sem)


def _moe_ffn_kernel(tile_e_ref, nact_ref, src_ref, x_hbm, wg_ref, wu_ref, wd_ref, ys_ref,
                    xg0_ref, xg1_ref, xb_ref, acc_ref, sem):
    i = pl.program_id(0)
    f = pl.program_id(1)
    tm = xb_ref.shape[0]
    bufs = (xg0_ref, xg1_ref)

    def gather_start(tile, slot):
        base = tile * tm

        def issue(r, c):
            _row_copy(x_hbm, src_ref[base + r], bufs[slot], r, sem.at[slot]).start()
            return c

        lax.fori_loop(0, tm, issue, 0, unroll=8)

    def gather_wait(slot):
        pltpu.make_async_copy(x_hbm.at[pl.ds(0, tm * SLAB_ROWS)], bufs[slot], sem.at[slot]).wait()

    @pl.when(i < nact_ref[0])
    def _():
        for slot in range(2):
            @pl.when((f == 0) & (i % 2 == slot))
            def _():
                @pl.when(i == 0)
                def _():
                    gather_start(0, slot)

                @pl.when(i + 1 < nact_ref[0])
                def _():
                    gather_start(i + 1, 1 - slot)

                gather_wait(slot)
                for c in range(xb_ref.shape[1] // LANES):
                    xb_ref[:, c * LANES:(c + 1) * LANES] = bufs[slot][pl.ds(c, tm, stride=SLAB_ROWS), :].astype(BF16)

        xb = xb_ref[...]
        h = (_silu(_dot(xb, wg_ref[0])) * _dot(xb, wu_ref[0])).astype(BF16)
        y = _dot(h, wd_ref[0])

        @pl.when(f == 0)
        def _():
            acc_ref[...] = y

        @pl.when(f > 0)
        def _():
            acc_ref[...] += y

        @pl.when(f == pl.num_programs(1) - 1)
        def _():
            _to_slab(ys_ref, acc_ref[...])

    @pl.when((i >= nact_ref[0]) & (f == pl.num_programs(1) - 1))
    def _():
        ys_ref[...] = jnp.zeros(ys_ref.shape, F32)


def _moe_ffn(tile_e, nact, src, x_slab, wg, wu, wd, tm, layer):
    n_rows = src.shape[0]
    _, d, ff = wg.shape
    tf = min(512, ff)
    nf = ff // tf
    fe = lambda i, f, na: jnp.where(i < na[0], f, nf - 1)
    ex = lambda i, te: layer * N_EXPERTS + te[i]
    return pl.pallas_call(
        _moe_ffn_kernel,
        grid_spec=pltpu.PrefetchScalarGridSpec(
            num_scalar_prefetch=3, grid=(n_rows // tm, nf),
            in_specs=[pl.BlockSpec(memory_space=pl.ANY),
                      pl.BlockSpec((1, d, tf), lambda i, f, te, na, sr: (ex(i, te), 0, fe(i, f, na))),
                      pl.BlockSpec((1, d, tf), lambda i, f, te, na, sr: (ex(i, te), 0, fe(i, f, na))),
                      pl.BlockSpec((1, tf, d), lambda i, f, te, na, sr: (ex(i, te), fe(i, f, na), 0))],
            out_specs=pl.BlockSpec((tm * SLAB_ROWS, LANES), lambda i, f, te, na, sr: (i, 0)),
            scratch_shapes=[pltpu.VMEM((tm * SLAB_ROWS, LANES), F32), pltpu.VMEM((tm * SLAB_ROWS, LANES), F32),
                            pltpu.VMEM((tm, d), BF16), pltpu.VMEM((tm, d), F32), pltpu.SemaphoreType.DMA((2,))]),
        out_shape=jax.ShapeDtypeStruct((n_rows * SLAB_ROWS, LANES), F32),
        compiler_params=_cparams(("arbitrary", "arbitrary")),
        name="moe_ffn",
    )(tile_e, nact, src, x_slab, wg, wu, wd)


def _moe_combine_kernel(pos0_ref, pos1_ref, x1_ref, ext_ref, ys_hbm, g_ref, b_ref, o_ref,
                        buf0_ref, buf1_ref, h_ref, sem):
    tm = o_ref.shape[0]
    base = pl.program_id(0) * tm

    def issue(r, c):
        _row_copy(ys_hbm, pos0_ref[base + r], buf0_ref, r, sem.at[0]).start()
        _row_copy(ys_hbm, pos1_ref[base + r], buf1_ref, r, sem.at[1]).start()
        return c

    lax.fori_loop(0, tm, issue, 0, unroll=8)
    pltpu.make_async_copy(ys_hbm.at[pl.ds(0, tm * SLAB_ROWS)], buf0_ref, sem.at[0]).wait()
    pltpu.make_async_copy(ys_hbm.at[pl.ds(0, tm * SLAB_ROWS)], buf1_ref, sem.at[1]).wait()

    ext = ext_ref[...]
    w_lo = ext[:, EXT_W_LO:EXT_W_LO + 1]
    w_hi = ext[:, EXT_W_LO + 1:EXT_W_LO + 2]
    for c in range(o_ref.shape[1] // LANES):
        rows = pl.ds(c, tm, stride=SLAB_ROWS)
        y = w_lo * buf0_ref[rows, :] + w_hi * buf1_ref[rows, :]
        h_ref[:, c * LANES:(c + 1) * LANES] = DEEPNORM_ALPHA * x1_ref[rows, :] + y
    o_ref[...] = _layer_norm(h_ref[...], g_ref[...], b_ref[...])


def _moe_combine(pos0, pos1, x1_slab, ext, ys, g, bb):
    n = ext.shape[0]
    d = g.shape[1]
    tm = min(512, n)
    return pl.pallas_call(
        _moe_combine_kernel,
        grid_spec=pltpu.PrefetchScalarGridSpec(
            num_scalar_prefetch=2, grid=(n // tm,),
            in_specs=[pl.BlockSpec((tm * SLAB_ROWS, LANES), lambda i, p0, p1: (i, 0)),
                      pl.BlockSpec((tm, LANES), lambda i, p0, p1: (i, 0)),
                      pl.BlockSpec(memory_space=pl.ANY),
                      pl.BlockSpec(g.shape, lambda i, p0, p1: (0, 0)),
                      pl.BlockSpec(bb.shape, lambda i, p0, p1: (0, 0))],
            out_specs=pl.BlockSpec((tm, d), lambda i, p0, p1: (i, 0)),
            scratch_shapes=[pltpu.VMEM((tm * SLAB_ROWS, LANES), F32), pltpu.VMEM((tm * SLAB_ROWS, LANES), F32),
                            pltpu.VMEM((tm, d), F32), pltpu.SemaphoreType.DMA((2,))]),
        out_shape=jax.ShapeDtypeStruct((n, d), F32),
        compiler_params=_cparams(("arbitrary",)),
        name="moe_combine",
    )(pos0, pos1, x1_slab, ext, ys, g, bb)


def _moe_src_kernel(pos0_ref, pos1_ref, src_ref, *, n, n_rows):
    def clear(p, c):
        src_ref[p] = 0
        return c

    lax.fori_loop(0, n_rows, clear, 0, unroll=8)

    def put(t, c):
        src_ref[pos0_ref[t]] = t
        src_ref[pos1_ref[t]] = t
        return c

    lax.fori_loop(0, n, put, 0, unroll=8)


def _moe_src(pos0, pos1, n_rows):
    n = pos0.shape[0]
    return pl.pallas_call(
        functools.partial(_moe_src_kernel, n=n, n_rows=n_rows),
        grid_spec=pltpu.PrefetchScalarGridSpec(
            num_scalar_prefetch=2, grid=(1,), in_specs=[],
            out_specs=pl.BlockSpec(memory_space=pltpu.SMEM)),
        out_shape=jax.ShapeDtypeStruct((n_rows,), jnp.int32),
        name="moe_src",
    )(pos0, pos1)


def _moe_ln2(x1_slab, ext, wg, wu, wd, layer, g, bb):
    n = ext.shape[0]
    tm = 512 if n >= 8192 else 256
    n_rows = 2 * n + N_EXPERTS * tm
    sel = (ext[:, EXT_SEL:EXT_SEL + N_EXPERTS] > 0.5).astype(jnp.int32)
    csum = jnp.cumsum(sel, axis=0)
    padded = (csum[-1] + tm - 1) // tm * tm
    seg_end = jnp.cumsum(padded)
    slot = (seg_end - padded)[None, :] + csum - sel
    experts = jnp.arange(N_EXPERTS, dtype=jnp.int32)[None, :]
    e_lo = ext[:, EXT_W_LO + 2].astype(jnp.int32)[:, None]
    e_hi = ext[:, EXT_W_LO + 3].astype(jnp.int32)[:, None]
    pos0 = jnp.sum(jnp.where(experts == e_lo, slot, 0), axis=1)
    pos1 = jnp.sum(jnp.where(experts == e_hi, slot, 0), axis=1)
    nact = seg_end[-1:] // tm
    tiles = jnp.arange(n_rows // tm, dtype=jnp.int32)
    first_row = jnp.minimum(tiles, nact[0] - 1) * tm
    tile_e = jnp.minimum(jnp.sum((seg_end[None, :] <= first_row[:, None]).astype(jnp.int32), axis=1), N_EXPERTS - 1)

    src = _moe_src(pos0, pos1, n_rows)
    ys = _moe_ffn(tile_e, nact, src, x1_slab, wg, wu, wd, tm, layer)
    return _moe_combine(pos0, pos1, x1_slab, ext, ys, g, bb)


def _band_bias_kernel(tab_ref, o_ref):
    size, width = tab_ref.shape[1], o_ref.shape[2]
    entry = lax.broadcasted_iota(jnp.int32, (size, width), 0)
    w = lax.broadcasted_iota(jnp.int32, (size, width), 1)
    tab = tab_ref[...]
    for l in range(o_ref.shape[1]):
        idx = jnp.clip(CB_PAST_ROWS + l - w, -REL_CLIP, REL_CLIP) + REL_CLIP
        o_ref[:, l, :] = _dot_hi(tab, jnp.where(entry == idx, 1.0, 0.0))


def _band_bias(rel_bias):
    depth, h, size = rel_bias.shape
    size_pad = -(-size // LANES) * LANES
    tab = jnp.pad(rel_bias.reshape(depth * h, size), ((0, 0), (0, size_pad - size)))
    width = CB_PAST_ROWS + CHUNK
    out = pl.pallas_call(
        _band_bias_kernel,
        out_shape=jax.ShapeDtypeStruct((depth * h, CHUNK, width), F32),
        name="band_bias",
    )(tab)
    return out.reshape(depth, h, CHUNK, width)


_HALF_ROPE = MLA_ROPE // 2
_SRC_KR = MLA_Q_RANK + MLA_KV_RANK
_SRC_GQKV = _SRC_KR + MLA_ROPE
_SRC_GZ = _SRC_GQKV + GDN_CONV_DIM
_SRC_GBA = _SRC_GZ + GDN_VAL_DIM
_SRC_CB = _SRC_GBA + 2 * GDN_HEADS
W_IN_SEGMENTS = (
    (C_GQKV, _SRC_GQKV, GDN_CONV_DIM), (C_GZ, _SRC_GZ, GDN_VAL_DIM), (C_CQ, 0, MLA_Q_RANK),
    (C_CKV, MLA_Q_RANK, MLA_KV_RANK), (C_KR, _SRC_KR, MLA_ROPE),
    (C_KR + MLA_ROPE, _SRC_KR + _HALF_ROPE, _HALF_ROPE), (C_KR + MLA_ROPE + _HALF_ROPE, _SRC_KR, _HALF_ROPE),
    (C_GBA, _SRC_GBA, 2 * GDN_HEADS), (C_CB, _SRC_CB, 3 * CB_DIM))


def _w_in_relayout_kernel(w_ref, o_ref):
    o_ref[0, :, C_GBA:C_GBA + LANES] = jnp.zeros((o_ref.shape[1], LANES), BF16)
    for dst, src, width in W_IN_SEGMENTS:
        o_ref[0, :, dst:dst + width] = w_ref[0, :, src:src + width].astype(BF16)


def _w_in_relayout(w_in):
    depth, d, width = w_in.shape
    tm = 256
    return pl.pallas_call(
        _w_in_relayout_kernel,
        grid=(depth, d // tm),
        in_specs=[pl.BlockSpec((1, tm, width), lambda l, i: (l, i, 0))],
        out_specs=pl.BlockSpec((1, tm, IN_PAD), lambda l, i: (l, i, 0)),
        out_shape=jax.ShapeDtypeStruct((depth, d, IN_PAD), BF16),
        compiler_params=_cparams(("parallel", "parallel")),
        name="w_in_relayout",
    )(w_in)


def _prep_layer(q_norm_g, w_uq, kv_norm_g, w_uk, w_uv, conv_w, a_log, dt_bias, gdn_norm_g, rel_bias, w_out,
                ln1_g, ln1_b, ln2_g, ln2_b):
    half = MLA_ROPE // 2
    r = w_uq.shape[0]
    wq_nope = w_uq[:, :, :MLA_NOPE].reshape(r, MLA_HEADS * MLA_NOPE).astype(BF16)
    wq_r = w_uq[:, :, MLA_NOPE:]
    wq_rope = wq_r.reshape(r, MLA_HEADS * MLA_ROPE).astype(BF16)
    wq_rope_sw = jnp.concatenate([wq_r[..., half:], wq_r[..., :half]], -1).reshape(r, MLA_HEADS * MLA_ROPE).astype(BF16)
    wuk_t = jnp.transpose(w_uk, (1, 2, 0)).astype(BF16)
    wuv = jnp.transpose(w_uv, (1, 0, 2)).astype(BF16)
    lane_pad = lambda a: jnp.pad(a, (GDN_HEADS, LANES - 2 * GDN_HEADS))[None, :]
    bias = rel_bias
    w_out_b = w_out.astype(BF16)
    na = MLA_HEADS * MLA_V
    return dict(
        q_norm_g=q_norm_g[None, :], kv_norm_g=kv_norm_g[None, :],
        wq_nope=wq_nope, wq_rope=wq_rope, wq_rope_sw=wq_rope_sw, wuk_t=wuk_t, wuv=wuv,
        conv_w=conv_w, alog=lane_pad(a_log), dtb=lane_pad(dt_bias), gnorm=gdn_norm_g[None, :], bias=bias,
        wo_a=w_out_b[:na], wo_b=w_out_b[na:na + GDN_VAL_DIM], wo_c=w_out_b[na + GDN_VAL_DIM:],
        ln1_g=ln1_g[None, :], ln1_b=ln1_b[None, :], ln2_g=ln2_g[None, :], ln2_b=ln2_b[None, :])


def _rope_tables(start, t):
    pos = start + jnp.arange(t, dtype=jnp.int32)
    inv = ROPE_THETA ** (-jnp.arange(0, MLA_ROPE, 2, dtype=F32) / MLA_ROPE)
    ang = pos.astype(F32)[:, None] * inv[None, :]
    cos, sin = jnp.cos(ang), jnp.sin(ang)
    return jnp.concatenate([cos, cos], -1), jnp.concatenate([-sin, sin], -1)


def _layer(x, p, layer, shared, caches):
    b, t, d = x.shape
    x2d = x.reshape(b * t, d)
    proj = _in_proj(x2d, shared["w_in"], layer).reshape(b, t, IN_PAD)

    start = 0 if caches is None else caches["ckv"].shape[1]
    cos2, sin2 = _rope_tables(start, t)
    q, ckv_new, krope_new = _mla_prep(proj, cos2, sin2, p["q_norm_g"], p["kv_norm_g"], p["wq_nope"], p["wq_rope"],
                                      p["wq_rope_sw"], p["wuk_t"])
    if caches is None:
        o_a = _mla_attn(q, ckv_new, krope_new, p["wuv"], 0)
        s_past = jnp.zeros((b, GDN_HEADS, GDN_DK, GDN_DV), F32)
        conv_past = jnp.zeros((b, GDN_CONV - 1, GDN_CONV_DIM), F32)
        o_c = _cb_attn(proj, None, None, layer, p["bias"])
    else:
        o_a = _mla_attn_cached(q, caches["ckv"], caches["krope"], layer, ckv_new, krope_new, p["wuv"])
        s_past, conv_past = caches["gdn"][layer], caches["conv"][layer]
        o_c = _cb_attn(proj, caches["cb_k"], caches["cb_v"], layer, p["bias"])
    o_b, s_new, conv_new = _gdn(proj, conv_past, p["conv_w"], p["alog"], p["dtb"], p["gnorm"], s_past)

    x1, ext = _out_ln1(o_a.reshape(b * t, -1), o_b.reshape(b * t, -1), o_c.reshape(b * t, -1), x2d,
                       p["wo_a"], p["wo_b"], p["wo_c"], p["ln1_g"], p["ln1_b"], shared["rwt"], shared["rb"])
    x2 = _moe_ln2(x1, ext, shared["w_gate"], shared["w_up"], shared["w_down"], layer, p["ln2_g"], p["ln2_b"])

    keep = min(CB_PAST_ROWS, t)
    cb_new = lambda col: proj[:, t - keep:, col:col + CB_DIM].reshape(b, keep, CB_HEADS, CB_DH)
    state = (ckv_new, krope_new, s_new, conv_new, cb_new(C_CB + CB_DIM), cb_new(C_CB + 2 * CB_DIM))
    return x2.reshape(b, t, d), state


def kernel(x_prompt, x_sample, cache_mla_ckv, cache_mla_krope, state_gdn, state_gdn_conv, cache_cb_k, cache_cb_v,
           w_in, q_norm_g, w_uq, kv_norm_g, w_uk, w_uv, conv_w, a_log, dt_bias, gdn_norm_g, rel_bias, w_out,
           ln1_g, ln1_b, router_w, router_b, w_gate, w_up, w_down, ln2_g, ln2_b):
    depth = w_in.shape[0]
    band_bias = _band_bias(rel_bias)
    layers = [_prep_layer(q_norm_g[l], w_uq[l], kv_norm_g[l], w_uk[l], w_uv[l], conv_w[l], a_log[l],
                          dt_bias[l], gdn_norm_g[l], band_bias[l], w_out[l], ln1_g[l], ln1_b[l],
                          ln2_g[l], ln2_b[l]) for l in range(depth)]
    stack = lambda w: w.astype(BF16).reshape((depth * N_EXPERTS,) + w.shape[2:])
    shared = dict(w_in=_w_in_relayout(w_in), rwt=router_w.T, rb=router_b[:, None], w_gate=stack(w_gate), w_up=stack(w_up), w_down=stack(w_down))
    merge = lambda a: a.reshape((a.shape[0] * a.shape[1],) + a.shape[2:])
    rows = cache_cb_k.shape[2]
    caches = dict(ckv=merge(cache_mla_ckv), krope=merge(cache_mla_krope), gdn=state_gdn, conv=state_gdn_conv,
                  cb_k=merge(cache_cb_k).reshape(-1, rows, CB_DIM), cb_v=merge(cache_cb_v).reshape(-1, rows, CB_DIM))

    def run_trunk(x, trunk_caches):
        new = ([], [], [], [], [], [])
        for l in range(depth):
            x, st = _layer(x, layers[l], l, shared, trunk_caches)
            for lst, a in zip(new, st):
                lst.append(a)
        return (x, *[jnp.stack(a) for a in new])

    outs_p = run_trunk(x_prompt, None)
    outs_s = run_trunk(x_sample, caches)
    return (outs_p[0], outs_s[0], *outs_p[1:], *outs_s[1:])
```

```python
import functools
import math

import jax
import jax.numpy as jnp
import numpy as np
from jax import lax
from jax.experimental import pallas as pl
from jax.experimental.pallas import tpu as pltpu

F32 = jnp.float32
BF16 = jnp.bfloat16

CHUNK = 64
MLA_HEADS = 6
MLA_Q_RANK = 512
MLA_KV_RANK = 256
MLA_NOPE = 128
MLA_ROPE = 64
MLA_V = 128
ROPE_THETA = 10000.0
GDN_HEADS = 6
GDN_DK = 128
GDN_DV = 128
GDN_CONV = 4
GDN_KEY_DIM = GDN_HEADS * GDN_DK
GDN_VAL_DIM = GDN_HEADS * GDN_DV
GDN_CONV_DIM = 2 * GDN_KEY_DIM + GDN_VAL_DIM
CB_HEADS = 4
CB_DH = 128
CB_DIM = CB_HEADS * CB_DH
CB_PAST_ROWS = 8 * CHUNK
REL_CLIP = 256
N_EXPERTS = 16
N_GROUPS = 4
EXPERTS_PER_GROUP = N_EXPERTS // N_GROUPS
DEPTH = 2
DEEPNORM_ALPHA = (2 * DEPTH) ** 0.25

LANES = 128
C_GQKV = 0
C_GZ = C_GQKV + GDN_CONV_DIM
C_CQ = C_GZ + GDN_VAL_DIM
C_CKV = C_CQ + MLA_Q_RANK
C_KR = C_CKV + MLA_KV_RANK
C_GBA = C_KR + 2 * MLA_ROPE
C_CB = C_GBA + LANES
IN_PAD = C_CB + 3 * CB_DIM

SLAB_ROWS = 2048 // LANES
EXT_SEL = 0
EXT_W_LO = N_EXPERTS

VMEM_LIMIT = 56 * 1024 * 1024


def _cparams(sem):
    return pltpu.CompilerParams(dimension_semantics=sem, vmem_limit_bytes=VMEM_LIMIT)


def _dot(a, b):
    return jnp.dot(a, b, preferred_element_type=F32)


def _dot_nt(a, b):
    return lax.dot_general(a, b, (((1,), (1,)), ((), ())), preferred_element_type=F32)


def _dot_hi(a, b):
    return jnp.dot(a, b, preferred_element_type=F32, precision=lax.Precision.HIGHEST)


def _dot_nt_hi(a, b):
    return lax.dot_general(a, b, (((1,), (1,)), ((), ())), preferred_element_type=F32,
                           precision=lax.Precision.HIGHEST)


def _sigmoid(x):
    return 1.0 / (1.0 + jnp.exp(-x))


def _silu(x):
    return x * _sigmoid(x)


def _layer_norm(h, g, b, eps=1e-5):
    mu = jnp.mean(h, -1, keepdims=True)
    d = h - mu
    var = jnp.mean(d * d, -1, keepdims=True)
    return d * lax.rsqrt(var + eps) * g + b


def _rms_norm(x, g, eps=1e-6):
    return x * lax.rsqrt(jnp.mean(x * x, -1, keepdims=True) + eps) * g


def _inproj_kernel(x_ref, w_ref, o_ref, xb_ref):
    @pl.when(pl.program_id(1) == 0)
    def _():
        xb_ref[...] = x_ref[...].astype(BF16)

    o_ref[...] = _dot(xb_ref[...], w_ref[0])


def _in_proj(x2d, w, layer):
    n, d = x2d.shape
    width = w.shape[2]
    tm = min(1024, n)
    tn = 512
    return pl.pallas_call(
        _inproj_kernel,
        grid=(n // tm, width // tn),
        in_specs=[pl.BlockSpec((tm, d), lambda i, j: (i, 0)),
                  pl.BlockSpec((1, d, tn), lambda i, j: (layer, 0, j))],
        out_specs=pl.BlockSpec((tm, tn), lambda i, j: (i, j)),
        out_shape=jax.ShapeDtypeStruct((n, width), F32),
        scratch_shapes=[pltpu.VMEM((tm, d), BF16)],
        compiler_params=_cparams(("parallel", "arbitrary")),
        name="in_proj",
    )(x2d, w)


def _mla_prep_kernel(cq_ref, ckv_ref, kr_ref, cosq_ref, sinq_ref, cosk_ref, sink_ref,
                     qg_ref, kvg_ref, wqn_ref, wqr_ref, wqs_ref, wuk_ref,
                     q_ref, ckvn_ref, krn_ref):
    cqn = _rms_norm(cq_ref[0], qg_ref[...]).astype(BF16)
    q_nope = _dot(cqn, wqn_ref[...]).astype(BF16)
    q_rope = _dot(cqn, wqr_ref[...])
    q_rope_sw = _dot(cqn, wqs_ref[...])
    q_rot = (q_rope * cosq_ref[...] + q_rope_sw * sinq_ref[...]).astype(BF16)
    for h in range(MLA_HEADS):
        q_lat = _dot(q_nope[:, h * MLA_NOPE:(h + 1) * MLA_NOPE], wuk_ref[h])
        q_ref[0, h, :, 0:MLA_KV_RANK] = q_lat.astype(BF16)
        q_ref[0, h, :, MLA_KV_RANK:] = q_rot[:, h * MLA_ROPE:(h + 1) * MLA_ROPE]
    ckvn_ref[0] = _rms_norm(ckv_ref[0], kvg_ref[...])
    kr = kr_ref[0]
    krn_ref[0] = kr[:, :MLA_ROPE] * cosk_ref[...] + kr[:, MLA_ROPE:] * sink_ref[...]


def _mla_prep(proj, cos2, sin2, q_norm_g, kv_norm_g, wq_nope, wq_rope, wq_rope_sw, wuk_t):
    b, t, _ = proj.shape
    tm = min(512, t)
    cosq = jnp.tile(cos2, (1, MLA_HEADS))
    sinq = jnp.tile(sin2, (1, MLA_HEADS))
    full = lambda a: pl.BlockSpec(a.shape, lambda bi, i: (0,) * a.ndim)
    row = lambda w: pl.BlockSpec((tm, w), lambda bi, i: (i, 0))
    qd = MLA_KV_RANK + MLA_ROPE
    return pl.pallas_call(
        _mla_prep_kernel,
        grid=(b, t // tm),
        in_specs=[pl.BlockSpec((1, tm, MLA_Q_RANK), lambda bi, i: (bi, i, C_CQ // MLA_Q_RANK)),
                  pl.BlockSpec((1, tm, MLA_KV_RANK), lambda bi, i: (bi, i, C_CKV // MLA_KV_RANK)),
                  pl.BlockSpec((1, tm, 2 * MLA_ROPE), lambda bi, i: (bi, i, C_KR // (2 * MLA_ROPE))),
                  row(MLA_HEADS * MLA_ROPE), row(MLA_HEADS * MLA_ROPE), row(MLA_ROPE), row(MLA_ROPE),
                  full(q_norm_g), full(kv_norm_g), full(wq_nope), full(wq_rope), full(wq_rope_sw),
                  full(wuk_t)],
        out_specs=[pl.BlockSpec((1, MLA_HEADS, tm, qd), lambda bi, i: (bi, 0, i, 0)),
                   pl.BlockSpec((1, tm, MLA_KV_RANK), lambda bi, i: (bi, i, 0)),
                   pl.BlockSpec((1, tm, MLA_ROPE), lambda bi, i: (bi, i, 0))],
        out_shape=[jax.ShapeDtypeStruct((b, MLA_HEADS, t, qd), BF16),
                   jax.ShapeDtypeStruct((b, t, MLA_KV_RANK), F32),
                   jax.ShapeDtypeStruct((b, t, MLA_ROPE), F32)],
        compiler_params=_cparams(("parallel", "parallel")),
        name="mla_prep",
    )(proj, proj, proj, cosq, sinq, cos2, sin2, q_norm_g, kv_norm_g, wq_nope, wq_rope, wq_rope_sw, wuk_t)


def _mla_attn_kernel(qi_ref, kj_ref, q_ref, ckv_ref, kr_ref, wuv_ref, o_ref, m_ref, l_ref, acc_ref, *, start, tq, tk):
    step = pl.program_id(1)
    i = qi_ref[step]
    j = kj_ref[step]

    @pl.when(j == 0)
    def _():
        m_ref[...] = jnp.full(m_ref.shape, -jnp.inf, F32)
        l_ref[...] = jnp.zeros(l_ref.shape, F32)
        acc_ref[...] = jnp.zeros(acc_ref.shape, F32)

    ckv = ckv_ref[0].astype(BF16)
    kr = kr_ref[0].astype(BF16)
    scale = (MLA_NOPE + MLA_ROPE) ** -0.5

    hpg = 1 if tq >= 256 else MLA_HEADS
    grows = hpg * tq

    def update(masked):
        if masked:
            q_pos = start + i * tq + lax.broadcasted_iota(jnp.int32, (grows, 1), 0) % tq
            k_pos = j * tk + lax.broadcasted_iota(jnp.int32, (1, tk), 1)
            visible = k_pos // CHUNK <= q_pos // CHUNK
        def scores(g):
            q = q_ref[0, g] if hpg == 1 else q_ref[0].reshape(grows, MLA_KV_RANK + MLA_ROPE)
            return _dot_nt(q[:, :MLA_KV_RANK], ckv) + _dot_nt(q[:, MLA_KV_RANK:], kr)

        groups = MLA_HEADS // hpg
        s_next = scores(0)
        for g in range(groups):
            s, s_next = s_next, (scores(g + 1) if g + 1 < groups else None)
            s = s * scale
            if masked:
                s = jnp.where(visible, s, -jnp.inf)
            rows = slice(g * grows, (g + 1) * grows)
            m_old = m_ref[rows]
            m_new = jnp.maximum(m_old, jnp.max(s, -1, keepdims=True))
            alpha = jnp.exp(m_old - m_new)
            p = jnp.exp(s - m_new)
            l_ref[rows] = alpha * l_ref[rows] + jnp.sum(p, -1, keepdims=True)
            acc_ref[rows] = alpha * acc_ref[rows] + _dot(p.astype(BF16), ckv)
            m_ref[rows] = m_new

    needs_mask = (j * tk + tk - 1) // CHUNK > (start + i * tq) // CHUNK
    pl.when(needs_mask)(lambda: update(True))
    pl.when(jnp.logical_not(needs_mask))(lambda: update(False))

    @pl.when(j == _last_kv_block(start, i, tq, tk))
    def _():
        o_lat = (acc_ref[...] / l_ref[...]).astype(BF16)
        for h in range(MLA_HEADS):
            o_ref[0, :, h * MLA_V:(h + 1) * MLA_V] = _dot(o_lat[h * tq:(h + 1) * tq], wuv_ref[h]).astype(BF16)


MLA_SUB_KEYS = 512


def _mla_attn_cached_kernel(q_ref, pckv_ref, pkr_ref, nckv_ref, nkr_ref, wuv_ref, o_ref, m_ref, l_ref, acc_ref, *, nb):
    j = pl.program_id(1)
    tq = nckv_ref.shape[1]
    rows = MLA_HEADS * tq

    @pl.when(j == 0)
    def _():
        m_ref[...] = jnp.full(m_ref.shape, -jnp.inf, F32)
        l_ref[...] = jnp.zeros(l_ref.shape, F32)
        acc_ref[...] = jnp.zeros(acc_ref.shape, F32)

    def update(ckv_ref, kr_ref):
        keys = ckv_ref.shape[1]
        sub = MLA_SUB_KEYS if keys % MLA_SUB_KEYS == 0 else keys
        q = q_ref[0].reshape(rows, MLA_KV_RANK + MLA_ROPE)

        def scores(u):
            ckv = ckv_ref[0, u * sub:(u + 1) * sub, :].astype(BF16)
            kr = kr_ref[0, u * sub:(u + 1) * sub, :].astype(BF16)
            return _dot_nt(q[:, :MLA_KV_RANK], ckv) + _dot_nt(q[:, MLA_KV_RANK:], kr), ckv

        nxt = scores(0)
        for u in range(keys // sub):
            (s, ckv), nxt = nxt, (scores(u + 1) if (u + 1) * sub < keys else None)
            s = s * ((MLA_NOPE + MLA_ROPE) ** -0.5)
            m_old = m_ref[...]
            m_new = jnp.maximum(m_old, jnp.max(s, -1, keepdims=True))
            alpha = jnp.exp(m_old - m_new)
            p = jnp.exp(s - m_new)
            l_ref[...] = alpha * l_ref[...] + jnp.sum(p, -1, keepdims=True)
            acc_ref[...] = alpha * acc_ref[...] + _dot(p.astype(BF16), ckv)
            m_ref[...] = m_new

    pl.when(j < nb)(lambda: update(pckv_ref, pkr_ref))

    @pl.when(j == nb)
    def _():
        update(nckv_ref, nkr_ref)
        o_lat = (acc_ref[...] / l_ref[...]).astype(BF16)
        for h in range(MLA_HEADS):
            o_ref[0, :, h * MLA_V:(h + 1) * MLA_V] = _dot(o_lat[h * tq:(h + 1) * tq], wuv_ref[h]).astype(BF16)


def _mla_attn_cached(q, ckv_cache, kr_cache, layer, ckv_new, kr_new, wuv):
    b, _, t, qd = q.shape
    past = ckv_cache.shape[1]
    assert t == CHUNK and past % CHUNK == 0
    tk = next((c for c in (2048, 1024) if past % c == 0), None) or _pick_tk(past)
    nb = past // tk
    past_map = lambda bi, j: (layer * b + bi, jnp.minimum(j, nb - 1), 0)
    new_map = lambda bi, j: (bi, 0, 0)
    return pl.pallas_call(
        functools.partial(_mla_attn_cached_kernel, nb=nb),
        grid=(b, nb + 1),
        in_specs=[pl.BlockSpec((1, MLA_HEADS, t, qd), lambda bi, j: (bi, 0, 0, 0)),
                  pl.BlockSpec((1, tk, MLA_KV_RANK), past_map),
                  pl.BlockSpec((1, tk, MLA_ROPE), past_map),
                  pl.BlockSpec((1, t, MLA_KV_RANK), new_map),
                  pl.BlockSpec((1, t, MLA_ROPE), new_map),
                  pl.BlockSpec(wuv.shape, lambda bi, j: (0, 0, 0))],
        out_specs=pl.BlockSpec((1, t, MLA_HEADS * MLA_V), lambda bi, j: (bi, 0, 0)),
        out_shape=jax.ShapeDtypeStruct((b, t, MLA_HEADS * MLA_V), BF16),
        scratch_shapes=[pltpu.VMEM((MLA_HEADS * t, 1), F32),
                        pltpu.VMEM((MLA_HEADS * t, 1), F32),
                        pltpu.VMEM((MLA_HEADS * t, MLA_KV_RANK), F32)],
        compiler_params=_cparams(("parallel", "arbitrary")),
        name="mla_attn_cached",
    )(q, ckv_cache, kr_cache, ckv_new, kr_new, wuv)


def _last_kv_block(start, i, tq, tk):
    return ((start + i * tq + tq - 1) // CHUNK * CHUNK) // tk


def _pick_tk(s):
    for cand in (512, 1024, 832, 768, 640, 576, 448, 384, 320, 256, 192, 128, 64):
        if s % cand == 0:
            return cand
    raise ValueError(f"unsupported key length {s}")


def _mla_attn(q, ckv_all, kr_all, wuv, start):
    b, _, t, qd = q.shape
    s = ckv_all.shape[1]
    tq = min(256, t)
    tk = _pick_tk(s)
    assert tk % CHUNK == 0 and s % tk == 0 and t % tq == 0
    pairs = [(i, j) for i in range(t // tq) for j in range(_last_kv_block(start, i, tq, tk) + 1)]
    qi = jnp.asarray(np.array([p[0] for p in pairs], np.int32))
    kj = jnp.asarray(np.array([p[1] for p in pairs], np.int32))
    kv_map = lambda bi, st, qi_r, kj_r: (bi, kj_r[st], 0)
    return pl.pallas_call(
        functools.partial(_mla_attn_kernel, start=start, tq=tq, tk=tk),
        grid_spec=pltpu.PrefetchScalarGridSpec(
            num_scalar_prefetch=2, grid=(b, len(pairs)),
            in_specs=[pl.BlockSpec((1, MLA_HEADS, tq, qd), lambda bi, st, qi_r, kj_r: (bi, 0, qi_r[st], 0)),
                      pl.BlockSpec((1, tk, MLA_KV_RANK), kv_map),
                      pl.BlockSpec((1, tk, MLA_ROPE), kv_map),
                      pl.BlockSpec(wuv.shape, lambda bi, st, qi_r, kj_r: (0, 0, 0))],
            out_specs=pl.BlockSpec((1, tq, MLA_HEADS * MLA_V), lambda bi, st, qi_r, kj_r: (bi, qi_r[st], 0)),
            scratch_shapes=[pltpu.VMEM((MLA_HEADS * tq, 1), F32),
                            pltpu.VMEM((MLA_HEADS * tq, 1), F32),
                            pltpu.VMEM((MLA_HEADS * tq, MLA_KV_RANK), F32)]),
        out_shape=jax.ShapeDtypeStruct((b, t, MLA_HEADS * MLA_V), BF16),
        compiler_params=_cparams(("parallel", "arbitrary")),
        name="mla_attn",
    )(qi, kj, q, ckv_all, kr_all, wuv)


def _split_bf16(x):
    hi = x.astype(BF16)
    return hi, (x - hi.astype(F32)).astype(BF16)


def _unit_lower_solve_many(a_list, rhs_list):
    n = rhs_list[0].shape[1]
    levels = int(math.log2(CHUNK))
    xs, ps = list(rhs_list), list(a_list)
    for lvl in range(levels):
        for h in range(len(xs)):
            p_hi, p_lo = _split_bf16(ps[h])
            lhs = jnp.concatenate([p_hi, p_hi, p_lo], 1)
            if lvl < levels - 1:
                r_hi, r_lo = _split_bf16(jnp.concatenate([xs[h], ps[h]], 1))
                both = _dot(lhs, jnp.concatenate([r_hi, r_lo, r_hi], 0))
                px, ps[h] = both[:, :n], both[:, n:]
            else:
                r_hi, r_lo = _split_bf16(xs[h])
                px = _dot(lhs, jnp.concatenate([r_hi, r_lo, r_hi], 0))
            xs[h] = xs[h] - px if lvl == 0 else xs[h] + px
    return xs


def _gdn_kernel(qkv_ref, gz_ref, gba_ref, convp_ref, convw_ref, alog_ref, dtb_ref, gnorm_ref, s0_ref,
                ob_ref, snew_ref, convn_ref, s_scr, ext_scr, *, n_chunks):
    n = pl.program_id(1)
    L = CHUNK
    tail = 8

    @pl.when(n == 0)
    def _():
        s_scr[...] = s0_ref[0]
        ext_scr[0:tail, :] = jnp.zeros((tail, GDN_CONV_DIM), F32)
        ext_scr[tail - (GDN_CONV - 1):tail, :] = convp_ref[0]

    cur = qkv_ref[0]
    ext_scr[tail:tail + L, :] = cur
    w = convw_ref[...]
    conv = ext_scr[tail - 3:tail - 3 + L, :] * w[0:1]
    conv = conv + ext_scr[tail - 2:tail - 2 + L, :] * w[1:2]
    conv = conv + ext_scr[tail - 1:tail - 1 + L, :] * w[2:3]
    conv = conv + cur * w[3:4]
    conv = _silu(conv)

    @pl.when(n == n_chunks - 1)
    def _():
        convn_ref[0] = ext_scr[tail + L - (GDN_CONV - 1):tail + L, :]

    ext_scr[0:tail, :] = cur[L - tail:, :]

    gba = gba_ref[0]
    beta_all = _sigmoid(gba)
    z = gba + dtb_ref[...]
    softplus = jnp.maximum(z, 0.0) + jnp.log1p(jnp.exp(-jnp.abs(z)))
    g_all = -jnp.exp(alog_ref[...]) * softplus
    ri = lax.broadcasted_iota(jnp.int32, (L, L), 0)
    ci = lax.broadcasted_iota(jnp.int32, (L, L), 1)
    incl = ci <= ri
    strict = ci < ri
    g_cum = _dot_hi(incl.astype(F32), g_all)
    g_cum_t = g_cum.T

    def l2n(x):
        return x * lax.rsqrt(jnp.sum(x * x, -1, keepdims=True) + 1e-6)

    heads = range(GDN_HEADS)
    q = [l2n(conv[:, h * GDN_DK:(h + 1) * GDN_DK]) * (GDN_DK ** -0.5) for h in heads]
    k = [l2n(conv[:, GDN_KEY_DIM + h * GDN_DK:GDN_KEY_DIM + (h + 1) * GDN_DK]) for h in heads]
    v = [conv[:, 2 * GDN_KEY_DIM + h * GDN_DV:2 * GDN_KEY_DIM + (h + 1) * GDN_DV] for h in heads]
    beta = [beta_all[:, h:h + 1] for h in heads]
    gc = [g_cum[:, GDN_HEADS + h:GDN_HEADS + h + 1] for h in heads]
    dmat = [jnp.exp(jnp.where(incl, gc[h] - g_cum_t[GDN_HEADS + h:GDN_HEADS + h + 1, :], -jnp.inf)) for h in heads]
    eg = [jnp.exp(gc[h]) for h in heads]
    qk_kk = []
    for h in heads:
        kb = k[h].astype(BF16)
        qk_kk.append(_dot_nt(jnp.concatenate([q[h].astype(BF16), kb], 0), kb))
    a_mat = [jnp.where(strict, beta[h] * qk_kk[h][L:] * dmat[h], 0.0) for h in heads]
    rhs = [jnp.concatenate([beta[h] * v[h], (beta[h] * eg[h]) * k[h]], -1) for h in heads]
    sol = _unit_lower_solve_many(a_mat, rhs)
    s_old = [s_scr[h] for h in heads]
    wq_s = [_dot(jnp.concatenate([sol[h][:, GDN_DV:], q[h]], 0).astype(BF16), s_old[h].astype(BF16)) for h in heads]
    upd = []
    for h in heads:
        delta = sol[h][:, :GDN_DV] - wq_s[h][:L]
        kd = k[h] * jnp.exp(gc[h][L - 1:L, :] - gc[h])
        lhs = jnp.concatenate([qk_kk[h][:L] * dmat[h], kd.T], 0)
        upd.append(_dot(lhs.astype(BF16), delta.astype(BF16)))
    for h in heads:
        o = eg[h] * wq_s[h][L:] + upd[h][:L]
        s_scr[h] = jnp.exp(gc[h][L - 1:L, :]) * s_old[h] + upd[h][L:]
        gz = gz_ref[0, :, h * GDN_DV:(h + 1) * GDN_DV]
        ob_ref[0, :, h * GDN_DV:(h + 1) * GDN_DV] = (_rms_norm(o, gnorm_ref[...]) * _silu(gz)).astype(BF16)

    @pl.when(n == n_chunks - 1)
    def _():
        snew_ref[0] = s_scr[...]


def _gdn(proj, conv_past, conv_w, alog128, dtb128, gnorm, s0):
    b, t, _ = proj.shape
    L = CHUNK
    full = lambda a: pl.BlockSpec(a.shape, lambda bi, n: (0,) * a.ndim)
    return pl.pallas_call(
        functools.partial(_gdn_kernel, n_chunks=t // L),
        grid=(b, t // L),
        in_specs=[pl.BlockSpec((1, L, GDN_CONV_DIM), lambda bi, n: (bi, n, C_GQKV // GDN_CONV_DIM)),
                  pl.BlockSpec((1, L, GDN_VAL_DIM), lambda bi, n: (bi, n, C_GZ // GDN_VAL_DIM)),
                  pl.BlockSpec((1, L, LANES), lambda bi, n: (bi, n, C_GBA // LANES)),
                  pl.BlockSpec((1, GDN_CONV - 1, GDN_CONV_DIM), lambda bi, n: (bi, 0, 0)),
                  full(conv_w), full(alog128), full(dtb128), full(gnorm),
                  pl.BlockSpec((1, GDN_HEADS, GDN_DK, GDN_DV), lambda bi, n: (bi, 0, 0, 0))],
        out_specs=[pl.BlockSpec((1, L, GDN_VAL_DIM), lambda bi, n: (bi, n, 0)),
                   pl.BlockSpec((1, GDN_HEADS, GDN_DK, GDN_DV), lambda bi, n: (bi, 0, 0, 0)),
                   pl.BlockSpec((1, GDN_CONV - 1, GDN_CONV_DIM), lambda bi, n: (bi, 0, 0))],
        out_shape=[jax.ShapeDtypeStruct((b, t, GDN_VAL_DIM), BF16),
                   jax.ShapeDtypeStruct((b, GDN_HEADS, GDN_DK, GDN_DV), F32),
                   jax.ShapeDtypeStruct((b, GDN_CONV - 1, GDN_CONV_DIM), F32)],
        scratch_shapes=[pltpu.VMEM((GDN_HEADS, GDN_DK, GDN_DV), F32),
                        pltpu.VMEM((8 + L, GDN_CONV_DIM), F32)],
        compiler_params=_cparams(("parallel", "arbitrary")),
        name="gdn",
    )(proj, proj, proj, conv_past, conv_w, alog128, dtb128, gnorm, s0)


def _cb_attn_kernel(q_ref, kprev_ref, kcur_ref, vprev_ref, vcur_ref, bias_ref, o_ref, *, tq, pad):
    i = pl.program_id(1)
    L = CHUNK
    width = CB_PAST_ROWS + L
    scale = CB_DH ** -0.5
    for c in range(tq // L):
        lo = c * L
        kwin = jnp.concatenate([kprev_ref[0, lo:, :], kcur_ref[0, :lo + L, :]], 0).astype(BF16)
        vwin = jnp.concatenate([vprev_ref[0, lo:, :], vcur_ref[0, :lo + L, :]], 0).astype(BF16)
        q = q_ref[0, lo:lo + L, :].astype(BF16)
        row = i * tq + lo + lax.broadcasted_iota(jnp.int32, (1, width), 1)
        valid = row >= pad
        heads = [slice(h * CB_DH, (h + 1) * CB_DH) for h in range(CB_HEADS)]
        scores = [_dot_nt(q[:, hs], kwin[:, hs]) for hs in heads]
        probs = []
        for h in range(CB_HEADS):
            s = scores[h] * scale + bias_ref[h]
            s = jnp.where(valid, s, -jnp.inf)
            m = jnp.max(s, -1, keepdims=True)
            p = jnp.exp(s - m)
            probs.append((p / jnp.sum(p, -1, keepdims=True)).astype(BF16))
        for h, hs in enumerate(heads):
            o_ref[0, lo:lo + L, hs] = _dot(probs[h], vwin[:, hs]).astype(BF16)


def _cb_attn(proj, k_cache, v_cache, layer, bias):
    b, t, _ = proj.shape
    tq = min(CB_PAST_ROWS, t)
    kcol, vcol = C_CB // CB_DIM + 1, C_CB // CB_DIM + 2
    cur = lambda col: pl.BlockSpec((1, tq, CB_DIM), lambda bi, i: (bi, i, col))
    if k_cache is None:
        assert tq == CB_PAST_ROWS and t % tq == 0
        pad = CB_PAST_ROWS
        prev = lambda col: pl.BlockSpec((1, tq, CB_DIM), lambda bi, i: (bi, jnp.maximum(i - 1, 0), col))
        k_prev_arr, v_prev_arr, k_prev, v_prev = proj, proj, prev(kcol), prev(vcol)
    else:
        assert t == tq and k_cache.shape[1] == CB_PAST_ROWS
        pad = 0
        cache_spec = pl.BlockSpec((1, CB_PAST_ROWS, CB_DIM), lambda bi, i: (layer * b + bi, 0, 0))
        k_prev_arr, v_prev_arr, k_prev, v_prev = k_cache, v_cache, cache_spec, cache_spec
    return pl.pallas_call(
        functools.partial(_cb_attn_kernel, tq=tq, pad=pad),
        grid=(b, t // tq),
        in_specs=[cur(C_CB // CB_DIM), k_prev, cur(kcol), v_prev, cur(vcol),
                  pl.BlockSpec(bias.shape, lambda bi, i: (0, 0, 0))],
        out_specs=pl.BlockSpec((1, tq, CB_DIM), lambda bi, i: (bi, i, 0)),
        out_shape=jax.ShapeDtypeStruct((b, t, CB_DIM), BF16),
        compiler_params=_cparams(("parallel", "parallel")),
        name="cb_attn",
    )(proj, k_prev_arr, proj, v_prev_arr, proj, bias)


def _route(logits_t, rb):
    s = _sigmoid(logits_t)
    sb = s + rb
    rows = [sb[e:e + 1, :] for e in range(N_EXPERTS)]
    grp = []
    for g in range(N_GROUPS):
        r = rows[g * EXPERTS_PER_GROUP:(g + 1) * EXPERTS_PER_GROUP]
        best = None
        for a in range(EXPERTS_PER_GROUP):
            for c in range(a + 1, EXPERTS_PER_GROUP):
                pair = r[a] + r[c]
                best = pair if best is None else jnp.maximum(best, pair)
        grp.append(best)
    gmax = functools.reduce(jnp.maximum, grp)
    gsel = jnp.full(gmax.shape, N_GROUPS, jnp.int32)
    for g in reversed(range(N_GROUPS)):
        gsel = jnp.where(grp[g] == gmax, g, gsel)
    sel = []
    for e in range(N_EXPERTS):
        g = e // EXPERTS_PER_GROUP
        rank = jnp.zeros(gmax.shape, jnp.int32)
        for e2 in range(g * EXPERTS_PER_GROUP, (g + 1) * EXPERTS_PER_GROUP):
            if e2 == e:
                continue
            ahead = (rows[e2] >= rows[e]) if e2 < e else (rows[e2] > rows[e])
            rank = rank + ahead.astype(jnp.int32)
        sel.append(jnp.where((gsel == g) & (rank < 2), 1.0, 0.0))
    ssum = functools.reduce(lambda a, c: a + c, [sel[e] * s[e:e + 1, :] for e in range(N_EXPERTS)])
    zero = jnp.zeros(gmax.shape, F32)
    seen, w_lo, w_hi, e_lo, e_hi = zero, zero, zero, zero, zero
    for e in range(N_EXPERTS):
        gate_e = sel[e] * s[e:e + 1, :] / ssum
        first = sel[e] * jnp.where(seen == 0.0, 1.0, 0.0)
        second = sel[e] - first
        w_lo, w_hi = w_lo + first * gate_e, w_hi + second * gate_e
        e_lo, e_hi = e_lo + first * e, e_hi + second * e
        seen = seen + sel[e]
    return sel, w_lo, w_hi, e_lo, e_hi


def _to_slab(ref, val, tok0=0):
    tm = val.shape[0]
    for c in range(val.shape[1] // LANES):
        ref[pl.ds(tok0 * SLAB_ROWS + c, tm, stride=SLAB_ROWS), :] = val[:, c * LANES:(c + 1) * LANES]


def _out_ln1_kernel(oa_ref, ob_ref, oc_ref, x_ref, wa_ref, wb_ref, wc_ref, g_ref, b_ref, rwt_ref, rb_ref,
                    x1_ref, ext_ref):
    tm = x_ref.shape[0]
    halves = [slice(k * (tm // 2), (k + 1) * (tm // 2)) for k in range(2)]
    ys = [_dot(oa_ref[r, :], wa_ref[...]) + _dot(ob_ref[r, :], wb_ref[...]) + _dot(oc_ref[r, :], wc_ref[...])
          for r in halves]
    for r, y in zip(halves, ys):
        x1 = _layer_norm(DEEPNORM_ALPHA * x_ref[r, :] + y, g_ref[...], b_ref[...])
        _to_slab(x1_ref, x1, r.start)
        logits_t = _dot_nt_hi(rwt_ref[...], x1)
        sel, w_lo, w_hi, e_lo, e_hi = _route(logits_t, rb_ref[...])
        rows = sel + [w_lo, w_hi, e_lo, e_hi]
        ext = jnp.concatenate(rows + [jnp.zeros((LANES - len(rows), x1.shape[0]), F32)], 0)
        ext_ref[r, :] = ext.T


def _out_ln1(oa, ob, oc, x2d, wa, wb, wc, g, bb, rwt, rb):
    n, d = x2d.shape
    tm = min(256, n)
    full = lambda a: pl.BlockSpec(a.shape, lambda i: (0,) * a.ndim)
    row = lambda w: pl.BlockSpec((tm, w), lambda i: (i, 0))
    return pl.pallas_call(
        _out_ln1_kernel,
        grid=(n // tm,),
        in_specs=[row(oa.shape[1]), row(ob.shape[1]), row(oc.shape[1]), row(d),
                  full(wa), full(wb), full(wc), full(g), full(bb), full(rwt), full(rb)],
        out_specs=[pl.BlockSpec((tm * SLAB_ROWS, LANES), lambda i: (i, 0)), row(LANES)],
        out_shape=[jax.ShapeDtypeStruct((n * SLAB_ROWS, LANES), F32), jax.ShapeDtypeStruct((n, LANES), F32)],
        compiler_params=_cparams(("parallel",)),
        name="out_ln1",
    )(oa, ob, oc, x2d, wa, wb, wc, g, bb, rwt, rb)


def _row_copy(src_hbm, src_row, dst, dst_row, sem):
    return pltpu.make_async_copy(src_hbm.at[pl.ds(pl.multiple_of(src_row * SLAB_ROWS, SLAB_ROWS), SLAB_ROWS)],
                                 dst.at[pl.ds(pl.multiple_of(dst_row * SLAB_ROWS, SLAB_ROWS), SLAB_ROWS)], sem)


def _moe_ffn_kernel(tile_e_ref, nact_ref, src_ref, x_hbm, wg_ref, wu_ref, wd_ref, ys_ref,
                    xg0_ref, xg1_ref, xb_ref, acc_ref, sem):
    i = pl.program_id(0)
    f = pl.program_id(1)
    tm = xb_ref.shape[0]
    bufs = (xg0_ref, xg1_ref)

    def gather_start(tile, slot):
        base = tile * tm

        def issue(r, c):
            _row_copy(x_hbm, src_ref[base + r], bufs[slot], r, sem.at[slot]).start()
            return c

        lax.fori_loop(0, tm, issue, 0, unroll=8)

    def gather_wait(slot):
        pltpu.make_async_copy(x_hbm.at[pl.ds(0, tm * SLAB_ROWS)], bufs[slot], sem.at[slot]).wait()

    @pl.when(i < nact_ref[0])
    def _():
        for slot in range(2):
            @pl.when((f == 0) & (i % 2 == slot))
            def _():
                @pl.when(i == 0)
                def _():
                    gather_start(0, slot)

                @pl.when(i + 1 < nact_ref[0])
                def _():
                    gather_start(i + 1, 1 - slot)

                gather_wait(slot)
                for c in range(xb_ref.shape[1] // LANES):
                    xb_ref[:, c * LANES:(c + 1) * LANES] = bufs[slot][pl.ds(c, tm, stride=SLAB_ROWS), :].astype(BF16)

        xb = xb_ref[...]
        h = (_silu(_dot(xb, wg_ref[0].astype(BF16))) * _dot(xb, wu_ref[0].astype(BF16))).astype(BF16)
        y = _dot(h, wd_ref[0].astype(BF16))

        @pl.when(f == 0)
        def _():
            acc_ref[...] = y

        @pl.when(f > 0)
        def _():
            acc_ref[...] += y

        @pl.when(f == pl.num_programs(1) - 1)
        def _():
            _to_slab(ys_ref, acc_ref[...])

    @pl.when((i >= nact_ref[0]) & (f == pl.num_programs(1) - 1))
    def _():
        ys_ref[...] = jnp.zeros(ys_ref.shape, F32)


def _moe_ffn(tile_e, nact, src, x_slab, wg, wu, wd, tm, layer):
    n_rows = src.shape[0]
    _, d, ff = wg.shape
    tf = min(512, ff)
    nf = ff // tf
    fe = lambda i, f, na: jnp.where(i < na[0], f, nf - 1)
    ex = lambda i, te: layer * N_EXPERTS + te[i]
    return pl.pallas_call(
        _moe_ffn_kernel,
        grid_spec=pltpu.PrefetchScalarGridSpec(
            num_scalar_prefetch=3, grid=(n_rows // tm, nf),
            in_specs=[pl.BlockSpec(memory_space=pl.ANY),
                      pl.BlockSpec((1, d, tf), lambda i, f, te, na, sr: (ex(i, te), 0, fe(i, f, na))),
                      pl.BlockSpec((1, d, tf), lambda i, f, te, na, sr: (ex(i, te), 0, fe(i, f, na))),
                      pl.BlockSpec((1, tf, d), lambda i, f, te, na, sr: (ex(i, te), fe(i, f, na), 0))],
            out_specs=pl.BlockSpec((tm * SLAB_ROWS, LANES), lambda i, f, te, na, sr: (i, 0)),
            scratch_shapes=[pltpu.VMEM((tm * SLAB_ROWS, LANES), F32), pltpu.VMEM((tm * SLAB_ROWS, LANES), F32),
                            pltpu.VMEM((tm, d), BF16), pltpu.VMEM((tm, d), F32), pltpu.SemaphoreType.DMA((2,))]),
        out_shape=jax.ShapeDtypeStruct((n_rows * SLAB_ROWS, LANES), F32),
        compiler_params=_cparams(("arbitrary", "arbitrary")),
        name="moe_ffn",
    )(tile_e, nact, src, x_slab, wg, wu, wd)


def _moe_combine_kernel(pos0_ref, pos1_ref, x1_ref, ext_ref, ys_hbm, g_ref, b_ref, o_ref,
                        buf0_ref, buf1_ref, h_ref, sem):
    tm = o_ref.shape[0]
    base = pl.program_id(0) * tm

    def issue(r, c):
        _row_copy(ys_hbm, pos0_ref[base + r], buf0_ref, r, sem.at[0]).start()
        _row_copy(ys_hbm, pos1_ref[base + r], buf1_ref, r, sem.at[1]).start()
        return c

    lax.fori_loop(0, tm, issue, 0, unroll=8)
    pltpu.make_async_copy(ys_hbm.at[pl.ds(0, tm * SLAB_ROWS)], buf0_ref, sem.at[0]).wait()
    pltpu.make_async_copy(ys_hbm.at[pl.ds(0, tm * SLAB_ROWS)], buf1_ref, sem.at[1]).wait()

    ext = ext_ref[...]
    w_lo = ext[:, EXT_W_LO:EXT_W_LO + 1]
    w_hi = ext[:, EXT_W_LO + 1:EXT_W_LO + 2]
    for c in range(o_ref.shape[1] // LANES):
        rows = pl.ds(c, tm, stride=SLAB_ROWS)
        y = w_lo * buf0_ref[rows, :] + w_hi * buf1_ref[rows, :]
        h_ref[:, c * LANES:(c + 1) * LANES] = DEEPNORM_ALPHA * x1_ref[rows, :] + y
    o_ref[...] = _layer_norm(h_ref[...], g_ref[...], b_ref[...])


def _moe_combine(pos0, pos1, x1_slab, ext, ys, g, bb):
    n = ext.shape[0]
    d = g.shape[1]
    tm = min(512, n)
    return pl.pallas_call(
        _moe_combine_kernel,
        grid_spec=pltpu.PrefetchScalarGridSpec(
            num_scalar_prefetch=2, grid=(n // tm,),
            in_specs=[pl.BlockSpec((tm * SLAB_ROWS, LANES), lambda i, p0, p1: (i, 0)),
                      pl.BlockSpec((tm, LANES), lambda i, p0, p1: (i, 0)),
                      pl.BlockSpec(memory_space=pl.ANY),
                      pl.BlockSpec(g.shape, lambda i, p0, p1: (0, 0)),
                      pl.BlockSpec(bb.shape, lambda i, p0, p1: (0, 0))],
            out_specs=pl.BlockSpec((tm, d), lambda i, p0, p1: (i, 0)),
            scratch_shapes=[pltpu.VMEM((tm * SLAB_ROWS, LANES), F32), pltpu.VMEM((tm * SLAB_ROWS, LANES), F32),
                            pltpu.VMEM((tm, d), F32), pltpu.SemaphoreType.DMA((2,))]),
        out_shape=jax.ShapeDtypeStruct((n, d), F32),
        compiler_params=_cparams(("arbitrary",)),
        name="moe_combine",
    )(pos0, pos1, x1_slab, ext, ys, g, bb)


def _moe_src_kernel(pos0_ref, pos1_ref, src_ref, *, n, n_rows):
    def clear(p, c):
        src_ref[p] = 0
        return c

    lax.fori_loop(0, n_rows, clear, 0, unroll=8)

    def put(t, c):
        src_ref[pos0_ref[t]] = t
        src_ref[pos1_ref[t]] = t
        return c

    lax.fori_loop(0, n, put, 0, unroll=8)


def _moe_src(pos0, pos1, n_rows):
    n = pos0.shape[0]
    return pl.pallas_call(
        functools.partial(_moe_src_kernel, n=n, n_rows=n_rows),
        grid_spec=pltpu.PrefetchScalarGridSpec(
            num_scalar_prefetch=2, grid=(1,), in_specs=[],
            out_specs=pl.BlockSpec(memory_space=pltpu.SMEM)),
        out_shape=jax.ShapeDtypeStruct((n_rows,), jnp.int32),
        name="moe_src",
    )(pos0, pos1)


def _moe_ln2(x1_slab, ext, wg, wu, wd, layer, g, bb):
    n = ext.shape[0]
    tm = 512 if n >= 8192 else 256
    n_rows = 2 * n + N_EXPERTS * tm
    sel = (ext[:, EXT_SEL:EXT_SEL + N_EXPERTS] > 0.5).astype(jnp.int32)
    csum = jnp.cumsum(sel, axis=0)
    padded = (csum[-1] + tm - 1) // tm * tm
    seg_end = jnp.cumsum(padded)
    slot = (seg_end - padded)[None, :] + csum - sel
    experts = jnp.arange(N_EXPERTS, dtype=jnp.int32)[None, :]
    e_lo = ext[:, EXT_W_LO + 2].astype(jnp.int32)[:, None]
    e_hi = ext[:, EXT_W_LO + 3].astype(jnp.int32)[:, None]
    pos0 = jnp.sum(jnp.where(experts == e_lo, slot, 0), axis=1)
    pos1 = jnp.sum(jnp.where(experts == e_hi, slot, 0), axis=1)
    nact = seg_end[-1:] // tm
    tiles = jnp.arange(n_rows // tm, dtype=jnp.int32)
    first_row = jnp.minimum(tiles, nact[0] - 1) * tm
    tile_e = jnp.minimum(jnp.sum((seg_end[None, :] <= first_row[:, None]).astype(jnp.int32), axis=1), N_EXPERTS - 1)

    src = _moe_src(pos0, pos1, n_rows)
    ys = _moe_ffn(tile_e, nact, src, x1_slab, wg, wu, wd, tm, layer)
    return _moe_combine(pos0, pos1, x1_slab, ext, ys, g, bb)


def _band_bias_kernel(tab_ref, o_ref):
    size, width = tab_ref.shape[1], o_ref.shape[2]
    entry = lax.broadcasted_iota(jnp.int32, (size, width), 0)
    w = lax.broadcasted_iota(jnp.int32, (size, width), 1)
    tab = tab_ref[...]
    for l in range(o_ref.shape[1]):
        idx = jnp.clip(CB_PAST_ROWS + l - w, -REL_CLIP, REL_CLIP) + REL_CLIP
        o_ref[:, l, :] = _dot_hi(tab, jnp.where(entry == idx, 1.0, 0.0))


def _band_bias(rel_bias):
    depth, h, size = rel_bias.shape
    size_pad = -(-size // LANES) * LANES
    tab = jnp.pad(rel_bias.reshape(depth * h, size), ((0, 0), (0, size_pad - size)))
    width = CB_PAST_ROWS + CHUNK
    out = pl.pallas_call(
        _band_bias_kernel,
        out_shape=jax.ShapeDtypeStruct((depth * h, CHUNK, width), F32),
        name="band_bias",
    )(tab)
    return out.reshape(depth, h, CHUNK, width)


_HALF_ROPE = MLA_ROPE // 2
_SRC_KR = MLA_Q_RANK + MLA_KV_RANK
_SRC_GQKV = _SRC_KR + MLA_ROPE
_SRC_GZ = _SRC_GQKV + GDN_CONV_DIM
_SRC_GBA = _SRC_GZ + GDN_VAL_DIM
_SRC_CB = _SRC_GBA + 2 * GDN_HEADS
W_IN_SEGMENTS = (
    (C_GQKV, _SRC_GQKV, GDN_CONV_DIM), (C_GZ, _SRC_GZ, GDN_VAL_DIM), (C_CQ, 0, MLA_Q_RANK),
    (C_CKV, MLA_Q_RANK, MLA_KV_RANK), (C_KR, _SRC_KR, MLA_ROPE),
    (C_KR + MLA_ROPE, _SRC_KR + _HALF_ROPE, _HALF_ROPE), (C_KR + MLA_ROPE + _HALF_ROPE, _SRC_KR, _HALF_ROPE),
    (C_GBA, _SRC_GBA, 2 * GDN_HEADS), (C_CB, _SRC_CB, 3 * CB_DIM))


def _w_in_relayout_kernel(w_ref, o_ref):
    o_ref[0, :, C_GBA:C_GBA + LANES] = jnp.zeros((o_ref.shape[1], LANES), BF16)
    for dst, src, width in W_IN_SEGMENTS:
        o_ref[0, :, dst:dst + width] = w_ref[0, :, src:src + width].astype(BF16)


def _w_in_relayout(w_in):
    depth, d, width = w_in.shape
    tm = 256
    return pl.pallas_call(
        _w_in_relayout_kernel,
        grid=(depth, d // tm),
        in_specs=[pl.BlockSpec((1, tm, width), lambda l, i: (l, i, 0))],
        out_specs=pl.BlockSpec((1, tm, IN_PAD), lambda l, i: (l, i, 0)),
        out_shape=jax.ShapeDtypeStruct((depth, d, IN_PAD), BF16),
        compiler_params=_cparams(("parallel", "parallel")),
        name="w_in_relayout",
    )(w_in)


def _prep_layer(q_norm_g, w_uq, kv_norm_g, w_uk, w_uv, conv_w, a_log, dt_bias, gdn_norm_g, rel_bias, w_out,
                ln1_g, ln1_b, ln2_g, ln2_b):
    half = MLA_ROPE // 2
    r = w_uq.shape[0]
    wq_nope = w_uq[:, :, :MLA_NOPE].reshape(r, MLA_HEADS * MLA_NOPE).astype(BF16)
    wq_r = w_uq[:, :, MLA_NOPE:]
    wq_rope = wq_r.reshape(r, MLA_HEADS * MLA_ROPE).astype(BF16)
    wq_rope_sw = jnp.concatenate([wq_r[..., half:], wq_r[..., :half]], -1).reshape(r, MLA_HEADS * MLA_ROPE).astype(BF16)
    wuk_t = jnp.transpose(w_uk, (1, 2, 0)).astype(BF16)
    wuv = jnp.transpose(w_uv, (1, 0, 2)).astype(BF16)
    lane_pad = lambda a: jnp.pad(a, (GDN_HEADS, LANES - 2 * GDN_HEADS))[None, :]
    bias = rel_bias
    w_out_b = w_out.astype(BF16)
    na = MLA_HEADS * MLA_V
    return dict(
        q_norm_g=q_norm_g[None, :], kv_norm_g=kv_norm_g[None, :],
        wq_nope=wq_nope, wq_rope=wq_rope, wq_rope_sw=wq_rope_sw, wuk_t=wuk_t, wuv=wuv,
        conv_w=conv_w, alog=lane_pad(a_log), dtb=lane_pad(dt_bias), gnorm=gdn_norm_g[None, :], bias=bias,
        wo_a=w_out_b[:na], wo_b=w_out_b[na:na + GDN_VAL_DIM], wo_c=w_out_b[na + GDN_VAL_DIM:],
        ln1_g=ln1_g[None, :], ln1_b=ln1_b[None, :], ln2_g=ln2_g[None, :], ln2_b=ln2_b[None, :])


def _rope_tables(start, t):
    pos = start + jnp.arange(t, dtype=jnp.int32)
    inv = ROPE_THETA ** (-jnp.arange(0, MLA_ROPE, 2, dtype=F32) / MLA_ROPE)
    ang = pos.astype(F32)[:, None] * inv[None, :]
    cos, sin = jnp.cos(ang), jnp.sin(ang)
    return jnp.concatenate([cos, cos], -1), jnp.concatenate([-sin, sin], -1)


def _layer(x, p, layer, shared, caches):
    b, t, d = x.shape
    x2d = x.reshape(b * t, d)
    proj = _in_proj(x2d, shared["w_in"], layer).reshape(b, t, IN_PAD)

    start = 0 if caches is None else caches["ckv"].shape[1]
    cos2, sin2 = _rope_tables(start, t)
    q, ckv_new, krope_new = _mla_prep(proj, cos2, sin2, p["q_norm_g"], p["kv_norm_g"], p["wq_nope"], p["wq_rope"],
                                      p["wq_rope_sw"], p["wuk_t"])
    if caches is None:
        o_a = _mla_attn(q, ckv_new, krope_new, p["wuv"], 0)
        s_past = jnp.zeros((b, GDN_HEADS, GDN_DK, GDN_DV), F32)
        conv_past = jnp.zeros((b, GDN_CONV - 1, GDN_CONV_DIM), F32)
        o_c = _cb_attn(proj, None, None, layer, p["bias"])
    else:
        o_a = _mla_attn_cached(q, caches["ckv"], caches["krope"], layer, ckv_new, krope_new, p["wuv"])
        s_past, conv_past = caches["gdn"][layer], caches["conv"][layer]
        o_c = _cb_attn(proj, caches["cb_k"], caches["cb_v"], layer, p["bias"])
    o_b, s_new, conv_new = _gdn(proj, conv_past, p["conv_w"], p["alog"], p["dtb"], p["gnorm"], s_past)

    x1, ext = _out_ln1(o_a.reshape(b * t, -1), o_b.reshape(b * t, -1), o_c.reshape(b * t, -1), x2d,
                       p["wo_a"], p["wo_b"], p["wo_c"], p["ln1_g"], p["ln1_b"], shared["rwt"], shared["rb"])
    x2 = _moe_ln2(x1, ext, shared["w_gate"], shared["w_up"], shared["w_down"], layer, p["ln2_g"], p["ln2_b"])

    keep = min(CB_PAST_ROWS, t)
    cb_new = lambda col: proj[:, t - keep:, col:col + CB_DIM].reshape(b, keep, CB_HEADS, CB_DH)
    state = (ckv_new, krope_new, s_new, conv_new, cb_new(C_CB + CB_DIM), cb_new(C_CB + 2 * CB_DIM))
    return x2.reshape(b, t, d), state


def kernel(x_prompt, x_sample, cache_mla_ckv, cache_mla_krope, state_gdn, state_gdn_conv, cache_cb_k, cache_cb_v,
           w_in, q_norm_g, w_uq, kv_norm_g, w_uk, w_uv, conv_w, a_log, dt_bias, gdn_norm_g, rel_bias, w_out,
           ln1_g, ln1_b, router_w, router_b, w_gate, w_up, w_down, ln2_g, ln2_b):
    depth = w_in.shape[0]
    band_bias = _band_bias(rel_bias)
    layers = [_prep_layer(q_norm_g[l], w_uq[l], kv_norm_g[l], w_uk[l], w_uv[l], conv_w[l], a_log[l],
                          dt_bias[l], gdn_norm_g[l], band_bias[l], w_out[l], ln1_g[l], ln1_b[l],
                          ln2_g[l], ln2_b[l]) for l in range(depth)]
    stack = lambda w: w.reshape((depth * N_EXPERTS,) + w.shape[2:])
    shared = dict(w_in=_w_in_relayout(w_in), rwt=router_w.T, rb=router_b[:, None], w_gate=stack(w_gate), w_up=stack(w_up), w_down=stack(w_down))
    merge = lambda a: a.reshape((a.shape[0] * a.shape[1],) + a.shape[2:])
    rows = cache_cb_k.shape[2]
    caches = dict(ckv=merge(cache_mla_ckv), krope=merge(cache_mla_krope), gdn=state_gdn, conv=state_gdn_conv,
                  cb_k=merge(cache_cb_k).reshape(-1, rows, CB_DIM), cb_v=merge(cache_cb_v).reshape(-1, rows, CB_DIM))

    def run_trunk(x, trunk_caches):
        new = ([], [], [], [], [], [])
        for l in range(depth):
            x, st = _layer(x, layers[l], l, shared, trunk_caches)
            for lst, a in zip(new, st):
                lst.append(a)
        return (x, *[jnp.stack(a) for a in new])

    outs_p = run_trunk(x_prompt, None)
    outs_s = run_trunk(x_sample, caches)
    return (outs_p[0], outs_s[0], *outs_p[1:], *outs_s[1:])
```

```python
import functools
import math

import jax
import jax.numpy as jnp
import numpy as np
from jax import lax
from jax.experimental import pallas as pl
from jax.experimental.pallas import tpu as pltpu

F32 = jnp.float32
BF16 = jnp.bfloat16

CHUNK = 64
MLA_HEADS = 6
MLA_Q_RANK = 512
MLA_KV_RANK = 256
MLA_NOPE = 128
MLA_ROPE = 64
MLA_V = 128
ROPE_THETA = 10000.0
GDN_HEADS = 6
GDN_DK = 128
GDN_DV = 128
GDN_CONV = 4
GDN_KEY_DIM = GDN_HEADS * GDN_DK
GDN_VAL_DIM = GDN_HEADS * GDN_DV
GDN_CONV_DIM = 2 * GDN_KEY_DIM + GDN_VAL_DIM
CB_HEADS = 4
CB_DH = 128
CB_DIM = CB_HEADS * CB_DH
CB_PAST_ROWS = 8 * CHUNK
REL_CLIP = 256
N_EXPERTS = 16
N_GROUPS = 4
EXPERTS_PER_GROUP = N_EXPERTS // N_GROUPS
DEPTH = 2
DEEPNORM_ALPHA = (2 * DEPTH) ** 0.25

LANES = 128
C_GQKV = 0
C_GZ = C_GQKV + GDN_CONV_DIM
C_CQ = C_GZ + GDN_VAL_DIM
C_CKV = C_CQ + MLA_Q_RANK
C_KR = C_CKV + MLA_KV_RANK
C_GBA = C_KR + 2 * MLA_ROPE
C_CB = C_GBA + LANES
IN_PAD = C_CB + 3 * CB_DIM

SLAB_ROWS = 2048 // LANES
EXT_SEL = 0
EXT_W_LO = N_EXPERTS

VMEM_LIMIT = 56 * 1024 * 1024


def _cparams(sem):
    return pltpu.CompilerParams(dimension_semantics=sem, vmem_limit_bytes=VMEM_LIMIT)


def _dot(a, b):
    return jnp.dot(a, b, preferred_element_type=F32)


def _dot_nt(a, b):
    return lax.dot_general(a, b, (((1,), (1,)), ((), ())), preferred_element_type=F32)


def _dot_hi(a, b):
    return jnp.dot(a, b, preferred_element_type=F32, precision=lax.Precision.HIGHEST)


def _dot_nt_hi(a, b):
    return lax.dot_general(a, b, (((1,), (1,)), ((), ())), preferred_element_type=F32,
                           precision=lax.Precision.HIGHEST)


def _sigmoid(x):
    return 1.0 / (1.0 + jnp.exp(-x))


def _silu(x):
    return x * _sigmoid(x)


def _layer_norm(h, g, b, eps=1e-5):
    mu = jnp.mean(h, -1, keepdims=True)
    d = h - mu
    var = jnp.mean(d * d, -1, keepdims=True)
    return d * lax.rsqrt(var + eps) * g + b


def _rms_norm(x, g, eps=1e-6):
    return x * lax.rsqrt(jnp.mean(x * x, -1, keepdims=True) + eps) * g


def _inproj_kernel(x_ref, w_ref, o_ref, xb_ref):
    @pl.when(pl.program_id(1) == 0)
    def _():
        xb_ref[...] = x_ref[...].astype(BF16)

    o_ref[...] = _dot(xb_ref[...], w_ref[0])


def _in_proj(x2d, row0, n, w, layer):
    d = x2d.shape[1]
    width = w.shape[2]
    tm = min(1024, n)
    tn = 512
    assert row0 % tm == 0 and n % tm == 0
    return pl.pallas_call(
        _inproj_kernel,
        grid=(n // tm, width // tn),
        in_specs=[pl.BlockSpec((tm, d), lambda i, j: (row0 // tm + i, 0)),
                  pl.BlockSpec((1, d, tn), lambda i, j: (layer, 0, j))],
        out_specs=pl.BlockSpec((tm, tn), lambda i, j: (i, j)),
        out_shape=jax.ShapeDtypeStruct((n, width), F32),
        scratch_shapes=[pltpu.VMEM((tm, d), BF16)],
        compiler_params=_cparams(("parallel", "arbitrary")),
        name="in_proj",
    )(x2d, w)


def _mla_prep_kernel(cq_ref, ckv_ref, kr_ref, cosq_ref, sinq_ref, cosk_ref, sink_ref,
                     qg_ref, kvg_ref, wqn_ref, wqr_ref, wqs_ref, wuk_ref,
                     q_ref, ckvn_ref, krn_ref):
    cqn = _rms_norm(cq_ref[0], qg_ref[...]).astype(BF16)
    q_nope = _dot(cqn, wqn_ref[...]).astype(BF16)
    q_rope = _dot(cqn, wqr_ref[...])
    q_rope_sw = _dot(cqn, wqs_ref[...])
    q_rot = (q_rope * cosq_ref[...] + q_rope_sw * sinq_ref[...]).astype(BF16)
    for h in range(MLA_HEADS):
        q_lat = _dot(q_nope[:, h * MLA_NOPE:(h + 1) * MLA_NOPE], wuk_ref[h])
        q_ref[0, h, :, 0:MLA_KV_RANK] = q_lat.astype(BF16)
        q_ref[0, h, :, MLA_KV_RANK:] = q_rot[:, h * MLA_ROPE:(h + 1) * MLA_ROPE]
    ckvn_ref[0] = _rms_norm(ckv_ref[0], kvg_ref[...])
    kr = kr_ref[0]
    krn_ref[0] = kr[:, :MLA_ROPE] * cosk_ref[...] + kr[:, MLA_ROPE:] * sink_ref[...]


def _mla_prep(proj, cos2, sin2, q_norm_g, kv_norm_g, wq_nope, wq_rope, wq_rope_sw, wuk_t):
    b, t, _ = proj.shape
    tm = min(512, t)
    cosq = jnp.tile(cos2, (1, MLA_HEADS))
    sinq = jnp.tile(sin2, (1, MLA_HEADS))
    full = lambda a: pl.BlockSpec(a.shape, lambda bi, i: (0,) * a.ndim)
    row = lambda w: pl.BlockSpec((tm, w), lambda bi, i: (i, 0))
    qd = MLA_KV_RANK + MLA_ROPE
    return pl.pallas_call(
        _mla_prep_kernel,
        grid=(b, t // tm),
        in_specs=[pl.BlockSpec((1, tm, MLA_Q_RANK), lambda bi, i: (bi, i, C_CQ // MLA_Q_RANK)),
                  pl.BlockSpec((1, tm, MLA_KV_RANK), lambda bi, i: (bi, i, C_CKV // MLA_KV_RANK)),
                  pl.BlockSpec((1, tm, 2 * MLA_ROPE), lambda bi, i: (bi, i, C_KR // (2 * MLA_ROPE))),
                  row(MLA_HEADS * MLA_ROPE), row(MLA_HEADS * MLA_ROPE), row(MLA_ROPE), row(MLA_ROPE),
                  full(q_norm_g), full(kv_norm_g), full(wq_nope), full(wq_rope), full(wq_rope_sw),
                  full(wuk_t)],
        out_specs=[pl.BlockSpec((1, MLA_HEADS, tm, qd), lambda bi, i: (bi, 0, i, 0)),
                   pl.BlockSpec((1, tm, MLA_KV_RANK), lambda bi, i: (bi, i, 0)),
                   pl.BlockSpec((1, tm, MLA_ROPE), lambda bi, i: (bi, i, 0))],
        out_shape=[jax.ShapeDtypeStruct((b, MLA_HEADS, t, qd), BF16),
                   jax.ShapeDtypeStruct((b, t, MLA_KV_RANK), F32),
                   jax.ShapeDtypeStruct((b, t, MLA_ROPE), F32)],
        compiler_params=_cparams(("parallel", "parallel")),
        name="mla_prep",
    )(proj, proj, proj, cosq, sinq, cos2, sin2, q_norm_g, kv_norm_g, wq_nope, wq_rope, wq_rope_sw, wuk_t)


def _mla_attn_kernel(qi_ref, kj_ref, q_ref, ckv_ref, kr_ref, wuv_ref, o_ref, m_ref, l_ref, acc_ref, *, start, tq, tk):
    step = pl.program_id(1)
    i = qi_ref[step]
    j = kj_ref[step]

    @pl.when(j == 0)
    def _():
        m_ref[...] = jnp.full(m_ref.shape, -jnp.inf, F32)
        l_ref[...] = jnp.zeros(l_ref.shape, F32)
        acc_ref[...] = jnp.zeros(acc_ref.shape, F32)

    ckv = ckv_ref[0].astype(BF16)
    kr = kr_ref[0].astype(BF16)
    scale = (MLA_NOPE + MLA_ROPE) ** -0.5

    hpg = 1 if tq >= 256 else MLA_HEADS
    grows = hpg * tq

    def update(masked):
        if masked:
            q_pos = start + i * tq + lax.broadcasted_iota(jnp.int32, (grows, 1), 0) % tq
            k_pos = j * tk + lax.broadcasted_iota(jnp.int32, (1, tk), 1)
            visible = k_pos // CHUNK <= q_pos // CHUNK
        def scores(g):
            q = q_ref[0, g] if hpg == 1 else q_ref[0].reshape(grows, MLA_KV_RANK + MLA_ROPE)
            return _dot_nt(q[:, :MLA_KV_RANK], ckv) + _dot_nt(q[:, MLA_KV_RANK:], kr)

        groups = MLA_HEADS // hpg
        s_next = scores(0)
        for g in range(groups):
            s, s_next = s_next, (scores(g + 1) if g + 1 < groups else None)
            s = s * scale
            if masked:
                s = jnp.where(visible, s, -jnp.inf)
            rows = slice(g * grows, (g + 1) * grows)
            m_old = m_ref[rows]
            m_new = jnp.maximum(m_old, jnp.max(s, -1, keepdims=True))
            alpha = jnp.exp(m_old - m_new)
            p = jnp.exp(s - m_new)
            l_ref[rows] = alpha * l_ref[rows] + jnp.sum(p, -1, keepdims=True)
            acc_ref[rows] = alpha * acc_ref[rows] + _dot(p.astype(BF16), ckv)
            m_ref[rows] = m_new

    needs_mask = (j * tk + tk - 1) // CHUNK > (start + i * tq) // CHUNK
    pl.when(needs_mask)(lambda: update(True))
    pl.when(jnp.logical_not(needs_mask))(lambda: update(False))

    @pl.when(j == _last_kv_block(start, i, tq, tk))
    def _():
        o_lat = (acc_ref[...] / l_ref[...]).astype(BF16)
        for h in range(MLA_HEADS):
            o_ref[0, :, h * MLA_V:(h + 1) * MLA_V] = _dot(o_lat[h * tq:(h + 1) * tq], wuv_ref[h]).astype(BF16)


MLA_SUB_KEYS = 512


def _mla_attn_cached_kernel(q_ref, pckv_ref, pkr_ref, nckv_ref, nkr_ref, wuv_ref, o_ref, m_ref, l_ref, acc_ref, *, nb):
    j = pl.program_id(1)
    tq = nckv_ref.shape[1]
    rows = MLA_HEADS * tq

    @pl.when(j == 0)
    def _():
        m_ref[...] = jnp.full(m_ref.shape, -jnp.inf, F32)
        l_ref[...] = jnp.zeros(l_ref.shape, F32)
        acc_ref[...] = jnp.zeros(acc_ref.shape, F32)

    def update(ckv_ref, kr_ref):
        keys = ckv_ref.shape[1]
        sub = MLA_SUB_KEYS if keys % MLA_SUB_KEYS == 0 else keys
        q = q_ref[0].reshape(rows, MLA_KV_RANK + MLA_ROPE)

        def scores(u):
            ckv = ckv_ref[0, u * sub:(u + 1) * sub, :].astype(BF16)
            kr = kr_ref[0, u * sub:(u + 1) * sub, :].astype(BF16)
            return _dot_nt(q[:, :MLA_KV_RANK], ckv) + _dot_nt(q[:, MLA_KV_RANK:], kr), ckv

        nxt = scores(0)
        for u in range(keys // sub):
            (s, ckv), nxt = nxt, (scores(u + 1) if (u + 1) * sub < keys else None)
            s = s * ((MLA_NOPE + MLA_ROPE) ** -0.5)
            m_old = m_ref[...]
            m_new = jnp.maximum(m_old, jnp.max(s, -1, keepdims=True))
            alpha = jnp.exp(m_old - m_new)
            p = jnp.exp(s - m_new)
            l_ref[...] = alpha * l_ref[...] + jnp.sum(p, -1, keepdims=True)
            acc_ref[...] = alpha * acc_ref[...] + _dot(p.astype(BF16), ckv)
            m_ref[...] = m_new

    pl.when(j < nb)(lambda: update(pckv_ref, pkr_ref))

    @pl.when(j == nb)
    def _():
        update(nckv_ref, nkr_ref)
        o_lat = (acc_ref[...] / l_ref[...]).astype(BF16)
        for h in range(MLA_HEADS):
            o_ref[0, :, h * MLA_V:(h + 1) * MLA_V] = _dot(o_lat[h * tq:(h + 1) * tq], wuv_ref[h]).astype(BF16)


def _mla_attn_cached(q, ckv_cache, kr_cache, layer, ckv_new, kr_new, wuv):
    b, _, t, qd = q.shape
    past = ckv_cache.shape[1]
    assert t == CHUNK and past % CHUNK == 0
    tk = next((c for c in (2048, 1024) if past % c == 0), None) or _pick_tk(past)
    nb = past // tk
    past_map = lambda bi, j: (layer * b + bi, jnp.minimum(j, nb - 1), 0)
    new_map = lambda bi, j: (bi, 0, 0)
    return pl.pallas_call(
        functools.partial(_mla_attn_cached_kernel, nb=nb),
        grid=(b, nb + 1),
        in_specs=[pl.BlockSpec((1, MLA_HEADS, t, qd), lambda bi, j: (bi, 0, 0, 0)),
                  pl.BlockSpec((1, tk, MLA_KV_RANK), past_map),
                  pl.BlockSpec((1, tk, MLA_ROPE), past_map),
                  pl.BlockSpec((1, t, MLA_KV_RANK), new_map),
                  pl.BlockSpec((1, t, MLA_ROPE), new_map),
                  pl.BlockSpec(wuv.shape, lambda bi, j: (0, 0, 0))],
        out_specs=pl.BlockSpec((1, t, MLA_HEADS * MLA_V), lambda bi, j: (bi, 0, 0)),
        out_shape=jax.ShapeDtypeStruct((b, t, MLA_HEADS * MLA_V), BF16),
        scratch_shapes=[pltpu.VMEM((MLA_HEADS * t, 1), F32),
                        pltpu.VMEM((MLA_HEADS * t, 1), F32),
                        pltpu.VMEM((MLA_HEADS * t, MLA_KV_RANK), F32)],
        compiler_params=_cparams(("parallel", "arbitrary")),
        name="mla_attn_cached",
    )(q, ckv_cache, kr_cache, ckv_new, kr_new, wuv)


def _last_kv_block(start, i, tq, tk):
    return ((start + i * tq + tq - 1) // CHUNK * CHUNK) // tk


def _pick_tk(s):
    for cand in (512, 1024, 832, 768, 640, 576, 448, 384, 320, 256, 192, 128, 64):
        if s % cand == 0:
            return cand
    raise ValueError(f"unsupported key length {s}")


def _mla_attn(q, ckv_all, kr_all, wuv, start):
    b, _, t, qd = q.shape
    s = ckv_all.shape[1]
    tq = min(256, t)
    tk = _pick_tk(s)
    assert tk % CHUNK == 0 and s % tk == 0 and t % tq == 0
    pairs = [(i, j) for i in range(t // tq) for j in range(_last_kv_block(start, i, tq, tk) + 1)]
    qi = jnp.asarray(np.array([p[0] for p in pairs], np.int32))
    kj = jnp.asarray(np.array([p[1] for p in pairs], np.int32))
    kv_map = lambda bi, st, qi_r, kj_r: (bi, kj_r[st], 0)
    return pl.pallas_call(
        functools.partial(_mla_attn_kernel, start=start, tq=tq, tk=tk),
        grid_spec=pltpu.PrefetchScalarGridSpec(
            num_scalar_prefetch=2, grid=(b, len(pairs)),
            in_specs=[pl.BlockSpec((1, MLA_HEADS, tq, qd), lambda bi, st, qi_r, kj_r: (bi, 0, qi_r[st], 0)),
                      pl.BlockSpec((1, tk, MLA_KV_RANK), kv_map),
                      pl.BlockSpec((1, tk, MLA_ROPE), kv_map),
                      pl.BlockSpec(wuv.shape, lambda bi, st, qi_r, kj_r: (0, 0, 0))],
            out_specs=pl.BlockSpec((1, tq, MLA_HEADS * MLA_V), lambda bi, st, qi_r, kj_r: (bi, qi_r[st], 0)),
            scratch_shapes=[pltpu.VMEM((MLA_HEADS * tq, 1), F32),
                            pltpu.VMEM((MLA_HEADS * tq, 1), F32),
                            pltpu.VMEM((MLA_HEADS * tq, MLA_KV_RANK), F32)]),
        out_shape=jax.ShapeDtypeStruct((b, t, MLA_HEADS * MLA_V), BF16),
        compiler_params=_cparams(("parallel", "arbitrary")),
        name="mla_attn",
    )(qi, kj, q, ckv_all, kr_all, wuv)


def _split_bf16(x):
    hi = x.astype(BF16)
    return hi, (x - hi.astype(F32)).astype(BF16)


def _unit_lower_solve_many(a_list, rhs_list):
    n = rhs_list[0].shape[1]
    levels = int(math.log2(CHUNK))
    xs, ps = list(rhs_list), list(a_list)
    for lvl in range(levels):
        for h in range(len(xs)):
            p_hi, p_lo = _split_bf16(ps[h])
            lhs = jnp.concatenate([p_hi, p_hi, p_lo], 1)
            if lvl < levels - 1:
                r_hi, r_lo = _split_bf16(jnp.concatenate([xs[h], ps[h]], 1))
                both = _dot(lhs, jnp.concatenate([r_hi, r_lo, r_hi], 0))
                px, ps[h] = both[:, :n], both[:, n:]
            else:
                r_hi, r_lo = _split_bf16(xs[h])
                px = _dot(lhs, jnp.concatenate([r_hi, r_lo, r_hi], 0))
            xs[h] = xs[h] - px if lvl == 0 else xs[h] + px
    return xs


def _gdn_kernel(qkv_ref, gz_ref, gba_ref, convp_ref, convw_ref, alog_ref, dtb_ref, gnorm_ref, s0_ref,
                ob_ref, snew_ref, convn_ref, s_scr, ext_scr, *, n_chunks):
    n = pl.program_id(1)
    L = CHUNK
    tail = 8

    @pl.when(n == 0)
    def _():
        s_scr[...] = s0_ref[0]
        ext_scr[0:tail, :] = jnp.zeros((tail, GDN_CONV_DIM), F32)
        ext_scr[tail - (GDN_CONV - 1):tail, :] = convp_ref[0]

    cur = qkv_ref[0]
    ext_scr[tail:tail + L, :] = cur
    w = convw_ref[...]
    conv = ext_scr[tail - 3:tail - 3 + L, :] * w[0:1]
    conv = conv + ext_scr[tail - 2:tail - 2 + L, :] * w[1:2]
    conv = conv + ext_scr[tail - 1:tail - 1 + L, :] * w[2:3]
    conv = conv + cur * w[3:4]
    conv = _silu(conv)

    @pl.when(n == n_chunks - 1)
    def _():
        convn_ref[0] = ext_scr[tail + L - (GDN_CONV - 1):tail + L, :]

    ext_scr[0:tail, :] = cur[L - tail:, :]

    gba = gba_ref[0]
    beta_all = _sigmoid(gba)
    z = gba + dtb_ref[...]
    softplus = jnp.maximum(z, 0.0) + jnp.log1p(jnp.exp(-jnp.abs(z)))
    g_all = -jnp.exp(alog_ref[...]) * softplus
    ri = lax.broadcasted_iota(jnp.int32, (L, L), 0)
    ci = lax.broadcasted_iota(jnp.int32, (L, L), 1)
    incl = ci <= ri
    strict = ci < ri
    g_cum = _dot_hi(incl.astype(F32), g_all)
    g_cum_t = g_cum.T

    def l2n(x):
        return x * lax.rsqrt(jnp.sum(x * x, -1, keepdims=True) + 1e-6)

    heads = range(GDN_HEADS)
    q = [l2n(conv[:, h * GDN_DK:(h + 1) * GDN_DK]) * (GDN_DK ** -0.5) for h in heads]
    k = [l2n(conv[:, GDN_KEY_DIM + h * GDN_DK:GDN_KEY_DIM + (h + 1) * GDN_DK]) for h in heads]
    v = [conv[:, 2 * GDN_KEY_DIM + h * GDN_DV:2 * GDN_KEY_DIM + (h + 1) * GDN_DV] for h in heads]
    beta = [beta_all[:, h:h + 1] for h in heads]
    gc = [g_cum[:, GDN_HEADS + h:GDN_HEADS + h + 1] for h in heads]
    dmat = [jnp.exp(jnp.where(incl, gc[h] - g_cum_t[GDN_HEADS + h:GDN_HEADS + h + 1, :], -jnp.inf)) for h in heads]
    eg = [jnp.exp(gc[h]) for h in heads]
    qk_kk = []
    for h in heads:
        kb = k[h].astype(BF16)
        qk_kk.append(_dot_nt(jnp.concatenate([q[h].astype(BF16), kb], 0), kb))
    a_mat = [jnp.where(strict, beta[h] * qk_kk[h][L:] * dmat[h], 0.0) for h in heads]
    rhs = [jnp.concatenate([beta[h] * v[h], (beta[h] * eg[h]) * k[h]], -1) for h in heads]
    sol = _unit_lower_solve_many(a_mat, rhs)
    s_old = [s_scr[h] for h in heads]
    wq_s = [_dot(jnp.concatenate([sol[h][:, GDN_DV:], q[h]], 0).astype(BF16), s_old[h].astype(BF16)) for h in heads]
    upd = []
    for h in heads:
        delta = sol[h][:, :GDN_DV] - wq_s[h][:L]
        kd = k[h] * jnp.exp(gc[h][L - 1:L, :] - gc[h])
        lhs = jnp.concatenate([qk_kk[h][:L] * dmat[h], kd.T], 0)
        upd.append(_dot(lhs.astype(BF16), delta.astype(BF16)))
    for h in heads:
        o = eg[h] * wq_s[h][L:] + upd[h][:L]
        s_scr[h] = jnp.exp(gc[h][L - 1:L, :]) * s_old[h] + upd[h][L:]
        gz = gz_ref[0, :, h * GDN_DV:(h + 1) * GDN_DV]
        ob_ref[0, :, h * GDN_DV:(h + 1) * GDN_DV] = (_rms_norm(o, gnorm_ref[...]) * _silu(gz)).astype(BF16)

    @pl.when(n == n_chunks - 1)
    def _():
        snew_ref[0] = s_scr[...]


def _gdn(proj, conv_past, conv_w, alog128, dtb128, gnorm, s0):
    b, t, _ = proj.shape
    L = CHUNK
    full = lambda a: pl.BlockSpec(a.shape, lambda bi, n: (0,) * a.ndim)
    return pl.pallas_call(
        functools.partial(_gdn_kernel, n_chunks=t // L),
        grid=(b, t // L),
        in_specs=[pl.BlockSpec((1, L, GDN_CONV_DIM), lambda bi, n: (bi, n, C_GQKV // GDN_CONV_DIM)),
                  pl.BlockSpec((1, L, GDN_VAL_DIM), lambda bi, n: (bi, n, C_GZ // GDN_VAL_DIM)),
                  pl.BlockSpec((1, L, LANES), lambda bi, n: (bi, n, C_GBA // LANES)),
                  pl.BlockSpec((1, GDN_CONV - 1, GDN_CONV_DIM), lambda bi, n: (bi, 0, 0)),
                  full(conv_w), full(alog128), full(dtb128), full(gnorm),
                  pl.BlockSpec((1, GDN_HEADS, GDN_DK, GDN_DV), lambda bi, n: (bi, 0, 0, 0))],
        out_specs=[pl.BlockSpec((1, L, GDN_VAL_DIM), lambda bi, n: (bi, n, 0)),
                   pl.BlockSpec((1, GDN_HEADS, GDN_DK, GDN_DV), lambda bi, n: (bi, 0, 0, 0)),
                   pl.BlockSpec((1, GDN_CONV - 1, GDN_CONV_DIM), lambda bi, n: (bi, 0, 0))],
        out_shape=[jax.ShapeDtypeStruct((b, t, GDN_VAL_DIM), BF16),
                   jax.ShapeDtypeStruct((b, GDN_HEADS, GDN_DK, GDN_DV), F32),
                   jax.ShapeDtypeStruct((b, GDN_CONV - 1, GDN_CONV_DIM), F32)],
        scratch_shapes=[pltpu.VMEM((GDN_HEADS, GDN_DK, GDN_DV), F32),
                        pltpu.VMEM((8 + L, GDN_CONV_DIM), F32)],
        compiler_params=_cparams(("parallel", "arbitrary")),
        name="gdn",
    )(proj, proj, proj, conv_past, conv_w, alog128, dtb128, gnorm, s0)


def _cb_attn_kernel(q_ref, kprev_ref, kcur_ref, vprev_ref, vcur_ref, bias_ref, o_ref, *, tq, pad):
    i = pl.program_id(1)
    L = CHUNK
    width = CB_PAST_ROWS + L
    scale = CB_DH ** -0.5
    for c in range(tq // L):
        lo = c * L
        kwin = jnp.concatenate([kprev_ref[0, lo:, :], kcur_ref[0, :lo + L, :]], 0).astype(BF16)
        vwin = jnp.concatenate([vprev_ref[0, lo:, :], vcur_ref[0, :lo + L, :]], 0).astype(BF16)
        q = q_ref[0, lo:lo + L, :].astype(BF16)
        row = i * tq + lo + lax.broadcasted_iota(jnp.int32, (1, width), 1)
        valid = row >= pad
        heads = [slice(h * CB_DH, (h + 1) * CB_DH) for h in range(CB_HEADS)]
        scores = [_dot_nt(q[:, hs], kwin[:, hs]) for hs in heads]
        probs = []
        for h in range(CB_HEADS):
            s = scores[h] * scale + bias_ref[h]
            s = jnp.where(valid, s, -jnp.inf)
            m = jnp.max(s, -1, keepdims=True)
            p = jnp.exp(s - m)
            probs.append((p / jnp.sum(p, -1, keepdims=True)).astype(BF16))
        for h, hs in enumerate(heads):
            o_ref[0, lo:lo + L, hs] = _dot(probs[h], vwin[:, hs]).astype(BF16)


def _cb_attn(proj, k_cache, v_cache, layer, bias):
    b, t, _ = proj.shape
    tq = min(CB_PAST_ROWS, t)
    kcol, vcol = C_CB // CB_DIM + 1, C_CB // CB_DIM + 2
    cur = lambda col: pl.BlockSpec((1, tq, CB_DIM), lambda bi, i: (bi, i, col))
    if k_cache is None:
        assert tq == CB_PAST_ROWS and t % tq == 0
        pad = CB_PAST_ROWS
        prev = lambda col: pl.BlockSpec((1, tq, CB_DIM), lambda bi, i: (bi, jnp.maximum(i - 1, 0), col))
        k_prev_arr, v_prev_arr, k_prev, v_prev = proj, proj, prev(kcol), prev(vcol)
    else:
        assert t == tq and k_cache.shape[1] == CB_PAST_ROWS
        pad = 0
        cache_spec = pl.BlockSpec((1, CB_PAST_ROWS, CB_DIM), lambda bi, i: (layer * b + bi, 0, 0))
        k_prev_arr, v_prev_arr, k_prev, v_prev = k_cache, v_cache, cache_spec, cache_spec
    return pl.pallas_call(
        functools.partial(_cb_attn_kernel, tq=tq, pad=pad),
        grid=(b, t // tq),
        in_specs=[cur(C_CB // CB_DIM), k_prev, cur(kcol), v_prev, cur(vcol),
                  pl.BlockSpec(bias.shape, lambda bi, i: (0, 0, 0))],
        out_specs=pl.BlockSpec((1, tq, CB_DIM), lambda bi, i: (bi, i, 0)),
        out_shape=jax.ShapeDtypeStruct((b, t, CB_DIM), BF16),
        compiler_params=_cparams(("parallel", "parallel")),
        name="cb_attn",
    )(proj, k_prev_arr, proj, v_prev_arr, proj, bias)


def _route(logits_t, rb):
    s = _sigmoid(logits_t)
    sb = s + rb
    rows = [sb[e:e + 1, :] for e in range(N_EXPERTS)]
    grp = []
    for g in range(N_GROUPS):
        r = rows[g * EXPERTS_PER_GROUP:(g + 1) * EXPERTS_PER_GROUP]
        best = None
        for a in range(EXPERTS_PER_GROUP):
            for c in range(a + 1, EXPERTS_PER_GROUP):
                pair = r[a] + r[c]
                best = pair if best is None else jnp.maximum(best, pair)
        grp.append(best)
    gmax = functools.reduce(jnp.maximum, grp)
    gsel = jnp.full(gmax.shape, N_GROUPS, jnp.int32)
    for g in reversed(range(N_GROUPS)):
        gsel = jnp.where(grp[g] == gmax, g, gsel)
    sel = []
    for e in range(N_EXPERTS):
        g = e // EXPERTS_PER_GROUP
        rank = jnp.zeros(gmax.shape, jnp.int32)
        for e2 in range(g * EXPERTS_PER_GROUP, (g + 1) * EXPERTS_PER_GROUP):
            if e2 == e:
                continue
            ahead = (rows[e2] >= rows[e]) if e2 < e else (rows[e2] > rows[e])
            rank = rank + ahead.astype(jnp.int32)
        sel.append(jnp.where((gsel == g) & (rank < 2), 1.0, 0.0))
    ssum = functools.reduce(lambda a, c: a + c, [sel[e] * s[e:e + 1, :] for e in range(N_EXPERTS)])
    zero = jnp.zeros(gmax.shape, F32)
    seen, w_lo, w_hi, e_lo, e_hi = zero, zero, zero, zero, zero
    for e in range(N_EXPERTS):
        gate_e = sel[e] * s[e:e + 1, :] / ssum
        first = sel[e] * jnp.where(seen == 0.0, 1.0, 0.0)
        second = sel[e] - first
        w_lo, w_hi = w_lo + first * gate_e, w_hi + second * gate_e
        e_lo, e_hi = e_lo + first * e, e_hi + second * e
        seen = seen + sel[e]
    return sel, w_lo, w_hi, e_lo, e_hi


def _to_slab(ref, val, tok0=0):
    tm = val.shape[0]
    for c in range(val.shape[1] // LANES):
        ref[pl.ds(tok0 * SLAB_ROWS + c, tm, stride=SLAB_ROWS), :] = val[:, c * LANES:(c + 1) * LANES]


def _out_ln1_kernel(oa_p, ob_p, oc_p, x_p, oa_s, ob_s, oc_s, x_s, wa_ref, wb_ref, wc_ref, g_ref, b_ref, rwt_ref, rb_ref,
                    x1_ref, ext_ref, *, blocks_p):
    def body(oa_ref, ob_ref, oc_ref, x_ref):
        tm = x_ref.shape[0]
        halves = [slice(k * (tm // 2), (k + 1) * (tm // 2)) for k in range(2)]
        ys = [_dot(oa_ref[r, :], wa_ref[...]) + _dot(ob_ref[r, :], wb_ref[...]) + _dot(oc_ref[r, :], wc_ref[...])
              for r in halves]
        for r, y in zip(halves, ys):
            x1 = _layer_norm(DEEPNORM_ALPHA * x_ref[r, :] + y, g_ref[...], b_ref[...])
            _to_slab(x1_ref, x1, r.start)
            logits_t = _dot_nt_hi(rwt_ref[...], x1)
            sel, w_lo, w_hi, e_lo, e_hi = _route(logits_t, rb_ref[...])
            rows = sel + [w_lo, w_hi, e_lo, e_hi]
            ext = jnp.concatenate(rows + [jnp.zeros((LANES - len(rows), x1.shape[0]), F32)], 0)
            ext_ref[r, :] = ext.T

    first = pl.program_id(0) < blocks_p
    pl.when(first)(lambda: body(oa_p, ob_p, oc_p, x_p))
    pl.when(jnp.logical_not(first))(lambda: body(oa_s, ob_s, oc_s, x_s))


def _out_ln1(mix_p, x_p, row0_p, mix_s, x_s, row0_s, wa, wb, wc, g, bb, rwt, rb):
    n_p, n_s = mix_p[0].shape[0], mix_s[0].shape[0]
    d = x_p.shape[1]
    tm = min(256, n_p, n_s)
    assert n_p % tm == 0 and n_s % tm == 0 and row0_p % tm == 0 and row0_s % tm == 0
    blocks_p = n_p // tm
    full = lambda a: pl.BlockSpec(a.shape, lambda i: (0,) * a.ndim)
    blk_p = lambda i: jnp.minimum(i, blocks_p - 1)
    blk_s = lambda i: jnp.maximum(i - blocks_p, 0)
    row_p = lambda w, off=0: pl.BlockSpec((tm, w), lambda i: (blk_p(i) + off, 0))
    row_s = lambda w, off=0: pl.BlockSpec((tm, w), lambda i: (blk_s(i) + off, 0))
    n = n_p + n_s
    return pl.pallas_call(
        functools.partial(_out_ln1_kernel, blocks_p=blocks_p),
        grid=(n // tm,),
        in_specs=[row_p(mix_p[0].shape[1]), row_p(mix_p[1].shape[1]), row_p(mix_p[2].shape[1]), row_p(d, row0_p // tm),
                  row_s(mix_s[0].shape[1]), row_s(mix_s[1].shape[1]), row_s(mix_s[2].shape[1]), row_s(d, row0_s // tm),
                  full(wa), full(wb), full(wc), full(g), full(bb), full(rwt), full(rb)],
        out_specs=[pl.BlockSpec((tm * SLAB_ROWS, LANES), lambda i: (i, 0)), pl.BlockSpec((tm, LANES), lambda i: (i, 0))],
        out_shape=[jax.ShapeDtypeStruct((n * SLAB_ROWS, LANES), F32), jax.ShapeDtypeStruct((n, LANES), F32)],
        compiler_params=_cparams(("arbitrary",)),
        name="out_ln1",
    )(*mix_p, x_p, *mix_s, x_s, wa, wb, wc, g, bb, rwt, rb)


def _row_copy(src_hbm, src_row, dst, dst_row, sem):
    return pltpu.make_async_copy(src_hbm.at[pl.ds(pl.multiple_of(src_row * SLAB_ROWS, SLAB_ROWS), SLAB_ROWS)],
                                 dst.at[pl.ds(pl.multiple_of(dst_row * SLAB_ROWS, SLAB_ROWS), SLAB_ROWS)], sem)


def _moe_ffn_kernel(tile_e_ref, nact_ref, src_ref, x_hbm, wg_ref, wu_ref, wd_ref, ys_ref,
                    xg0_ref, xg1_ref, xb_ref, acc_ref, sem):
    i = pl.program_id(0)
    f = pl.program_id(1)
    tm = xb_ref.shape[0]
    bufs = (xg0_ref, xg1_ref)

    def gather_start(tile, slot):
        base = tile * tm

        def issue(r, c):
            _row_copy(x_hbm, src_ref[base + r], bufs[slot], r, sem.at[slot]).start()
            return c

        lax.fori_loop(0, tm, issue, 0, unroll=8)

    def gather_wait(slot):
        pltpu.make_async_copy(x_hbm.at[pl.ds(0, tm * SLAB_ROWS)], bufs[slot], sem.at[slot]).wait()

    @pl.when(i < nact_ref[0])
    def _():
        for slot in range(2):
            @pl.when((f == 0) & (i % 2 == slot))
            def _():
                @pl.when(i == 0)
                def _():
                    gather_start(0, slot)

                @pl.when(i + 1 < nact_ref[0])
                def _():
                    gather_start(i + 1, 1 - slot)

                gather_wait(slot)
                for c in range(xb_ref.shape[1] // LANES):
                    xb_ref[:, c * LANES:(c + 1) * LANES] = bufs[slot][pl.ds(c, tm, stride=SLAB_ROWS), :].astype(BF16)

        xb = xb_ref[...]
        h = (_silu(_dot(xb, wg_ref[0].astype(BF16))) * _dot(xb, wu_ref[0].astype(BF16))).astype(BF16)
        y = _dot(h, wd_ref[0].astype(BF16))

        @pl.when(f == 0)
        def _():
            acc_ref[...] = y

        @pl.when(f > 0)
        def _():
            acc_ref[...] += y

        @pl.when(f == pl.num_programs(1) - 1)
        def _():
            _to_slab(ys_ref, acc_ref[...])

    @pl.when((i >= nact_ref[0]) & (f == pl.num_programs(1) - 1))
    def _():
        ys_ref[...] = jnp.zeros(ys_ref.shape, F32)


def _moe_ffn(tile_e, nact, src, x_slab, wg, wu, wd, tm, layer):
    n_rows = src.shape[0]
    _, d, ff = wg.shape
    tf = min(512, ff)
    nf = ff // tf
    fe = lambda i, f, na: jnp.where(i < na[0], f, nf - 1)
    ex = lambda i, te: layer * N_EXPERTS + te[i]
    return pl.pallas_call(
        _moe_ffn_kernel,
        grid_spec=pltpu.PrefetchScalarGridSpec(
            num_scalar_prefetch=3, grid=(n_rows // tm, nf),
            in_specs=[pl.BlockSpec(memory_space=pl.ANY),
                      pl.BlockSpec((1, d, tf), lambda i, f, te, na, sr: (ex(i, te), 0, fe(i, f, na))),
                      pl.BlockSpec((1, d, tf), lambda i, f, te, na, sr: (ex(i, te), 0, fe(i, f, na))),
                      pl.BlockSpec((1, tf, d), lambda i, f, te, na, sr: (ex(i, te), fe(i, f, na), 0))],
            out_specs=pl.BlockSpec((tm * SLAB_ROWS, LANES), lambda i, f, te, na, sr: (i, 0)),
            scratch_shapes=[pltpu.VMEM((tm * SLAB_ROWS, LANES), F32), pltpu.VMEM((tm * SLAB_ROWS, LANES), F32),
                            pltpu.VMEM((tm, d), BF16), pltpu.VMEM((tm, d), F32), pltpu.SemaphoreType.DMA((2,))]),
        out_shape=jax.ShapeDtypeStruct((n_rows * SLAB_ROWS, LANES), F32),
        compiler_params=_cparams(("arbitrary", "arbitrary")),
        name="moe_ffn",
    )(tile_e, nact, src, x_slab, wg, wu, wd)


def _moe_combine_kernel(pos0_ref, pos1_ref, x1_ref, ext_ref, ys_hbm, g_ref, b_ref, o_ref,
                        buf0_ref, buf1_ref, h_ref, sem):
    tm = o_ref.shape[0]
    base = pl.program_id(0) * tm

    def issue(r, c):
        _row_copy(ys_hbm, pos0_ref[base + r], buf0_ref, r, sem.at[0]).start()
        _row_copy(ys_hbm, pos1_ref[base + r], buf1_ref, r, sem.at[1]).start()
        return c

    lax.fori_loop(0, tm, issue, 0, unroll=8)
    pltpu.make_async_copy(ys_hbm.at[pl.ds(0, tm * SLAB_ROWS)], buf0_ref, sem.at[0]).wait()
    pltpu.make_async_copy(ys_hbm.at[pl.ds(0, tm * SLAB_ROWS)], buf1_ref, sem.at[1]).wait()

    ext = ext_ref[...]
    w_lo = ext[:, EXT_W_LO:EXT_W_LO + 1]
    w_hi = ext[:, EXT_W_LO + 1:EXT_W_LO + 2]
    for c in range(o_ref.shape[1] // LANES):
        rows = pl.ds(c, tm, stride=SLAB_ROWS)
        y = w_lo * buf0_ref[rows, :] + w_hi * buf1_ref[rows, :]
        h_ref[:, c * LANES:(c + 1) * LANES] = DEEPNORM_ALPHA * x1_ref[rows, :] + y
    o_ref[...] = _layer_norm(h_ref[...], g_ref[...], b_ref[...])


def _moe_combine(pos0, pos1, x1_slab, ext, ys, g, bb):
    n = ext.shape[0]
    d = g.shape[1]
    tm = next(c for c in (512, 256, 128, 64) if n % c == 0)
    return pl.pallas_call(
        _moe_combine_kernel,
        grid_spec=pltpu.PrefetchScalarGridSpec(
            num_scalar_prefetch=2, grid=(n // tm,),
            in_specs=[pl.BlockSpec((tm * SLAB_ROWS, LANES), lambda i, p0, p1: (i, 0)),
                      pl.BlockSpec((tm, LANES), lambda i, p0, p1: (i, 0)),
                      pl.BlockSpec(memory_space=pl.ANY),
                      pl.BlockSpec(g.shape, lambda i, p0, p1: (0, 0)),
                      pl.BlockSpec(bb.shape, lambda i, p0, p1: (0, 0))],
            out_specs=pl.BlockSpec((tm, d), lambda i, p0, p1: (i, 0)),
            scratch_shapes=[pltpu.VMEM((tm * SLAB_ROWS, LANES), F32), pltpu.VMEM((tm * SLAB_ROWS, LANES), F32),
                            pltpu.VMEM((tm, d), F32), pltpu.SemaphoreType.DMA((2,))]),
        out_shape=jax.ShapeDtypeStruct((n, d), F32),
        compiler_params=_cparams(("arbitrary",)),
        name="moe_combine",
    )(pos0, pos1, x1_slab, ext, ys, g, bb)


def _moe_src_kernel(pos0_ref, pos1_ref, src_ref, *, n, n_rows):
    def clear(p, c):
        src_ref[p] = 0
        return c

    lax.fori_loop(0, n_rows, clear, 0, unroll=8)

    def put(t, c):
        src_ref[pos0_ref[t]] = t
        src_ref[pos1_ref[t]] = t
        return c

    lax.fori_loop(0, n, put, 0, unroll=8)


def _moe_src(pos0, pos1, n_rows):
    n = pos0.shape[0]
    return pl.pallas_call(
        functools.partial(_moe_src_kernel, n=n, n_rows=n_rows),
        grid_spec=pltpu.PrefetchScalarGridSpec(
            num_scalar_prefetch=2, grid=(1,), in_specs=[],
            out_specs=pl.BlockSpec(memory_space=pltpu.SMEM)),
        out_shape=jax.ShapeDtypeStruct((n_rows,), jnp.int32),
        name="moe_src",
    )(pos0, pos1)


def _moe_ln2(x1_slab, ext, wg, wu, wd, layer, g, bb):
    n = ext.shape[0]
    tm = 512 if n >= 8192 else 256
    n_rows = 2 * n + N_EXPERTS * tm
    sel = (ext[:, EXT_SEL:EXT_SEL + N_EXPERTS] > 0.5).astype(jnp.int32)
    csum = jnp.cumsum(sel, axis=0)
    padded = (csum[-1] + tm - 1) // tm * tm
    seg_end = jnp.cumsum(padded)
    slot = (seg_end - padded)[None, :] + csum - sel
    experts = jnp.arange(N_EXPERTS, dtype=jnp.int32)[None, :]
    e_lo = ext[:, EXT_W_LO + 2].astype(jnp.int32)[:, None]
    e_hi = ext[:, EXT_W_LO + 3].astype(jnp.int32)[:, None]
    pos0 = jnp.sum(jnp.where(experts == e_lo, slot, 0), axis=1)
    pos1 = jnp.sum(jnp.where(experts == e_hi, slot, 0), axis=1)
    nact = seg_end[-1:] // tm
    tiles = jnp.arange(n_rows // tm, dtype=jnp.int32)
    first_row = jnp.minimum(tiles, nact[0] - 1) * tm
    tile_e = jnp.minimum(jnp.sum((seg_end[None, :] <= first_row[:, None]).astype(jnp.int32), axis=1), N_EXPERTS - 1)

    src = _moe_src(pos0, pos1, n_rows)
    ys = _moe_ffn(tile_e, nact, src, x1_slab, wg, wu, wd, tm, layer)
    return _moe_combine(pos0, pos1, x1_slab, ext, ys, g, bb)


def _band_bias_kernel(tab_ref, o_ref):
    size, width = tab_ref.shape[1], o_ref.shape[2]
    entry = lax.broadcasted_iota(jnp.int32, (size, width), 0)
    w = lax.broadcasted_iota(jnp.int32, (size, width), 1)
    tab = tab_ref[...]
    for l in range(o_ref.shape[1]):
        idx = jnp.clip(CB_PAST_ROWS + l - w, -REL_CLIP, REL_CLIP) + REL_CLIP
        o_ref[:, l, :] = _dot_hi(tab, jnp.where(entry == idx, 1.0, 0.0))


def _band_bias(rel_bias):
    depth, h, size = rel_bias.shape
    size_pad = -(-size // LANES) * LANES
    tab = jnp.pad(rel_bias.reshape(depth * h, size), ((0, 0), (0, size_pad - size)))
    width = CB_PAST_ROWS + CHUNK
    out = pl.pallas_call(
        _band_bias_kernel,
        out_shape=jax.ShapeDtypeStruct((depth * h, CHUNK, width), F32),
        name="band_bias",
    )(tab)
    return out.reshape(depth, h, CHUNK, width)


_HALF_ROPE = MLA_ROPE // 2
_SRC_KR = MLA_Q_RANK + MLA_KV_RANK
_SRC_GQKV = _SRC_KR + MLA_ROPE
_SRC_GZ = _SRC_GQKV + GDN_CONV_DIM
_SRC_GBA = _SRC_GZ + GDN_VAL_DIM
_SRC_CB = _SRC_GBA + 2 * GDN_HEADS
W_IN_SEGMENTS = (
    (C_GQKV, _SRC_GQKV, GDN_CONV_DIM), (C_GZ, _SRC_GZ, GDN_VAL_DIM), (C_CQ, 0, MLA_Q_RANK),
    (C_CKV, MLA_Q_RANK, MLA_KV_RANK), (C_KR, _SRC_KR, MLA_ROPE),
    (C_KR + MLA_ROPE, _SRC_KR + _HALF_ROPE, _HALF_ROPE), (C_KR + MLA_ROPE + _HALF_ROPE, _SRC_KR, _HALF_ROPE),
    (C_GBA, _SRC_GBA, 2 * GDN_HEADS), (C_CB, _SRC_CB, 3 * CB_DIM))


def _w_in_relayout_kernel(w_ref, o_ref):
    o_ref[0, :, C_GBA:C_GBA + LANES] = jnp.zeros((o_ref.shape[1], LANES), BF16)
    for dst, src, width in W_IN_SEGMENTS:
        o_ref[0, :, dst:dst + width] = w_ref[0, :, src:src + width].astype(BF16)


def _w_in_relayout(w_in):
    depth, d, width = w_in.shape
    tm = 256
    return pl.pallas_call(
        _w_in_relayout_kernel,
        grid=(depth, d // tm),
        in_specs=[pl.BlockSpec((1, tm, width), lambda l, i: (l, i, 0))],
        out_specs=pl.BlockSpec((1, tm, IN_PAD), lambda l, i: (l, i, 0)),
        out_shape=jax.ShapeDtypeStruct((depth, d, IN_PAD), BF16),
        compiler_params=_cparams(("parallel", "parallel")),
        name="w_in_relayout",
    )(w_in)


def _prep_layer(q_norm_g, w_uq, kv_norm_g, w_uk, w_uv, conv_w, a_log, dt_bias, gdn_norm_g, rel_bias, w_out,
                ln1_g, ln1_b, ln2_g, ln2_b):
    half = MLA_ROPE // 2
    r = w_uq.shape[0]
    wq_nope = w_uq[:, :, :MLA_NOPE].reshape(r, MLA_HEADS * MLA_NOPE).astype(BF16)
    wq_r = w_uq[:, :, MLA_NOPE:]
    wq_rope = wq_r.reshape(r, MLA_HEADS * MLA_ROPE).astype(BF16)
    wq_rope_sw = jnp.concatenate([wq_r[..., half:], wq_r[..., :half]], -1).reshape(r, MLA_HEADS * MLA_ROPE).astype(BF16)
    wuk_t = jnp.transpose(w_uk, (1, 2, 0)).astype(BF16)
    wuv = jnp.transpose(w_uv, (1, 0, 2)).astype(BF16)
    lane_pad = lambda a: jnp.pad(a, (GDN_HEADS, LANES - 2 * GDN_HEADS))[None, :]
    bias = rel_bias
    w_out_b = w_out.astype(BF16)
    na = MLA_HEADS * MLA_V
    return dict(
        q_norm_g=q_norm_g[None, :], kv_norm_g=kv_norm_g[None, :],
        wq_nope=wq_nope, wq_rope=wq_rope, wq_rope_sw=wq_rope_sw, wuk_t=wuk_t, wuv=wuv,
        conv_w=conv_w, alog=lane_pad(a_log), dtb=lane_pad(dt_bias), gnorm=gdn_norm_g[None, :], bias=bias,
        wo_a=w_out_b[:na], wo_b=w_out_b[na:na + GDN_VAL_DIM], wo_c=w_out_b[na + GDN_VAL_DIM:],
        ln1_g=ln1_g[None, :], ln1_b=ln1_b[None, :], ln2_g=ln2_g[None, :], ln2_b=ln2_b[None, :])


def _rope_tables(start, t):
    pos = start + jnp.arange(t, dtype=jnp.int32)
    inv = ROPE_THETA ** (-jnp.arange(0, MLA_ROPE, 2, dtype=F32) / MLA_ROPE)
    ang = pos.astype(F32)[:, None] * inv[None, :]
    cos, sin = jnp.cos(ang), jnp.sin(ang)
    return jnp.concatenate([cos, cos], -1), jnp.concatenate([-sin, sin], -1)


def _mixers(x2d, row0, b, t, p, layer, shared, caches):
    proj = _in_proj(x2d, row0, b * t, shared["w_in"], layer).reshape(b, t, IN_PAD)

    start = 0 if caches is None else caches["ckv"].shape[1]
    cos2, sin2 = _rope_tables(start, t)
    q, ckv_new, krope_new = _mla_prep(proj, cos2, sin2, p["q_norm_g"], p["kv_norm_g"], p["wq_nope"], p["wq_rope"],
                                      p["wq_rope_sw"], p["wuk_t"])
    if caches is None:
        o_a = _mla_attn(q, ckv_new, krope_new, p["wuv"], 0)
        s_past = jnp.zeros((b, GDN_HEADS, GDN_DK, GDN_DV), F32)
        conv_past = jnp.zeros((b, GDN_CONV - 1, GDN_CONV_DIM), F32)
        o_c = _cb_attn(proj, None, None, layer, p["bias"])
    else:
        o_a = _mla_attn_cached(q, caches["ckv"], caches["krope"], layer, ckv_new, krope_new, p["wuv"])
        s_past, conv_past = caches["gdn"][layer], caches["conv"][layer]
        o_c = _cb_attn(proj, caches["cb_k"], caches["cb_v"], layer, p["bias"])
    o_b, s_new, conv_new = _gdn(proj, conv_past, p["conv_w"], p["alog"], p["dtb"], p["gnorm"], s_past)

    keep = min(CB_PAST_ROWS, t)
    cb_new = lambda col: proj[:, t - keep:, col:col + CB_DIM].reshape(b, keep, CB_HEADS, CB_DH)
    state = (ckv_new, krope_new, s_new, conv_new, cb_new(C_CB + CB_DIM), cb_new(C_CB + 2 * CB_DIM))
    return tuple(o.reshape(b * t, -1) for o in (o_a, o_b, o_c)), state


def kernel(x_prompt, x_sample, cache_mla_ckv, cache_mla_krope, state_gdn, state_gdn_conv, cache_cb_k, cache_cb_v,
           w_in, q_norm_g, w_uq, kv_norm_g, w_uk, w_uv, conv_w, a_log, dt_bias, gdn_norm_g, rel_bias, w_out,
           ln1_g, ln1_b, router_w, router_b, w_gate, w_up, w_down, ln2_g, ln2_b):
    depth = w_in.shape[0]
    band_bias = _band_bias(rel_bias)
    layers = [_prep_layer(q_norm_g[l], w_uq[l], kv_norm_g[l], w_uk[l], w_uv[l], conv_w[l], a_log[l],
                          dt_bias[l], gdn_norm_g[l], band_bias[l], w_out[l], ln1_g[l], ln1_b[l],
                          ln2_g[l], ln2_b[l]) for l in range(depth)]
    stack = lambda w: w.reshape((depth * N_EXPERTS,) + w.shape[2:])
    shared = dict(w_in=_w_in_relayout(w_in), rwt=router_w.T, rb=router_b[:, None], w_gate=stack(w_gate), w_up=stack(w_up), w_down=stack(w_down))
    merge = lambda a: a.reshape((a.shape[0] * a.shape[1],) + a.shape[2:])
    rows = cache_cb_k.shape[2]
    caches = dict(ckv=merge(cache_mla_ckv), krope=merge(cache_mla_krope), gdn=state_gdn, conv=state_gdn_conv,
                  cb_k=merge(cache_cb_k).reshape(-1, rows, CB_DIM), cb_v=merge(cache_cb_v).reshape(-1, rows, CB_DIM))

    (bp, tp, d), (bs, ts, _) = x_prompt.shape, x_sample.shape
    n_p, n_s = bp * tp, bs * ts
    x_p, row0_p, x_s, row0_s = x_prompt.reshape(n_p, d), 0, x_sample.reshape(n_s, d), 0
    new_p, new_s = ([], [], [], [], [], []), ([], [], [], [], [], [])
    for l in range(depth):
        p = layers[l]
        mix_p, st_p = _mixers(x_p, row0_p, bp, tp, p, l, shared, None)
        mix_s, st_s = _mixers(x_s, row0_s, bs, ts, p, l, shared, caches)
        x1, ext = _out_ln1(mix_p, x_p, row0_p, mix_s, x_s, row0_s, p["wo_a"], p["wo_b"], p["wo_c"],
                           p["ln1_g"], p["ln1_b"], shared["rwt"], shared["rb"])
        x2 = _moe_ln2(x1, ext, shared["w_gate"], shared["w_up"], shared["w_down"], l, p["ln2_g"], p["ln2_b"])
        x_p, row0_p, x_s, row0_s = x2, 0, x2, n_p
        for new, st in ((new_p, st_p), (new_s, st_s)):
            for lst, a in zip(new, st):
                lst.append(a)
    y_prompt = x2[:n_p].reshape(bp, tp, d)
    y_sample = x2[n_p:].reshape(bs, ts, d)
    return (y_prompt, y_sample, *[jnp.stack(a) for a in new_p], *[jnp.stack(a) for a in new_s])
```

```python
import functools
import math

import jax
import jax.numpy as jnp
import numpy as np
from jax import lax
from jax.experimental import pallas as pl
from jax.experimental.pallas import tpu as pltpu

F32 = jnp.float32
BF16 = jnp.bfloat16

CHUNK = 64
MLA_HEADS = 6
MLA_Q_RANK = 512
MLA_KV_RANK = 256
MLA_NOPE = 128
MLA_ROPE = 64
MLA_V = 128
ROPE_THETA = 10000.0
GDN_HEADS = 6
GDN_DK = 128
GDN_DV = 128
GDN_CONV = 4
GDN_KEY_DIM = GDN_HEADS * GDN_DK
GDN_VAL_DIM = GDN_HEADS * GDN_DV
GDN_CONV_DIM = 2 * GDN_KEY_DIM + GDN_VAL_DIM
CB_HEADS = 4
CB_DH = 128
CB_DIM = CB_HEADS * CB_DH
CB_PAST_ROWS = 8 * CHUNK
REL_CLIP = 256
N_EXPERTS = 16
N_GROUPS = 4
EXPERTS_PER_GROUP = N_EXPERTS // N_GROUPS
DEPTH = 2
DEEPNORM_ALPHA = (2 * DEPTH) ** 0.25

LANES = 128
C_GQKV = 0
C_GZ = C_GQKV + GDN_CONV_DIM
C_CQ = C_GZ + GDN_VAL_DIM
C_CKV = C_CQ + MLA_Q_RANK
C_KR = C_CKV + MLA_KV_RANK
C_GBA = C_KR + 2 * MLA_ROPE
C_CB = C_GBA + LANES
IN_PAD = C_CB + 3 * CB_DIM

SLAB_ROWS = 2048 // LANES
EXT_SEL = 0
EXT_W_LO = N_EXPERTS

VMEM_LIMIT = 56 * 1024 * 1024


def _cparams(sem):
    return pltpu.CompilerParams(dimension_semantics=sem, vmem_limit_bytes=VMEM_LIMIT)


def _dot(a, b):
    return jnp.dot(a, b, preferred_element_type=F32)


def _dot_nt(a, b):
    return lax.dot_general(a, b, (((1,), (1,)), ((), ())), preferred_element_type=F32)


def _dot_hi(a, b):
    return jnp.dot(a, b, preferred_element_type=F32, precision=lax.Precision.HIGHEST)


def _dot_nt_hi(a, b):
    return lax.dot_general(a, b, (((1,), (1,)), ((), ())), preferred_element_type=F32,
                           precision=lax.Precision.HIGHEST)


def _sigmoid(x):
    return 1.0 / (1.0 + jnp.exp(-x))


def _silu(x):
    return x * _sigmoid(x)


def _layer_norm(h, g, b, eps=1e-5):
    mu = jnp.mean(h, -1, keepdims=True)
    d = h - mu
    var = jnp.mean(d * d, -1, keepdims=True)
    return d * lax.rsqrt(var + eps) * g + b


def _rms_norm(x, g, eps=1e-6):
    return x * lax.rsqrt(jnp.mean(x * x, -1, keepdims=True) + eps) * g


def _inproj_kernel(x_ref, w_ref, o_ref, xb_ref):
    @pl.when(pl.program_id(1) == 0)
    def _():
        xb_ref[...] = x_ref[...].astype(BF16)

    o_ref[...] = _dot(xb_ref[...], w_ref[0])


def _in_proj(x2d, row0, n, w, layer):
    d = x2d.shape[1]
    width = w.shape[2]
    tm = min(1024, n)
    tn = 512
    assert row0 % tm == 0 and n % tm == 0
    return pl.pallas_call(
        _inproj_kernel,
        grid=(n // tm, width // tn),
        in_specs=[pl.BlockSpec((tm, d), lambda i, j: (row0 // tm + i, 0)),
                  pl.BlockSpec((1, d, tn), lambda i, j: (layer, 0, j))],
        out_specs=pl.BlockSpec((tm, tn), lambda i, j: (i, j)),
        out_shape=jax.ShapeDtypeStruct((n, width), F32),
        scratch_shapes=[pltpu.VMEM((tm, d), BF16)],
        compiler_params=_cparams(("parallel", "arbitrary")),
        name="in_proj",
    )(x2d, w)


def _mla_prep_kernel(cq_ref, ckv_ref, kr_ref, cosq_ref, sinq_ref, cosk_ref, sink_ref,
                     qg_ref, kvg_ref, wqn_ref, wqr_ref, wqs_ref, wuk_ref,
                     q_ref, ckvn_ref, krn_ref):
    cqn = _rms_norm(cq_ref[0], qg_ref[...]).astype(BF16)
    q_nope = _dot(cqn, wqn_ref[...]).astype(BF16)
    q_rope = _dot(cqn, wqr_ref[...])
    q_rope_sw = _dot(cqn, wqs_ref[...])
    q_rot = (q_rope * cosq_ref[...] + q_rope_sw * sinq_ref[...]).astype(BF16)
    for h in range(MLA_HEADS):
        q_lat = _dot(q_nope[:, h * MLA_NOPE:(h + 1) * MLA_NOPE], wuk_ref[h])
        q_ref[0, h, :, 0:MLA_KV_RANK] = q_lat.astype(BF16)
        q_ref[0, h, :, MLA_KV_RANK:] = q_rot[:, h * MLA_ROPE:(h + 1) * MLA_ROPE]
    ckvn_ref[0] = _rms_norm(ckv_ref[0], kvg_ref[...])
    kr = kr_ref[0]
    krn_ref[0] = kr[:, :MLA_ROPE] * cosk_ref[...] + kr[:, MLA_ROPE:] * sink_ref[...]


def _mla_prep(proj, cos2, sin2, q_norm_g, kv_norm_g, wq_nope, wq_rope, wq_rope_sw, wuk_t):
    b, t, _ = proj.shape
    tm = min(512, t)
    cosq = jnp.tile(cos2, (1, MLA_HEADS))
    sinq = jnp.tile(sin2, (1, MLA_HEADS))
    full = lambda a: pl.BlockSpec(a.shape, lambda bi, i: (0,) * a.ndim)
    row = lambda w: pl.BlockSpec((tm, w), lambda bi, i: (i, 0))
    qd = MLA_KV_RANK + MLA_ROPE
    return pl.pallas_call(
        _mla_prep_kernel,
        grid=(b, t // tm),
        in_specs=[pl.BlockSpec((1, tm, MLA_Q_RANK), lambda bi, i: (bi, i, C_CQ // MLA_Q_RANK)),
                  pl.BlockSpec((1, tm, MLA_KV_RANK), lambda bi, i: (bi, i, C_CKV // MLA_KV_RANK)),
                  pl.BlockSpec((1, tm, 2 * MLA_ROPE), lambda bi, i: (bi, i, C_KR // (2 * MLA_ROPE))),
                  row(MLA_HEADS * MLA_ROPE), row(MLA_HEADS * MLA_ROPE), row(MLA_ROPE), row(MLA_ROPE),
                  full(q_norm_g), full(kv_norm_g), full(wq_nope), full(wq_rope), full(wq_rope_sw),
                  full(wuk_t)],
        out_specs=[pl.BlockSpec((1, MLA_HEADS, tm, qd), lambda bi, i: (bi, 0, i, 0)),
                   pl.BlockSpec((1, tm, MLA_KV_RANK), lambda bi, i: (bi, i, 0)),
                   pl.BlockSpec((1, tm, MLA_ROPE), lambda bi, i: (bi, i, 0))],
        out_shape=[jax.ShapeDtypeStruct((b, MLA_HEADS, t, qd), BF16),
                   jax.ShapeDtypeStruct((b, t, MLA_KV_RANK), F32),
                   jax.ShapeDtypeStruct((b, t, MLA_ROPE), F32)],
        compiler_params=_cparams(("parallel", "parallel")),
        name="mla_prep",
    )(proj, proj, proj, cosq, sinq, cos2, sin2, q_norm_g, kv_norm_g, wq_nope, wq_rope, wq_rope_sw, wuk_t)


def _mla_attn_kernel(qi_ref, kj_ref, q_ref, ckv_ref, kr_ref, wuv_ref, o_ref, m_ref, l_ref, acc_ref, *, start, tq, tk):
    step = pl.program_id(1)
    i = qi_ref[step]
    j = kj_ref[step]

    @pl.when(j == 0)
    def _():
        m_ref[...] = jnp.full(m_ref.shape, -jnp.inf, F32)
        l_ref[...] = jnp.zeros(l_ref.shape, F32)
        acc_ref[...] = jnp.zeros(acc_ref.shape, F32)

    ckv = ckv_ref[0].astype(BF16)
    kr = kr_ref[0].astype(BF16)
    scale = (MLA_NOPE + MLA_ROPE) ** -0.5

    hpg = 1 if tq >= 256 else MLA_HEADS
    grows = hpg * tq

    def update(masked):
        if masked:
            q_pos = start + i * tq + lax.broadcasted_iota(jnp.int32, (grows, 1), 0) % tq
            k_pos = j * tk + lax.broadcasted_iota(jnp.int32, (1, tk), 1)
            visible = k_pos // CHUNK <= q_pos // CHUNK
        def scores(g):
            q = q_ref[0, g] if hpg == 1 else q_ref[0].reshape(grows, MLA_KV_RANK + MLA_ROPE)
            return _dot_nt(q[:, :MLA_KV_RANK], ckv) + _dot_nt(q[:, MLA_KV_RANK:], kr)

        groups = MLA_HEADS // hpg
        s_next = scores(0)
        for g in range(groups):
            s, s_next = s_next, (scores(g + 1) if g + 1 < groups else None)
            s = s * scale
            if masked:
                s = jnp.where(visible, s, -jnp.inf)
            rows = slice(g * grows, (g + 1) * grows)
            m_old = m_ref[rows]
            m_new = jnp.maximum(m_old, jnp.max(s, -1, keepdims=True))
            alpha = jnp.exp(m_old - m_new)
            p = jnp.exp(s - m_new)
            l_ref[rows] = alpha * l_ref[rows] + jnp.sum(p, -1, keepdims=True)
            acc_ref[rows] = alpha * acc_ref[rows] + _dot(p.astype(BF16), ckv)
            m_ref[rows] = m_new

    needs_mask = (j * tk + tk - 1) // CHUNK > (start + i * tq) // CHUNK
    pl.when(needs_mask)(lambda: update(True))
    pl.when(jnp.logical_not(needs_mask))(lambda: update(False))

    @pl.when(j == _last_kv_block(start, i, tq, tk))
    def _():
        o_lat = (acc_ref[...] / l_ref[...]).astype(BF16)
        for h in range(MLA_HEADS):
            o_ref[0, :, h * MLA_V:(h + 1) * MLA_V] = _dot(o_lat[h * tq:(h + 1) * tq], wuv_ref[h]).astype(BF16)


MLA_SUB_KEYS = 512


def _mla_attn_cached_kernel(q_ref, pckv_ref, pkr_ref, nckv_ref, nkr_ref, wuv_ref, o_ref, m_ref, l_ref, acc_ref, *, nb):
    j = pl.program_id(1)
    tq = nckv_ref.shape[1]
    rows = MLA_HEADS * tq

    @pl.when(j == 0)
    def _():
        m_ref[...] = jnp.full(m_ref.shape, -jnp.inf, F32)
        l_ref[...] = jnp.zeros(l_ref.shape, F32)
        acc_ref[...] = jnp.zeros(acc_ref.shape, F32)

    def update(ckv_ref, kr_ref):
        keys = ckv_ref.shape[1]
        sub = MLA_SUB_KEYS if keys % MLA_SUB_KEYS == 0 else keys
        q = q_ref[0].reshape(rows, MLA_KV_RANK + MLA_ROPE)

        def scores(u):
            ckv = ckv_ref[0, u * sub:(u + 1) * sub, :].astype(BF16)
            kr = kr_ref[0, u * sub:(u + 1) * sub, :].astype(BF16)
            return _dot_nt(q[:, :MLA_KV_RANK], ckv) + _dot_nt(q[:, MLA_KV_RANK:], kr), ckv

        nxt = scores(0)
        for u in range(keys // sub):
            (s, ckv), nxt = nxt, (scores(u + 1) if (u + 1) * sub < keys else None)
            s = s * ((MLA_NOPE + MLA_ROPE) ** -0.5)
            m_old = m_ref[...]
            m_new = jnp.maximum(m_old, jnp.max(s, -1, keepdims=True))
            alpha = jnp.exp(m_old - m_new)
            p = jnp.exp(s - m_new)
            l_ref[...] = alpha * l_ref[...] + jnp.sum(p, -1, keepdims=True)
            acc_ref[...] = alpha * acc_ref[...] + _dot(p.astype(BF16), ckv)
            m_ref[...] = m_new

    pl.when(j < nb)(lambda: update(pckv_ref, pkr_ref))

    @pl.when(j == nb)
    def _():
        update(nckv_ref, nkr_ref)
        o_lat = (acc_ref[...] / l_ref[...]).astype(BF16)
        for h in range(MLA_HEADS):
            o_ref[0, :, h * MLA_V:(h + 1) * MLA_V] = _dot(o_lat[h * tq:(h + 1) * tq], wuv_ref[h]).astype(BF16)


def _mla_attn_cached(q, ckv_cache, kr_cache, layer, ckv_new, kr_new, wuv):
    b, _, t, qd = q.shape
    past = ckv_cache.shape[1]
    assert t == CHUNK and past % CHUNK == 0
    tk = next((c for c in (2048, 1024) if past % c == 0), None) or _pick_tk(past)
    nb = past // tk
    past_map = lambda bi, j: (layer * b + bi, jnp.minimum(j, nb - 1), 0)
    new_map = lambda bi, j: (bi, 0, 0)
    return pl.pallas_call(
        functools.partial(_mla_attn_cached_kernel, nb=nb),
        grid=(b, nb + 1),
        in_specs=[pl.BlockSpec((1, MLA_HEADS, t, qd), lambda bi, j: (bi, 0, 0, 0)),
                  pl.BlockSpec((1, tk, MLA_KV_RANK), past_map),
                  pl.BlockSpec((1, tk, MLA_ROPE), past_map),
                  pl.BlockSpec((1, t, MLA_KV_RANK), new_map),
                  pl.BlockSpec((1, t, MLA_ROPE), new_map),
                  pl.BlockSpec(wuv.shape, lambda bi, j: (0, 0, 0))],
        out_specs=pl.BlockSpec((1, t, MLA_HEADS * MLA_V), lambda bi, j: (bi, 0, 0)),
        out_shape=jax.ShapeDtypeStruct((b, t, MLA_HEADS * MLA_V), BF16),
        scratch_shapes=[pltpu.VMEM((MLA_HEADS * t, 1), F32),
                        pltpu.VMEM((MLA_HEADS * t, 1), F32),
                        pltpu.VMEM((MLA_HEADS * t, MLA_KV_RANK), F32)],
        compiler_params=_cparams(("parallel", "arbitrary")),
        name="mla_attn_cached",
    )(q, ckv_cache, kr_cache, ckv_new, kr_new, wuv)


def _last_kv_block(start, i, tq, tk):
    return ((start + i * tq + tq - 1) // CHUNK * CHUNK) // tk


def _pick_tk(s):
    for cand in (512, 1024, 832, 768, 640, 576, 448, 384, 320, 256, 192, 128, 64):
        if s % cand == 0:
            return cand
    raise ValueError(f"unsupported key length {s}")


def _mla_attn(q, ckv_all, kr_all, wuv, start):
    b, _, t, qd = q.shape
    s = ckv_all.shape[1]
    tq = min(256, t)
    tk = _pick_tk(s)
    assert tk % CHUNK == 0 and s % tk == 0 and t % tq == 0
    pairs = [(i, j) for i in range(t // tq) for j in range(_last_kv_block(start, i, tq, tk) + 1)]
    qi = jnp.asarray(np.array([p[0] for p in pairs], np.int32))
    kj = jnp.asarray(np.array([p[1] for p in pairs], np.int32))
    kv_map = lambda bi, st, qi_r, kj_r: (bi, kj_r[st], 0)
    return pl.pallas_call(
        functools.partial(_mla_attn_kernel, start=start, tq=tq, tk=tk),
        grid_spec=pltpu.PrefetchScalarGridSpec(
            num_scalar_prefetch=2, grid=(b, len(pairs)),
            in_specs=[pl.BlockSpec((1, MLA_HEADS, tq, qd), lambda bi, st, qi_r, kj_r: (bi, 0, qi_r[st], 0)),
                      pl.BlockSpec((1, tk, MLA_KV_RANK), kv_map),
                      pl.BlockSpec((1, tk, MLA_ROPE), kv_map),
                      pl.BlockSpec(wuv.shape, lambda bi, st, qi_r, kj_r: (0, 0, 0))],
            out_specs=pl.BlockSpec((1, tq, MLA_HEADS * MLA_V), lambda bi, st, qi_r, kj_r: (bi, qi_r[st], 0)),
            scratch_shapes=[pltpu.VMEM((MLA_HEADS * tq, 1), F32),
                            pltpu.VMEM((MLA_HEADS * tq, 1), F32),
                            pltpu.VMEM((MLA_HEADS * tq, MLA_KV_RANK), F32)]),
        out_shape=jax.ShapeDtypeStruct((b, t, MLA_HEADS * MLA_V), BF16),
        compiler_params=_cparams(("parallel", "arbitrary")),
        name="mla_attn",
    )(qi, kj, q, ckv_all, kr_all, wuv)


def _split_bf16(x):
    hi = x.astype(BF16)
    return hi, (x - hi.astype(F32)).astype(BF16)


def _unit_lower_solve_many(a_list, rhs_list):
    n = rhs_list[0].shape[1]
    levels = int(math.log2(CHUNK))
    xs, ps = list(rhs_list), list(a_list)
    for lvl in range(levels):
        for h in range(len(xs)):
            p_hi, p_lo = _split_bf16(ps[h])
            lhs = jnp.concatenate([p_hi, p_hi, p_lo], 1)
            if lvl < levels - 1:
                r_hi, r_lo = _split_bf16(jnp.concatenate([xs[h], ps[h]], 1))
                both = _dot(lhs, jnp.concatenate([r_hi, r_lo, r_hi], 0))
                px, ps[h] = both[:, :n], both[:, n:]
            else:
                r_hi, r_lo = _split_bf16(xs[h])
                px = _dot(lhs, jnp.concatenate([r_hi, r_lo, r_hi], 0))
            xs[h] = xs[h] - px if lvl == 0 else xs[h] + px
    return xs


def _gdn_kernel(qkv_ref, gz_ref, gba_ref, convp_ref, convw_ref, alog_ref, dtb_ref, gnorm_ref, s0_ref,
                ob_ref, snew_ref, convn_ref, s_scr, ext_scr, *, n_chunks):
    n = pl.program_id(1)
    L = CHUNK
    tail = 8

    @pl.when(n == 0)
    def _():
        s_scr[...] = s0_ref[0]
        ext_scr[0:tail, :] = jnp.zeros((tail, GDN_CONV_DIM), F32)
        ext_scr[tail - (GDN_CONV - 1):tail, :] = convp_ref[0]

    cur = qkv_ref[0]
    ext_scr[tail:tail + L, :] = cur
    w = convw_ref[...]
    conv = ext_scr[tail - 3:tail - 3 + L, :] * w[0:1]
    conv = conv + ext_scr[tail - 2:tail - 2 + L, :] * w[1:2]
    conv = conv + ext_scr[tail - 1:tail - 1 + L, :] * w[2:3]
    conv = conv + cur * w[3:4]
    conv = _silu(conv)

    @pl.when(n == n_chunks - 1)
    def _():
        convn_ref[0] = ext_scr[tail + L - (GDN_CONV - 1):tail + L, :]

    ext_scr[0:tail, :] = cur[L - tail:, :]

    gba = gba_ref[0]
    beta_all = _sigmoid(gba)
    z = gba + dtb_ref[...]
    softplus = jnp.maximum(z, 0.0) + jnp.log1p(jnp.exp(-jnp.abs(z)))
    g_all = -jnp.exp(alog_ref[...]) * softplus
    ri = lax.broadcasted_iota(jnp.int32, (L, L), 0)
    ci = lax.broadcasted_iota(jnp.int32, (L, L), 1)
    incl = ci <= ri
    strict = ci < ri
    g_cum = _dot_hi(incl.astype(F32), g_all)
    g_cum_t = g_cum.T

    def l2n(x):
        return x * lax.rsqrt(jnp.sum(x * x, -1, keepdims=True) + 1e-6)

    heads = range(GDN_HEADS)
    q = [l2n(conv[:, h * GDN_DK:(h + 1) * GDN_DK]) * (GDN_DK ** -0.5) for h in heads]
    k = [l2n(conv[:, GDN_KEY_DIM + h * GDN_DK:GDN_KEY_DIM + (h + 1) * GDN_DK]) for h in heads]
    v = [conv[:, 2 * GDN_KEY_DIM + h * GDN_DV:2 * GDN_KEY_DIM + (h + 1) * GDN_DV] for h in heads]
    beta = [beta_all[:, h:h + 1] for h in heads]
    gc = [g_cum[:, GDN_HEADS + h:GDN_HEADS + h + 1] for h in heads]
    dmat = [jnp.exp(jnp.where(incl, gc[h] - g_cum_t[GDN_HEADS + h:GDN_HEADS + h + 1, :], -jnp.inf)) for h in heads]
    eg = [jnp.exp(gc[h]) for h in heads]
    qk_kk = []
    for h in heads:
        kb = k[h].astype(BF16)
        qk_kk.append(_dot_nt(jnp.concatenate([q[h].astype(BF16), kb], 0), kb))
    a_mat = [jnp.where(strict, beta[h] * qk_kk[h][L:] * dmat[h], 0.0) for h in heads]
    rhs = [jnp.concatenate([beta[h] * v[h], (beta[h] * eg[h]) * k[h]], -1) for h in heads]
    sol = _unit_lower_solve_many(a_mat, rhs)
    s_old = [s_scr[h] for h in heads]
    wq_s = [_dot(jnp.concatenate([sol[h][:, GDN_DV:], q[h]], 0).astype(BF16), s_old[h].astype(BF16)) for h in heads]
    upd = []
    for h in heads:
        delta = sol[h][:, :GDN_DV] - wq_s[h][:L]
        kd = k[h] * jnp.exp(gc[h][L - 1:L, :] - gc[h])
        lhs = jnp.concatenate([qk_kk[h][:L] * dmat[h], kd.T], 0)
        upd.append(_dot(lhs.astype(BF16), delta.astype(BF16)))
    for h in heads:
        o = eg[h] * wq_s[h][L:] + upd[h][:L]
        s_scr[h] = jnp.exp(gc[h][L - 1:L, :]) * s_old[h] + upd[h][L:]
        gz = gz_ref[0, :, h * GDN_DV:(h + 1) * GDN_DV]
        ob_ref[0, :, h * GDN_DV:(h + 1) * GDN_DV] = (_rms_norm(o, gnorm_ref[...]) * _silu(gz)).astype(BF16)

    @pl.when(n == n_chunks - 1)
    def _():
        snew_ref[0] = s_scr[...]


def _gdn(proj, conv_past, conv_w, alog128, dtb128, gnorm, s0):
    b, t, _ = proj.shape
    L = CHUNK
    full = lambda a: pl.BlockSpec(a.shape, lambda bi, n: (0,) * a.ndim)
    return pl.pallas_call(
        functools.partial(_gdn_kernel, n_chunks=t // L),
        grid=(b, t // L),
        in_specs=[pl.BlockSpec((1, L, GDN_CONV_DIM), lambda bi, n: (bi, n, C_GQKV // GDN_CONV_DIM)),
                  pl.BlockSpec((1, L, GDN_VAL_DIM), lambda bi, n: (bi, n, C_GZ // GDN_VAL_DIM)),
                  pl.BlockSpec((1, L, LANES), lambda bi, n: (bi, n, C_GBA // LANES)),
                  pl.BlockSpec((1, GDN_CONV - 1, GDN_CONV_DIM), lambda bi, n: (bi, 0, 0)),
                  full(conv_w), full(alog128), full(dtb128), full(gnorm),
                  pl.BlockSpec((1, GDN_HEADS, GDN_DK, GDN_DV), lambda bi, n: (bi, 0, 0, 0))],
        out_specs=[pl.BlockSpec((1, L, GDN_VAL_DIM), lambda bi, n: (bi, n, 0)),
                   pl.BlockSpec((1, GDN_HEADS, GDN_DK, GDN_DV), lambda bi, n: (bi, 0, 0, 0)),
                   pl.BlockSpec((1, GDN_CONV - 1, GDN_CONV_DIM), lambda bi, n: (bi, 0, 0))],
        out_shape=[jax.ShapeDtypeStruct((b, t, GDN_VAL_DIM), BF16),
                   jax.ShapeDtypeStruct((b, GDN_HEADS, GDN_DK, GDN_DV), F32),
                   jax.ShapeDtypeStruct((b, GDN_CONV - 1, GDN_CONV_DIM), F32)],
        scratch_shapes=[pltpu.VMEM((GDN_HEADS, GDN_DK, GDN_DV), F32),
                        pltpu.VMEM((8 + L, GDN_CONV_DIM), F32)],
        compiler_params=_cparams(("parallel", "arbitrary")),
        name="gdn",
    )(proj, proj, proj, conv_past, conv_w, alog128, dtb128, gnorm, s0)


def _cb_attn_kernel(q_ref, kprev_ref, kcur_ref, vprev_ref, vcur_ref, bias_ref, o_ref, *, tq, pad):
    i = pl.program_id(1)
    L = CHUNK
    width = CB_PAST_ROWS + L
    scale = CB_DH ** -0.5
    for c in range(tq // L):
        lo = c * L
        kwin = jnp.concatenate([kprev_ref[0, lo:, :], kcur_ref[0, :lo + L, :]], 0).astype(BF16)
        vwin = jnp.concatenate([vprev_ref[0, lo:, :], vcur_ref[0, :lo + L, :]], 0).astype(BF16)
        q = q_ref[0, lo:lo + L, :].astype(BF16)
        row = i * tq + lo + lax.broadcasted_iota(jnp.int32, (1, width), 1)
        valid = row >= pad
        heads = [slice(h * CB_DH, (h + 1) * CB_DH) for h in range(CB_HEADS)]
        scores = [_dot_nt(q[:, hs], kwin[:, hs]) for hs in heads]
        probs = []
        for h in range(CB_HEADS):
            s = scores[h] * scale + bias_ref[h]
            s = jnp.where(valid, s, -jnp.inf)
            m = jnp.max(s, -1, keepdims=True)
            p = jnp.exp(s - m)
            probs.append((p / jnp.sum(p, -1, keepdims=True)).astype(BF16))
        for h, hs in enumerate(heads):
            o_ref[0, lo:lo + L, hs] = _dot(probs[h], vwin[:, hs]).astype(BF16)


def _cb_attn(proj, k_cache, v_cache, layer, bias):
    b, t, _ = proj.shape
    tq = min(CB_PAST_ROWS, t)
    kcol, vcol = C_CB // CB_DIM + 1, C_CB // CB_DIM + 2
    cur = lambda col: pl.BlockSpec((1, tq, CB_DIM), lambda bi, i: (bi, i, col))
    if k_cache is None:
        assert tq == CB_PAST_ROWS and t % tq == 0
        pad = CB_PAST_ROWS
        prev = lambda col: pl.BlockSpec((1, tq, CB_DIM), lambda bi, i: (bi, jnp.maximum(i - 1, 0), col))
        k_prev_arr, v_prev_arr, k_prev, v_prev = proj, proj, prev(kcol), prev(vcol)
    else:
        assert t == tq and k_cache.shape[1] == CB_PAST_ROWS
        pad = 0
        cache_spec = pl.BlockSpec((1, CB_PAST_ROWS, CB_DIM), lambda bi, i: (layer * b + bi, 0, 0))
        k_prev_arr, v_prev_arr, k_prev, v_prev = k_cache, v_cache, cache_spec, cache_spec
    return pl.pallas_call(
        functools.partial(_cb_attn_kernel, tq=tq, pad=pad),
        grid=(b, t // tq),
        in_specs=[cur(C_CB // CB_DIM), k_prev, cur(kcol), v_prev, cur(vcol),
                  pl.BlockSpec(bias.shape, lambda bi, i: (0, 0, 0))],
        out_specs=pl.BlockSpec((1, tq, CB_DIM), lambda bi, i: (bi, i, 0)),
        out_shape=jax.ShapeDtypeStruct((b, t, CB_DIM), BF16),
        compiler_params=_cparams(("parallel", "parallel")),
        name="cb_attn",
    )(proj, k_prev_arr, proj, v_prev_arr, proj, bias)


def _route(logits_t, rb):
    s = _sigmoid(logits_t)
    sb = s + rb
    rows = [sb[e:e + 1, :] for e in range(N_EXPERTS)]
    grp = []
    for g in range(N_GROUPS):
        r = rows[g * EXPERTS_PER_GROUP:(g + 1) * EXPERTS_PER_GROUP]
        best = None
        for a in range(EXPERTS_PER_GROUP):
            for c in range(a + 1, EXPERTS_PER_GROUP):
                pair = r[a] + r[c]
                best = pair if best is None else jnp.maximum(best, pair)
        grp.append(best)
    gmax = functools.reduce(jnp.maximum, grp)
    gsel = jnp.full(gmax.shape, N_GROUPS, jnp.int32)
    for g in reversed(range(N_GROUPS)):
        gsel = jnp.where(grp[g] == gmax, g, gsel)
    sel = []
    for e in range(N_EXPERTS):
        g = e // EXPERTS_PER_GROUP
        rank = jnp.zeros(gmax.shape, jnp.int32)
        for e2 in range(g * EXPERTS_PER_GROUP, (g + 1) * EXPERTS_PER_GROUP):
            if e2 == e:
                continue
            ahead = (rows[e2] >= rows[e]) if e2 < e else (rows[e2] > rows[e])
            rank = rank + ahead.astype(jnp.int32)
        sel.append(jnp.where((gsel == g) & (rank < 2), 1.0, 0.0))
    ssum = functools.reduce(lambda a, c: a + c, [sel[e] * s[e:e + 1, :] for e in range(N_EXPERTS)])
    zero = jnp.zeros(gmax.shape, F32)
    seen, w_lo, w_hi, e_lo, e_hi = zero, zero, zero, zero, zero
    for e in range(N_EXPERTS):
        gate_e = sel[e] * s[e:e + 1, :] / ssum
        first = sel[e] * jnp.where(seen == 0.0, 1.0, 0.0)
        second = sel[e] - first
        w_lo, w_hi = w_lo + first * gate_e, w_hi + second * gate_e
        e_lo, e_hi = e_lo + first * e, e_hi + second * e
        seen = seen + sel[e]
    return sel, w_lo, w_hi, e_lo, e_hi


def _to_slab(ref, val, tok0=0):
    tm = val.shape[0]
    for c in range(val.shape[1] // LANES):
        ref[pl.ds(tok0 * SLAB_ROWS + c, tm, stride=SLAB_ROWS), :] = val[:, c * LANES:(c + 1) * LANES]


def _out_ln1_kernel(oa_p, ob_p, oc_p, x_p, oa_s, ob_s, oc_s, x_s, wa_ref, wb_ref, wc_ref, g_ref, b_ref, rwt_ref, rb_ref,
                    x1_ref, ext_ref, *, blocks_p):
    def body(oa_ref, ob_ref, oc_ref, x_ref):
        tm = x_ref.shape[0]
        halves = [slice(k * (tm // 2), (k + 1) * (tm // 2)) for k in range(2)]
        ys = [_dot(oa_ref[r, :], wa_ref[...]) + _dot(ob_ref[r, :], wb_ref[...]) + _dot(oc_ref[r, :], wc_ref[...])
              for r in halves]
        for r, y in zip(halves, ys):
            x1 = _layer_norm(DEEPNORM_ALPHA * x_ref[r, :] + y, g_ref[...], b_ref[...])
            _to_slab(x1_ref, x1, r.start)
            logits_t = _dot_nt_hi(rwt_ref[...], x1)
            sel, w_lo, w_hi, e_lo, e_hi = _route(logits_t, rb_ref[...])
            rows = sel + [w_lo, w_hi, e_lo, e_hi]
            ext = jnp.concatenate(rows + [jnp.zeros((LANES - len(rows), x1.shape[0]), F32)], 0)
            ext_ref[r, :] = ext.T

    first = pl.program_id(0) < blocks_p
    pl.when(first)(lambda: body(oa_p, ob_p, oc_p, x_p))
    pl.when(jnp.logical_not(first))(lambda: body(oa_s, ob_s, oc_s, x_s))


def _out_ln1(mix_p, x_p, row0_p, mix_s, x_s, row0_s, wa, wb, wc, g, bb, rwt, rb):
    n_p, n_s = mix_p[0].shape[0], mix_s[0].shape[0]
    d = x_p.shape[1]
    tm = min(256, n_p, n_s)
    assert n_p % tm == 0 and n_s % tm == 0 and row0_p % tm == 0 and row0_s % tm == 0
    blocks_p = n_p // tm
    full = lambda a: pl.BlockSpec(a.shape, lambda i: (0,) * a.ndim)
    blk_p = lambda i: jnp.minimum(i, blocks_p - 1)
    blk_s = lambda i: jnp.maximum(i - blocks_p, 0)
    row_p = lambda w, off=0: pl.BlockSpec((tm, w), lambda i: (blk_p(i) + off, 0))
    row_s = lambda w, off=0: pl.BlockSpec((tm, w), lambda i: (blk_s(i) + off, 0))
    n = n_p + n_s
    return pl.pallas_call(
        functools.partial(_out_ln1_kernel, blocks_p=blocks_p),
        grid=(n // tm,),
        in_specs=[row_p(mix_p[0].shape[1]), row_p(mix_p[1].shape[1]), row_p(mix_p[2].shape[1]), row_p(d, row0_p // tm),
                  row_s(mix_s[0].shape[1]), row_s(mix_s[1].shape[1]), row_s(mix_s[2].shape[1]), row_s(d, row0_s // tm),
                  full(wa), full(wb), full(wc), full(g), full(bb), full(rwt), full(rb)],
        out_specs=[pl.BlockSpec((tm * SLAB_ROWS, LANES), lambda i: (i, 0)), pl.BlockSpec((tm, LANES), lambda i: (i, 0))],
        out_shape=[jax.ShapeDtypeStruct((n * SLAB_ROWS, LANES), F32), jax.ShapeDtypeStruct((n, LANES), F32)],
        compiler_params=_cparams(("arbitrary",)),
        name="out_ln1",
    )(*mix_p, x_p, *mix_s, x_s, wa, wb, wc, g, bb, rwt, rb)


def _row_copy(src_hbm, src_row, dst, dst_row, sem):
    return pltpu.make_async_copy(src_hbm.at[pl.ds(pl.multiple_of(src_row * SLAB_ROWS, SLAB_ROWS), SLAB_ROWS)],
                                 dst.at[pl.ds(pl.multiple_of(dst_row * SLAB_ROWS, SLAB_ROWS), SLAB_ROWS)], sem)


def _moe_ffn_kernel(tile_e_ref, nact_ref, src_ref, x_hbm, wg_ref, wu_ref, wd_ref, ys_ref,
                    xg0_ref, xg1_ref, xb_ref, acc_ref, sem):
    i = pl.program_id(0)
    f = pl.program_id(1)
    tm = xb_ref.shape[0]
    bufs = (xg0_ref, xg1_ref)

    def gather_start(tile, slot):
        base = tile * tm

        def issue(r, c):
            _row_copy(x_hbm, src_ref[base + r], bufs[slot], r, sem.at[slot]).start()
            return c

        lax.fori_loop(0, tm, issue, 0, unroll=8)

    def gather_wait(slot):
        pltpu.make_async_copy(x_hbm.at[pl.ds(0, tm * SLAB_ROWS)], bufs[slot], sem.at[slot]).wait()

    @pl.when(i < nact_ref[0])
    def _():
        for slot in range(2):
            @pl.when((f == 0) & (i % 2 == slot))
            def _():
                @pl.when(i == 0)
                def _():
                    gather_start(0, slot)

                @pl.when(i + 1 < nact_ref[0])
                def _():
                    gather_start(i + 1, 1 - slot)

                gather_wait(slot)
                for c in range(xb_ref.shape[1] // LANES):
                    xb_ref[:, c * LANES:(c + 1) * LANES] = bufs[slot][pl.ds(c, tm, stride=SLAB_ROWS), :].astype(BF16)

        xb = xb_ref[...]
        h = (_silu(_dot(xb, wg_ref[0].astype(BF16))) * _dot(xb, wu_ref[0].astype(BF16))).astype(BF16)
        y = _dot(h, wd_ref[0].astype(BF16))

        @pl.when(f == 0)
        def _():
            acc_ref[...] = y

        @pl.when(f > 0)
        def _():
            acc_ref[...] += y

        @pl.when(f == pl.num_programs(1) - 1)
        def _():
            _to_slab(ys_ref, acc_ref[...])

    @pl.when((i >= nact_ref[0]) & (f == pl.num_programs(1) - 1))
    def _():
        ys_ref[...] = jnp.zeros(ys_ref.shape, F32)


def _moe_ffn(tile_e, nact, src, x_slab, wg, wu, wd, tm, layer):
    n_rows = src.shape[0]
    _, d, ff = wg.shape
    tf = min(512, ff)
    nf = ff // tf
    fe = lambda i, f, na: jnp.where(i < na[0], f, nf - 1)
    ex = lambda i, te: layer * N_EXPERTS + te[i]
    return pl.pallas_call(
        _moe_ffn_kernel,
        grid_spec=pltpu.PrefetchScalarGridSpec(
            num_scalar_prefetch=3, grid=(n_rows // tm, nf),
            in_specs=[pl.BlockSpec(memory_space=pl.ANY),
                      pl.BlockSpec((1, d, tf), lambda i, f, te, na, sr: (ex(i, te), 0, fe(i, f, na))),
                      pl.BlockSpec((1, d, tf), lambda i, f, te, na, sr: (ex(i, te), 0, fe(i, f, na))),
                      pl.BlockSpec((1, tf, d), lambda i, f, te, na, sr: (ex(i, te), fe(i, f, na), 0))],
            out_specs=pl.BlockSpec((tm * SLAB_ROWS, LANES), lambda i, f, te, na, sr: (i, 0)),
            scratch_shapes=[pltpu.VMEM((tm * SLAB_ROWS, LANES), F32), pltpu.VMEM((tm * SLAB_ROWS, LANES), F32),
                            pltpu.VMEM((tm, d), BF16), pltpu.VMEM((tm, d), F32), pltpu.SemaphoreType.DMA((2,))]),
        out_shape=jax.ShapeDtypeStruct((n_rows * SLAB_ROWS, LANES), F32),
        compiler_params=_cparams(("arbitrary", "arbitrary")),
        name="moe_ffn",
    )(tile_e, nact, src, x_slab, wg, wu, wd)


def _moe_combine_kernel(pos0_ref, pos1_ref, x1_ref, ext_ref, ys_hbm, g_ref, b_ref, o_ref,
                        buf0_ref, buf1_ref, h_ref, sem):
    tm = o_ref.shape[0]
    base = pl.program_id(0) * tm

    def issue(r, c):
        _row_copy(ys_hbm, pos0_ref[base + r], buf0_ref, r, sem.at[0]).start(priority=0)
        _row_copy(ys_hbm, pos1_ref[base + r], buf1_ref, r, sem.at[1]).start(priority=1)
        return c

    lax.fori_loop(0, tm, issue, 0, unroll=8)
    pltpu.make_async_copy(ys_hbm.at[pl.ds(0, tm * SLAB_ROWS)], buf0_ref, sem.at[0]).wait()
    pltpu.make_async_copy(ys_hbm.at[pl.ds(0, tm * SLAB_ROWS)], buf1_ref, sem.at[1]).wait()

    ext = ext_ref[...]
    w_lo = ext[:, EXT_W_LO:EXT_W_LO + 1]
    w_hi = ext[:, EXT_W_LO + 1:EXT_W_LO + 2]
    for c in range(o_ref.shape[1] // LANES):
        rows = pl.ds(c, tm, stride=SLAB_ROWS)
        y = w_lo * buf0_ref[rows, :] + w_hi * buf1_ref[rows, :]
        h_ref[:, c * LANES:(c + 1) * LANES] = DEEPNORM_ALPHA * x1_ref[rows, :] + y
    o_ref[...] = _layer_norm(h_ref[...], g_ref[...], b_ref[...])


def _moe_combine(pos0, pos1, x1_slab, ext, ys, g, bb):
    n = ext.shape[0]
    d = g.shape[1]
    tm = next(c for c in (512, 256, 128, 64) if n % c == 0)
    return pl.pallas_call(
        _moe_combine_kernel,
        grid_spec=pltpu.PrefetchScalarGridSpec(
            num_scalar_prefetch=2, grid=(n // tm,),
            in_specs=[pl.BlockSpec((tm * SLAB_ROWS, LANES), lambda i, p0, p1: (i, 0)),
                      pl.BlockSpec((tm, LANES), lambda i, p0, p1: (i, 0)),
                      pl.BlockSpec(memory_space=pl.ANY),
                      pl.BlockSpec(g.shape, lambda i, p0, p1: (0, 0)),
                      pl.BlockSpec(bb.shape, lambda i, p0, p1: (0, 0))],
            out_specs=pl.BlockSpec((tm, d), lambda i, p0, p1: (i, 0)),
            scratch_shapes=[pltpu.VMEM((tm * SLAB_ROWS, LANES), F32), pltpu.VMEM((tm * SLAB_ROWS, LANES), F32),
                            pltpu.VMEM((tm, d), F32), pltpu.SemaphoreType.DMA((2,))]),
        out_shape=jax.ShapeDtypeStruct((n, d), F32),
        compiler_params=_cparams(("arbitrary",)),
        name="moe_combine",
    )(pos0, pos1, x1_slab, ext, ys, g, bb)


def _moe_src_kernel(pos0_ref, pos1_ref, src_ref, *, n, n_rows):
    def clear(p, c):
        src_ref[p] = 0
        return c

    lax.fori_loop(0, n_rows, clear, 0, unroll=8)

    def put(t, c):
        src_ref[pos0_ref[t]] = t
        src_ref[pos1_ref[t]] = t
        return c

    lax.fori_loop(0, n, put, 0, unroll=8)


def _moe_src(pos0, pos1, n_rows):
    n = pos0.shape[0]
    return pl.pallas_call(
        functools.partial(_moe_src_kernel, n=n, n_rows=n_rows),
        grid_spec=pltpu.PrefetchScalarGridSpec(
            num_scalar_prefetch=2, grid=(1,), in_specs=[],
            out_specs=pl.BlockSpec(memory_space=pltpu.SMEM)),
        out_shape=jax.ShapeDtypeStruct((n_rows,), jnp.int32),
        name="moe_src",
    )(pos0, pos1)


def _moe_ln2(x1_slab, ext, wg, wu, wd, layer, g, bb):
    n = ext.shape[0]
    tm = 512 if n >= 8192 else 256
    n_rows = 2 * n + N_EXPERTS * tm
    sel = (ext[:, EXT_SEL:EXT_SEL + N_EXPERTS] > 0.5).astype(jnp.int32)
    csum = jnp.cumsum(sel, axis=0)
    padded = (csum[-1] + tm - 1) // tm * tm
    seg_end = jnp.cumsum(padded)
    slot = (seg_end - padded)[None, :] + csum - sel
    experts = jnp.arange(N_EXPERTS, dtype=jnp.int32)[None, :]
    e_lo = ext[:, EXT_W_LO + 2].astype(jnp.int32)[:, None]
    e_hi = ext[:, EXT_W_LO + 3].astype(jnp.int32)[:, None]
    pos0 = jnp.sum(jnp.where(experts == e_lo, slot, 0), axis=1)
    pos1 = jnp.sum(jnp.where(experts == e_hi, slot, 0), axis=1)
    nact = seg_end[-1:] // tm
    tiles = jnp.arange(n_rows // tm, dtype=jnp.int32)
    first_row = jnp.minimum(tiles, nact[0] - 1) * tm
    tile_e = jnp.minimum(jnp.sum((seg_end[None, :] <= first_row[:, None]).astype(jnp.int32), axis=1), N_EXPERTS - 1)

    src = _moe_src(pos0, pos1, n_rows)
    ys = _moe_ffn(tile_e, nact, src, x1_slab, wg, wu, wd, tm, layer)
    return _moe_combine(pos0, pos1, x1_slab, ext, ys, g, bb)


def _band_bias_kernel(tab_ref, o_ref):
    size, width = tab_ref.shape[1], o_ref.shape[2]
    entry = lax.broadcasted_iota(jnp.int32, (size, width), 0)
    w = lax.broadcasted_iota(jnp.int32, (size, width), 1)
    tab = tab_ref[...]
    for l in range(o_ref.shape[1]):
        idx = jnp.clip(CB_PAST_ROWS + l - w, -REL_CLIP, REL_CLIP) + REL_CLIP
        o_ref[:, l, :] = _dot_hi(tab, jnp.where(entry == idx, 1.0, 0.0))


def _band_bias(rel_bias):
    depth, h, size = rel_bias.shape
    size_pad = -(-size // LANES) * LANES
    tab = jnp.pad(rel_bias.reshape(depth * h, size), ((0, 0), (0, size_pad - size)))
    width = CB_PAST_ROWS + CHUNK
    out = pl.pallas_call(
        _band_bias_kernel,
        out_shape=jax.ShapeDtypeStruct((depth * h, CHUNK, width), F32),
        name="band_bias",
    )(tab)
    return out.reshape(depth, h, CHUNK, width)


_HALF_ROPE = MLA_ROPE // 2
_SRC_KR = MLA_Q_RANK + MLA_KV_RANK
_SRC_GQKV = _SRC_KR + MLA_ROPE
_SRC_GZ = _SRC_GQKV + GDN_CONV_DIM
_SRC_GBA = _SRC_GZ + GDN_VAL_DIM
_SRC_CB = _SRC_GBA + 2 * GDN_HEADS
W_IN_SEGMENTS = (
    (C_GQKV, _SRC_GQKV, GDN_CONV_DIM), (C_GZ, _SRC_GZ, GDN_VAL_DIM), (C_CQ, 0, MLA_Q_RANK),
    (C_CKV, MLA_Q_RANK, MLA_KV_RANK), (C_KR, _SRC_KR, MLA_ROPE),
    (C_KR + MLA_ROPE, _SRC_KR + _HALF_ROPE, _HALF_ROPE), (C_KR + MLA_ROPE + _HALF_ROPE, _SRC_KR, _HALF_ROPE),
    (C_GBA, _SRC_GBA, 2 * GDN_HEADS), (C_CB, _SRC_CB, 3 * CB_DIM))


def _w_in_relayout_kernel(w_ref, o_ref):
    o_ref[0, :, C_GBA:C_GBA + LANES] = jnp.zeros((o_ref.shape[1], LANES), BF16)
    for dst, src, width in W_IN_SEGMENTS:
        o_ref[0, :, dst:dst + width] = w_ref[0, :, src:src + width].astype(BF16)


def _w_in_relayout(w_in):
    depth, d, width = w_in.shape
    tm = 256
    return pl.pallas_call(
        _w_in_relayout_kernel,
        grid=(depth, d // tm),
        in_specs=[pl.BlockSpec((1, tm, width), lambda l, i: (l, i, 0))],
        out_specs=pl.BlockSpec((1, tm, IN_PAD), lambda l, i: (l, i, 0)),
        out_shape=jax.ShapeDtypeStruct((depth, d, IN_PAD), BF16),
        compiler_params=_cparams(("parallel", "parallel")),
        name="w_in_relayout",
    )(w_in)


def _prep_layer(q_norm_g, w_uq, kv_norm_g, w_uk, w_uv, conv_w, a_log, dt_bias, gdn_norm_g, rel_bias, w_out,
                ln1_g, ln1_b, ln2_g, ln2_b):
    half = MLA_ROPE // 2
    r = w_uq.shape[0]
    wq_nope = w_uq[:, :, :MLA_NOPE].reshape(r, MLA_HEADS * MLA_NOPE).astype(BF16)
    wq_r = w_uq[:, :, MLA_NOPE:]
    wq_rope = wq_r.reshape(r, MLA_HEADS * MLA_ROPE).astype(BF16)
    wq_rope_sw = jnp.concatenate([wq_r[..., half:], wq_r[..., :half]], -1).reshape(r, MLA_HEADS * MLA_ROPE).astype(BF16)
    wuk_t = jnp.transpose(w_uk, (1, 2, 0)).astype(BF16)
    wuv = jnp.transpose(w_uv, (1, 0, 2)).astype(BF16)
    lane_pad = lambda a: jnp.pad(a, (GDN_HEADS, LANES - 2 * GDN_HEADS))[None, :]
    bias = rel_bias
    w_out_b = w_out.astype(BF16)
    na = MLA_HEADS * MLA_V
    return dict(
        q_norm_g=q_norm_g[None, :], kv_norm_g=kv_norm_g[None, :],
        wq_nope=wq_nope, wq_rope=wq_rope, wq_rope_sw=wq_rope_sw, wuk_t=wuk_t, wuv=wuv,
        conv_w=conv_w, alog=lane_pad(a_log), dtb=lane_pad(dt_bias), gnorm=gdn_norm_g[None, :], bias=bias,
        wo_a=w_out_b[:na], wo_b=w_out_b[na:na + GDN_VAL_DIM], wo_c=w_out_b[na + GDN_VAL_DIM:],
        ln1_g=ln1_g[None, :], ln1_b=ln1_b[None, :], ln2_g=ln2_g[None, :], ln2_b=ln2_b[None, :])


def _rope_tables(start, t):
    pos = start + jnp.arange(t, dtype=jnp.int32)
    inv = ROPE_THETA ** (-jnp.arange(0, MLA_ROPE, 2, dtype=F32) / MLA_ROPE)
    ang = pos.astype(F32)[:, None] * inv[None, :]
    cos, sin = jnp.cos(ang), jnp.sin(ang)
    return jnp.concatenate([cos, cos], -1), jnp.concatenate([-sin, sin], -1)


def _mixers(x2d, row0, b, t, p, layer, shared, caches):
    proj = _in_proj(x2d, row0, b * t, shared["w_in"], layer).reshape(b, t, IN_PAD)

    start = 0 if caches is None else caches["ckv"].shape[1]
    cos2, sin2 = _rope_tables(start, t)
    q, ckv_new, krope_new = _mla_prep(proj, cos2, sin2, p["q_norm_g"], p["kv_norm_g"], p["wq_nope"], p["wq_rope"],
                                      p["wq_rope_sw"], p["wuk_t"])
    if caches is None:
        o_a = _mla_attn(q, ckv_new, krope_new, p["wuv"], 0)
        s_past = jnp.zeros((b, GDN_HEADS, GDN_DK, GDN_DV), F32)
        conv_past = jnp.zeros((b, GDN_CONV - 1, GDN_CONV_DIM), F32)
        o_c = _cb_attn(proj, None, None, layer, p["bias"])
    else:
        o_a = _mla_attn_cached(q, caches["ckv"], caches["krope"], layer, ckv_new, krope_new, p["wuv"])
        s_past, conv_past = caches["gdn"][layer], caches["conv"][layer]
        o_c = _cb_attn(proj, caches["cb_k"], caches["cb_v"], layer, p["bias"])
    o_b, s_new, conv_new = _gdn(proj, conv_past, p["conv_w"], p["alog"], p["dtb"], p["gnorm"], s_past)

    keep = min(CB_PAST_ROWS, t)
    cb_new = lambda col: proj[:, t - keep:, col:col + CB_DIM].reshape(b, keep, CB_HEADS, CB_DH)
    state = (ckv_new, krope_new, s_new, conv_new, cb_new(C_CB + CB_DIM), cb_new(C_CB + 2 * CB_DIM))
    return tuple(o.reshape(b * t, -1) for o in (o_a, o_b, o_c)), state


def kernel(x_prompt, x_sample, cache_mla_ckv, cache_mla_krope, state_gdn, state_gdn_conv, cache_cb_k, cache_cb_v,
           w_in, q_norm_g, w_uq, kv_norm_g, w_uk, w_uv, conv_w, a_log, dt_bias, gdn_norm_g, rel_bias, w_out,
           ln1_g, ln1_b, router_w, router_b, w_gate, w_up, w_down, ln2_g, ln2_b):
    depth = w_in.shape[0]
    band_bias = _band_bias(rel_bias)
    layers = [_prep_layer(q_norm_g[l], w_uq[l], kv_norm_g[l], w_uk[l], w_uv[l], conv_w[l], a_log[l],
                          dt_bias[l], gdn_norm_g[l], band_bias[l], w_out[l], ln1_g[l], ln1_b[l],
                          ln2_g[l], ln2_b[l]) for l in range(depth)]
    stack = lambda w: w.reshape((depth * N_EXPERTS,) + w.shape[2:])
    shared = dict(w_in=_w_in_relayout(w_in), rwt=router_w.T, rb=router_b[:, None], w_gate=stack(w_gate), w_up=stack(w_up), w_down=stack(w_down))
    merge = lambda a: a.reshape((a.shape[0] * a.shape[1],) + a.shape[2:])
    rows = cache_cb_k.shape[2]
    caches = dict(ckv=merge(cache_mla_ckv), krope=merge(cache_mla_krope), gdn=state_gdn, conv=state_gdn_conv,
                  cb_k=merge(cache_cb_k).reshape(-1, rows, CB_DIM), cb_v=merge(cache_cb_v).reshape(-1, rows, CB_DIM))

    (bp, tp, d), (bs, ts, _) = x_prompt.shape, x_sample.shape
    n_p, n_s = bp * tp, bs * ts
    x_p, row0_p, x_s, row0_s = x_prompt.reshape(n_p, d), 0, x_sample.reshape(n_s, d), 0
    new_p, new_s = ([], [], [], [], [], []), ([], [], [], [], [], [])
    for l in range(depth):
        p = layers[l]
        mix_p, st_p = _mixers(x_p, row0_p, bp, tp, p, l, shared, None)
        mix_s, st_s = _mixers(x_s, row0_s, bs, ts, p, l, shared, caches)
        x1, ext = _out_ln1(mix_p, x_p, row0_p, mix_s, x_s, row0_s, p["wo_a"], p["wo_b"], p["wo_c"],
                           p["ln1_g"], p["ln1_b"], shared["rwt"], shared["rb"])
        x2 = _moe_ln2(x1, ext, shared["w_gate"], shared["w_up"], shared["w_down"], l, p["ln2_g"], p["ln2_b"])
        x_p, row0_p, x_s, row0_s = x2, 0, x2, n_p
        for new, st in ((new_p, st_p), (new_s, st_s)):
            for lst, a in zip(new, st):
                lst.append(a)
    y_prompt = x2[:n_p].reshape(bp, tp, d)
    y_sample = x2[n_p:].reshape(bs, ts, d)
    return (y_prompt, y_sample, *[jnp.stack(a) for a in new_p], *[jnp.stack(a) for a in new_s])
```
